```python
import math
import jax
import jax.numpy as jnp
from jax import lax
import numpy as np

D_MODEL = 1024
BATCH = 8
SEQ = 4096
DEPTH = 4

GRID_W = 64
CTX_LEN = 256
F32 = jnp.float32
ALPHA = (2.0 * DEPTH) ** 0.25
BETA = (8.0 * DEPTH) ** -0.25
LN_EPS = 1e-5
NORM_EPS = 1e-6
NEG = -1e30
GLA_HEADS = 4
GLA_DK = 64
GLA_DV = 128
GLA_RANK = 16
GLA_TAU = 16.0
GLA_CHUNK = 64
NA_HEADS = 8
NA_DH = 64
NA_KH = 8
NA_KW = 16
ML_HEADS = 4
ML_DQK = 64
ML_DV = 128
ML_CHUNK = 64
SW_HEADS = 8
SW_KV_HEADS = 2
SW_DH = 64
SW_WINDOW = 128
SW_BLOCK = 128
ROPE_BASE = 10000.0

GLA_W = GLA_HEADS * GLA_DV
NA_W = NA_HEADS * NA_DH
ML_W = ML_HEADS * ML_DV
SW_W = SW_HEADS * SW_DH
EVEN_SPLITS = (GLA_HEADS * GLA_DK, GLA_HEADS * GLA_DK, GLA_W, GLA_W, 2 * GLA_RANK, NA_W, NA_W, NA_W, NA_W)
ODD_SPLITS = (ML_HEADS * ML_DQK, ML_HEADS * ML_DQK, ML_W, ML_W, 4 * ML_HEADS, SW_W, SW_KV_HEADS * SW_DH, SW_KV_HEADS * SW_DH, SW_W)
EVEN_IN = sum(EVEN_SPLITS)
ODD_IN = sum(ODD_SPLITS)
EVEN_OUT = GLA_W + NA_W
ODD_OUT = ML_W + SW_W
N_EVEN = (DEPTH + 1) // 2
N_ODD = DEPTH // 2

kernel_name = 'hybrid_gla_natten_mlstm_swa_block'


def split_cols(a, sizes):
    idx = [int(i) for i in np.cumsum(sizes)[:-1]]
    return jnp.split(a, idx, axis=-1)


def heads(a, h):
    b, t, _ = a.shape
    return a.reshape(b, t, h, -1).transpose(0, 2, 1, 3)


def merge(a):
    b, h, t, d = a.shape
    return a.transpose(0, 2, 1, 3).reshape(b, t, h * d)


def layer_norm(x, g, b):
    xf = x.astype(F32)
    mu = jnp.mean(xf, axis=-1, keepdims=True)
    var = jnp.mean(jnp.square(xf - mu), axis=-1, keepdims=True)
    return ((xf - mu) * lax.rsqrt(var + LN_EPS)).astype(x.dtype) * g + b


def head_rmsnorm(o, g):
    of = o.astype(F32)
    of = of * lax.rsqrt(jnp.mean(jnp.square(of), axis=-1, keepdims=True) + NORM_EPS)
    return merge(of.astype(o.dtype)) * g


def to_chunks(a, c):
    s = a.shape
    return jnp.moveaxis(a.reshape(s[:2] + (s[2] // c, c) + s[3:]), 2, 0)


def from_chunks(a):
    a = jnp.moveaxis(a, 0, 2)
    s = a.shape
    return a.reshape(s[:2] + (s[2] * s[3],) + s[4:])


def flip_t(args):
    return tuple(jnp.flip(a, axis=2) for a in args)


def prefix_scan(scan_fn, lat, ctx, state0):
    o_ctx, st = scan_fn(*ctx, state0)
    o_lat, _ = scan_fn(*lat, st)
    return o_lat, o_ctx


def bidir_prefix(scan_fn, lat_f, ctx_f, lat_b, ctx_b, state0):
    lf, cf = prefix_scan(scan_fn, lat_f, ctx_f, state0)
    lb, cb = prefix_scan(scan_fn, flip_t(lat_b), flip_t(ctx_b), state0)
    return lf + jnp.flip(lb, axis=2), cf + jnp.flip(cb, axis=2)


def gla_scan(q, k, v, g, s0):
    out_dtype = v.dtype
    qs, ks, vs, gs = [to_chunks(a.astype(F32), GLA_CHUNK) for a in (q, k, v, g)]
    tril = jnp.tril(jnp.ones((GLA_CHUNK, GLA_CHUNK), bool))

    def body(S, inp):
        qc, kc, vc, gc = inp
        b = jnp.cumsum(gc, axis=2)
        b_last = b[:, :, -1:, :]
        q_in = qc * jnp.exp(b)
        k_in = kc * jnp.exp(-b)
        a = jnp.where(tril, jnp.einsum('bhtd,bhsd->bhts', q_in, k_in), 0.0)
        o = jnp.einsum('bhts,bhsv->bhtv', a, vc) + jnp.einsum('bhtd,bhdv->bhtv', q_in, S)
        S = jnp.exp(b_last)[:, :, 0, :, None] * S + jnp.einsum('bhsd,bhsv->bhdv', kc * jnp.exp(b_last - b), vc)
        return S, o

    S, o = lax.scan(body, s0, (qs, ks, vs, gs))
    return from_chunks(o).astype(out_dtype), S


def mlstm_scan(q, k, v, ig, lf, state):
    out_dtype = v.dtype
    qs, ks, vs, is_, fs = [to_chunks(a.astype(F32), ML_CHUNK) for a in (q, k, v, ig, lf)]
    tril = jnp.tril(jnp.ones((ML_CHUNK, ML_CHUNK), bool))

    def body(st, inp):
        cm, n, m = st
        qc, kc, vc, ic, fc = inp
        b = jnp.cumsum(fc, axis=-1)
        dmat = jnp.where(tril, b[..., :, None] - b[..., None, :] + ic[..., None, :], NEG)
        inter = b + m[..., None]
        m_t = jnp.maximum(jnp.max(dmat, axis=-1), inter)
        w = jnp.exp(dmat - m_t[..., None])
        w_int = jnp.exp(inter - m_t)
        s = jnp.einsum('bhtd,bhsd->bhts', qc, kc) * w
        num = s @ vc + w_int[..., None] * jnp.einsum('bhtd,bhdv->bhtv', qc, cm)
        den = jnp.sum(s, axis=-1) + w_int * jnp.einsum('bhtd,bhd->bht', qc, n)
        h = num / jnp.maximum(jnp.abs(den), jnp.exp(-m_t))[..., None]
        b_last = b[..., -1]
        ws_log = b_last[..., None] - b + ic
        m_new = jnp.maximum(b_last + m, jnp.max(ws_log, axis=-1))
        dec = jnp.exp(b_last + m - m_new)
        ws = jnp.exp(ws_log - m_new[..., None])
        cm = dec[..., None, None] * cm + jnp.einsum('bhs,bhsd,bhsv->bhdv', ws, kc, vc)
        n = dec[..., None] * n + jnp.einsum('bhs,bhsd->bhd', ws, kc)
        return (cm, n, m_new), h

    st, h = lax.scan(body, state, (qs, ks, vs, is_, fs))
    return from_chunks(h).astype(out_dtype), st


def axial_rope(x, rows, cols):
    half = x.shape[-1] // 2
    nf = half // 2
    freqs = ROPE_BASE ** (-jnp.arange(nf, dtype=F32) / nf)

    def rot(xp, pos):
        ang = pos[:, None] * freqs[None, :]
        cos = jnp.cos(ang).astype(x.dtype)
        sin = jnp.sin(ang).astype(x.dtype)
        x1, x2 = xp[..., :nf], xp[..., nf:]
        return jnp.concatenate([x1 * cos - x2 * sin, x1 * sin + x2 * cos], axis=-1)

    return jnp.concatenate([rot(x[..., :half], rows), rot(x[..., half:], cols)], axis=-1)


def dense_attn(q, k, v, sink=None):
    scale = q.shape[-1] ** -0.5
    s = jnp.einsum('bkgld,bksd->bkgls', q, k).astype(F32) * scale
    if sink is None:
        p = jax.nn.softmax(s, axis=-1)
    else:
        s_sink = jnp.broadcast_to(sink.astype(F32)[None, :, :, None, None], s.shape[:-1] + (1,))
        p = jax.nn.softmax(jnp.concatenate([s, s_sink], axis=-1), axis=-1)[..., :-1]
    return jnp.einsum('bkgls,bksd->bkgld', p.astype(v.dtype), v)


def neighbourhood_attn(q, k, v, kc, vc, rpb):
    b, h, t, dh = q.shape
    rows_n = t // GRID_W
    kh = min(NA_KH, rows_n)
    qg, kg, vg = [a.reshape(b, h, rows_n, GRID_W, dh) for a in (q, k, v)]
    cols = jnp.arange(GRID_W)
    cs = jnp.clip(cols - NA_KW // 2, 0, GRID_W - NA_KW)
    col_ok = (cols[None, :] >= cs[:, None]) & (cols[None, :] < cs[:, None] + NA_KW)
    dc_idx = jnp.clip(cols[None, :] - cols[:, None] + NA_KW - 1, 0, 2 * NA_KW - 2)
    scale = dh ** -0.5
    n_lat = kh * GRID_W

    def row(r):
        rs = jnp.clip(r - kh // 2, 0, rows_n - kh)
        q_r = lax.dynamic_index_in_dim(qg, r, axis=2, keepdims=False)
        k_blk = lax.dynamic_slice_in_dim(kg, rs, kh, axis=2)
        v_blk = lax.dynamic_slice_in_dim(vg, rs, kh, axis=2)
        dr_idx = rs + jnp.arange(kh) - r + NA_KH - 1
        bias = rpb[:, dr_idx[:, None, None], dc_idx[None, :, :]].transpose(0, 2, 1, 3)
        s_lat = jnp.einsum('bhqd,bhiwd->bhqiw', q_r, k_blk).astype(F32) * scale + bias.astype(F32)[None]
        s_lat = jnp.where(col_ok[:, None, :], s_lat, NEG).reshape(b, h, GRID_W, n_lat)
        s_ctx = jnp.einsum('bhqd,bhcd->bhqc', q_r, kc).astype(F32) * scale
        p = jax.nn.softmax(jnp.concatenate([s_lat, s_ctx], axis=-1), axis=-1)
        p_lat = p[..., :n_lat].reshape(b, h, GRID_W, kh, GRID_W).astype(v.dtype)
        p_ctx = p[..., n_lat:].astype(v.dtype)
        return jnp.einsum('bhqiw,bhiwd->bhqd', p_lat, v_blk) + jnp.einsum('bhqc,bhcd->bhqd', p_ctx, vc)

    o = lax.map(row, jnp.arange(rows_n))
    return jnp.moveaxis(o, 0, 2).reshape(b, h, t, dh)


def swa_attn(q, k, v, kc, vc, sink):
    b, h, t, dh = q.shape
    kv = k.shape[1]
    g = h // kv
    n_ctx = kc.shape[2]
    qg = q.reshape(b, kv, g, t, dh)
    nb = t // SW_BLOCK
    span = SW_BLOCK + 2 * SW_WINDOW
    pad = ((0, 0), (0, 0), (SW_WINDOW, SW_WINDOW), (0, 0))
    k_pad = jnp.pad(k, pad)
    v_pad = jnp.pad(v, pad)
    sink_f = sink.reshape(kv, g).astype(F32)
    scale = dh ** -0.5

    def block(n):
        start = n * SW_BLOCK
        q_b = lax.dynamic_slice_in_dim(qg, start, SW_BLOCK, axis=3)
        k_w = lax.dynamic_slice_in_dim(k_pad, start, span, axis=2)
        v_w = lax.dynamic_slice_in_dim(v_pad, start, span, axis=2)
        qpos = start + jnp.arange(SW_BLOCK)
        kpos = start - SW_WINDOW + jnp.arange(span)
        ok = (jnp.abs(qpos[:, None] - kpos[None, :]) <= SW_WINDOW) & (kpos[None, :] >= 0) & (kpos[None, :] < t)
        s_lat = jnp.where(ok, jnp.einsum('bkgqd,bksd->bkgqs', q_b, k_w).astype(F32) * scale, NEG)
        s_ctx = jnp.einsum('bkgqd,bkcd->bkgqc', q_b, kc).astype(F32) * scale
        s_sink = jnp.broadcast_to(sink_f[None, :, :, None, None], s_ctx.shape[:-1] + (1,))
        p = jax.nn.softmax(jnp.concatenate([s_lat, s_ctx, s_sink], axis=-1), axis=-1).astype(v.dtype)
        return (jnp.einsum('bkgqs,bksd->bkgqd', p[..., :span], v_w)
                + jnp.einsum('bkgqc,bkcd->bkgqd', p[..., span:span + n_ctx], vc))

    o = lax.map(block, jnp.arange(nb))
    return jnp.moveaxis(o, 0, 3).reshape(b, h, t, dh)


def mixer_even(h, hc, w_in, w_out, gla_gate_w, gla_gate_b, gla_norm_g, na_rpb, need_ctx):
    pl = split_cols(h @ w_in, EVEN_SPLITS)
    pc = split_cols(hc @ w_in, EVEN_SPLITS)
    bsz = h.shape[0]

    def gla_inputs(parts):
        qa, ka, va, _, lra = parts[:5]
        q = heads(qa, GLA_HEADS) * (GLA_DK ** -0.5)
        k = heads(ka, GLA_HEADS)
        v = heads(va, GLA_HEADS)
        lr = lra.reshape(lra.shape[:-1] + (2, GLA_RANK))
        z = jnp.einsum('btdr,drk->btdk', lr, gla_gate_w) + gla_gate_b
        g = jax.nn.log_sigmoid(z.astype(F32)) / GLA_TAU
        return (q, k, v, heads(g[:, :, 0], GLA_HEADS)), (q, k, v, heads(g[:, :, 1], GLA_HEADS))

    lat_f, lat_b = gla_inputs(pl)
    ctx_f, ctx_b = gla_inputs(pc)
    s0 = jnp.zeros((bsz, GLA_HEADS, GLA_DK, GLA_DV), F32)
    oa_lat, oa_ctx = bidir_prefix(gla_scan, lat_f, ctx_f, lat_b, ctx_b, s0)
    ya = head_rmsnorm(oa_lat, gla_norm_g) * jax.nn.silu(pl[3])

    kbc = heads(pc[6], NA_HEADS)
    vbc = heads(pc[7], NA_HEADS)
    ob = neighbourhood_attn(heads(pl[5], NA_HEADS), heads(pl[6], NA_HEADS), heads(pl[7], NA_HEADS), kbc, vbc, na_rpb)
    yb = merge(ob) * jax.nn.silu(pl[8])
    y_lat = jnp.concatenate([ya, yb], axis=-1) @ w_out
    if not need_ctx:
        return y_lat, None
    ya_c = head_rmsnorm(oa_ctx, gla_norm_g) * jax.nn.silu(pc[3])
    ob_c = dense_attn(heads(pc[5], NA_HEADS)[:, :, None], kbc, vbc)[:, :, 0]
    yb_c = merge(ob_c) * jax.nn.silu(pc[8])
    y_ctx = jnp.concatenate([ya_c, yb_c], axis=-1) @ w_out
    return y_lat, y_ctx


def mixer_odd(h, hc, w_in, w_out, ml_gate_b, ml_norm_g, sw_sink, need_ctx):
    pl = split_cols(h @ w_in, ODD_SPLITS)
    pc = split_cols(hc @ w_in, ODD_SPLITS)
    bsz, t, _ = h.shape

    def mlstm_inputs(parts):
        qm, km, vm, _, gates = parts[:5]
        q = heads(qm, ML_HEADS)
        k = heads(km, ML_HEADS) * (ML_DQK ** -0.5)
        v = heads(vm, ML_HEADS)
        gp = gates.reshape(gates.shape[:2] + (2, 2, ML_HEADS)).astype(F32) + ml_gate_b.astype(F32)
        gp = gp.transpose(2, 3, 0, 4, 1)
        return ((q, k, v, gp[0, 0], jax.nn.log_sigmoid(gp[0, 1])),
                (q, k, v, gp[1, 0], jax.nn.log_sigmoid(gp[1, 1])))

    lat_f, lat_b = mlstm_inputs(pl)
    ctx_f, ctx_b = mlstm_inputs(pc)
    st0 = (jnp.zeros((bsz, ML_HEADS, ML_DQK, ML_DV), F32), jnp.zeros((bsz, ML_HEADS, ML_DQK), F32),
           jnp.zeros((bsz, ML_HEADS), F32))
    oc_lat, oc_ctx = bidir_prefix(mlstm_scan, lat_f, ctx_f, lat_b, ctx_b, st0)
    yc = head_rmsnorm(oc_lat, ml_norm_g) * jax.nn.silu(pl[3])

    pos = jnp.arange(t)
    rows = (pos // GRID_W).astype(F32)
    cols = (pos % GRID_W).astype(F32)
    qd = axial_rope(heads(pl[5], SW_HEADS), rows, cols)
    kd = axial_rope(heads(pl[6], SW_KV_HEADS), rows, cols)
    vd = heads(pl[7], SW_KV_HEADS)
    kdc = heads(pc[6], SW_KV_HEADS)
    vdc = heads(pc[7], SW_KV_HEADS)
    od = swa_attn(qd, kd, vd, kdc, vdc, sw_sink)
    yd = merge(od) * jax.nn.silu(pl[8])
    y_lat = jnp.concatenate([yc, yd], axis=-1) @ w_out
    if not need_ctx:
        return y_lat, None
    yc_c = head_rmsnorm(oc_ctx, ml_norm_g) * jax.nn.silu(pc[3])
    n_ctx = hc.shape[1]
    qdc = heads(pc[5], SW_HEADS).reshape(bsz, SW_KV_HEADS, SW_HEADS // SW_KV_HEADS, n_ctx, SW_DH)
    od_c = dense_attn(qdc, kdc, vdc, sw_sink.reshape(SW_KV_HEADS, SW_HEADS // SW_KV_HEADS))
    yd_c = merge(od_c.reshape(bsz, SW_HEADS, n_ctx, SW_DH)) * jax.nn.silu(pc[8])
    y_ctx = jnp.concatenate([yc_c, yd_c], axis=-1) @ w_out
    return y_lat, y_ctx


def setup_inputs(seed: int = 0) -> dict:
    key = jax.random.key(seed)
    ks = jax.random.split(key, 20)

    def nrm(k, shape, scale):
        return jax.random.normal(k, shape, F32) * scale

    ml_base = jnp.stack([jnp.full((ML_HEADS,), -1.0, F32), jnp.linspace(3.0, 6.0, ML_HEADS, dtype=F32)])
    return {
        'x': nrm(ks[0], (BATCH, SEQ, D_MODEL), 1.0),
        'c': nrm(ks[1], (BATCH, D_MODEL), 1.0),
        'ctx': nrm(ks[2], (BATCH, CTX_LEN, D_MODEL), 1.0),
        'c_ctx': nrm(ks[3], (D_MODEL,), 1.0),
        'w_ada': nrm(ks[4], (DEPTH, D_MODEL, 3 * D_MODEL), 0.5 * D_MODEL ** -0.5),
        'b_ada': nrm(ks[5], (DEPTH, 3 * D_MODEL), 0.02),
        'ln_g': 1.0 + nrm(ks[6], (DEPTH, D_MODEL), 0.02),
        'ln_b': nrm(ks[7], (DEPTH, D_MODEL), 0.02),
        'w_in_even': nrm(ks[8], (N_EVEN, D_MODEL, EVEN_IN), D_MODEL ** -0.5),
        'w_out_even': nrm(ks[9], (N_EVEN, EVEN_OUT, D_MODEL), BETA * EVEN_OUT ** -0.5),
        'gla_gate_w': nrm(ks[10], (N_EVEN, 2, GLA_RANK, GLA_HEADS * GLA_DK), GLA_RANK ** -0.5),
        'gla_gate_b': 2.0 + nrm(ks[11], (N_EVEN, 2, GLA_HEADS * GLA_DK), 0.1),
        'gla_norm_g': 1.0 + nrm(ks[12], (N_EVEN, GLA_W), 0.02),
        'na_rpb': nrm(ks[13], (N_EVEN, NA_HEADS, 2 * NA_KH - 1, 2 * NA_KW - 1), 0.1),
        'w_in_odd': nrm(ks[14], (N_ODD, D_MODEL, ODD_IN), D_MODEL ** -0.5),
        'w_out_odd': nrm(ks[15], (N_ODD, ODD_OUT, D_MODEL), BETA * ODD_OUT ** -0.5),
        'ml_gate_b': ml_base[None, None] + nrm(ks[16], (N_ODD, 2, 2, ML_HEADS), 0.1),
        'ml_norm_g': 1.0 + nrm(ks[17], (N_ODD, ML_W), 0.02),
        'sw_sink': nrm(ks[18], (N_ODD, SW_HEADS), 0.5),
    }


def reference(x, c, ctx, c_ctx, w_ada, b_ada, ln_g, ln_b, w_in_even, w_out_even, gla_gate_w, gla_gate_b,
              gla_norm_g, na_rpb, w_in_odd, w_out_odd, ml_gate_b, ml_norm_g, sw_sink):
    cond = jax.nn.silu(c)
    cond_ctx = jax.nn.silu(c_ctx)
    cx = ctx
    for l in range(DEPTH):
        need_ctx = l < DEPTH - 1
        shift, scale, gate = jnp.split(cond @ w_ada[l] + b_ada[l], 3, axis=-1)
        shift_c, scale_c, gate_c = jnp.split(cond_ctx @ w_ada[l] + b_ada[l], 3, axis=-1)
        h = x * (1.0 + scale[:, None]) + shift[:, None]
        hc = cx * (1.0 + scale_c) + shift_c
        i = l // 2
        if l % 2 == 0:
            y, yc = mixer_even(h, hc, w_in_even[i], w_out_even[i], gla_gate_w[i], gla_gate_b[i],
                               gla_norm_g[i], na_rpb[i], need_ctx)
        else:
            y, yc = mixer_odd(h, hc, w_in_odd[i], w_out_odd[i], ml_gate_b[i], ml_norm_g[i], sw_sink[i], need_ctx)
        x = layer_norm(ALPHA * x + gate[:, None] * y, ln_g[l], ln_b[l])
        if need_ctx:
            cx = layer_norm(ALPHA * cx + gate_c * yc, ln_g[l], ln_b[l])
    return x
```

```python
import functools

import numpy as np
import jax
import jax.numpy as jnp
from jax import lax
from jax.experimental import pallas as pl
from jax.experimental.pallas import tpu as pltpu

F32 = jnp.float32
BF16 = jnp.bfloat16

D_MODEL = 1024
SEQ = 4096
DEPTH = 4
GRID_W = 64
CTX_LEN = 256
S_ALL = CTX_LEN + SEQ
ALPHA = (2.0 * DEPTH) ** 0.25
LN_EPS = 1e-5
NORM_EPS = 1e-6
NEG = -1e30
GLA_HEADS, GLA_DK, GLA_DV, GLA_RANK, GLA_TAU = 4, 64, 128, 16, 16.0
NA_HEADS, NA_DH, NA_KH, NA_KW = 8, 64, 8, 16
ML_HEADS, ML_DQK, ML_DV = 4, 64, 128
SW_HEADS, SW_KV_HEADS, SW_DH, SW_WINDOW, SW_BLOCK = 8, 2, 64, 128, 128
ROPE_BASE = 10000.0
CHUNK = 64
N_CHUNKS = S_ALL // CHUNK
CTX_CHUNKS = CTX_LEN // CHUNK
ROW_TILE = 256
N_ROW_TILES = S_ALL // ROW_TILE
GRID_ROWS = SEQ // GRID_W

VMEM_LIMIT = 56 * 1024 * 1024

NT_DIMS = (((1,), (1,)), ((), ()))
TN_DIMS = (((0,), (0,)), ((), ()))


def _dot(a, b):
    return jnp.dot(a, b, preferred_element_type=F32)


def _dot_nt(a, b):
    return lax.dot_general(a, b, NT_DIMS, preferred_element_type=F32)


def _dot_tn(a, b):
    return lax.dot_general(a, b, TN_DIMS, preferred_element_type=F32)


def _sigmoid(x):
    return 1.0 / (1.0 + jnp.exp(-x))


def _silu(x):
    return x * _sigmoid(x)


def _log_sigmoid(x):
    return jnp.minimum(x, 0.0) - jnp.log1p(jnp.exp(-jnp.abs(x)))


def _cumsum_rows(x, reverse):
    n = x.shape[0]
    row = lax.broadcasted_iota(jnp.int32, x.shape, 0)
    k = 1
    while k < n:
        if reverse:
            x = x + jnp.where(row < n - k, pltpu.roll(x, n - k, axis=0), 0.0)
        else:
            x = x + jnp.where(row >= k, pltpu.roll(x, k, axis=0), 0.0)
        k *= 2
    return x


def _split_dot(mat_bf16, x):
    hi = x.astype(BF16)
    lo = (x - hi.astype(F32)).astype(BF16)
    return _dot(mat_bf16, hi) + _dot(mat_bf16, lo)


def _split_dot_r(x, mat_bf16):
    hi = x.astype(BF16)
    lo = (x - hi.astype(F32)).astype(BF16)
    return _dot(hi, mat_bf16) + _dot(lo, mat_bf16)


def _params(*sem):
    return pltpu.CompilerParams(dimension_semantics=sem, vmem_limit_bytes=VMEM_LIMIT)


def _ada_kernel(c_ref, w_ref, b_ref, o_ref):
    cond = _silu(c_ref[...])
    o_ref[0] = jnp.dot(cond, w_ref[0], precision=lax.Precision.HIGHEST,
                       preferred_element_type=F32) + b_ref[0]


def _ada(cvec, w_ada, b_ada):
    rows = cvec.shape[0]
    nblk = 3
    return pl.pallas_call(
        _ada_kernel,
        out_shape=jax.ShapeDtypeStruct((DEPTH, rows, 3 * D_MODEL), F32),
        grid=(DEPTH, nblk),
        in_specs=[
            pl.BlockSpec((rows, D_MODEL), lambda l, n: (0, 0)),
            pl.BlockSpec((1, D_MODEL, D_MODEL), lambda l, n: (l, 0, n)),
            pl.BlockSpec((1, 1, D_MODEL), lambda l, n: (l, 0, n)),
        ],
        out_specs=pl.BlockSpec((1, rows, D_MODEL), lambda l, n: (l, 0, n)),
        compiler_params=_params("arbitrary", "arbitrary"),
        name="ada_mod",
    )(cvec, w_ada, b_ada.reshape(DEPTH, 1, 3 * D_MODEL))


def _modulated(x_ref, mod_ref):
    shift = mod_ref[0, 0, 0:1, :]
    scale = mod_ref[0, 0, 1:2, :]
    return (x_ref[0] * (1.0 + scale) + shift).astype(BF16)


EVEN_SLABS = (256, 256, 512, 512, 512, 512, 512, 512)


def _inproj_even_kernel(x_ref, mod_ref, w_ref, *outs):
    h = _modulated(x_ref, mod_ref)
    off = 0
    for ref, width in zip(outs[:-1], EVEN_SLABS):
        ref[0] = _dot(h, w_ref[:, off:off + width]).astype(ref.dtype)
        off += width
    lr = _dot(h, w_ref[:, off:off + 128])
    outs[-1][0] = lr[:, :2 * GLA_RANK]


ODD_SLABS = (256, 256, 512, 512)


def _inproj_odd_kernel(x_ref, mod_ref, w_ref, cos_ref, sin_ref, gb_ref,
                       mq, mk, mv, mg, sq, sk, sv, sg, gates):
    h = _modulated(x_ref, mod_ref)
    off = 0
    for ref, width in zip((mq, mk, mv, mg), ODD_SLABS):
        ref[0] = _dot(h, w_ref[:, off:off + width]).astype(ref.dtype)
        off += width
    cos = cos_ref[...]
    sin = sin_ref[...]
    for t in range(4):
        a = _dot(h, w_ref[:, off + t * 128:off + (t + 1) * 128])
        b = _dot(h, w_ref[:, off + 512 + t * 128:off + 512 + (t + 1) * 128])
        sq[0, :, t * 128:(t + 1) * 128] = (a * cos + b * sin).astype(BF16)
    off += 1024
    a = _dot(h, w_ref[:, off:off + 128])
    b = _dot(h, w_ref[:, off + 128:off + 256])
    sk[0] = (a * cos + b * sin).astype(BF16)
    off += 256
    sv[0] = _dot(h, w_ref[:, off:off + 128]).astype(BF16)
    off += 128
    sg[0] = _dot(h, w_ref[:, off:off + 512]).astype(BF16)
    off += 512
    g = _dot(h, w_ref[:, off:off + 128])
    gates[0] = g[:, :4 * ML_HEADS] + gb_ref[...]


def _mod_index(b, j):
    return (b, jnp.where(j == 0, 1, 0), 0, 0)


def _inproj_even(xc, mod, w):
    bsz = xc.shape[0]
    ntot = w.shape[1]
    out_shape = [jax.ShapeDtypeStruct((bsz, S_ALL, wd), BF16) for wd in EVEN_SLABS]
    out_shape.append(jax.ShapeDtypeStruct((bsz, S_ALL, 2 * GLA_RANK), F32))
    out_specs = [pl.BlockSpec((1, ROW_TILE, wd), lambda b, j: (b, j, 0)) for wd in EVEN_SLABS]
    out_specs.append(pl.BlockSpec((1, ROW_TILE, 2 * GLA_RANK), lambda b, j: (b, j, 0)))
    return pl.pallas_call(
        _inproj_even_kernel,
        out_shape=out_shape,
        grid=(bsz, N_ROW_TILES),
        in_specs=[
            pl.BlockSpec((1, ROW_TILE, D_MODEL), lambda b, j: (b, j, 0)),
            pl.BlockSpec((1, 1, 3, D_MODEL), _mod_index),
            pl.BlockSpec((D_MODEL, ntot), lambda b, j: (0, 0)),
        ],
        out_specs=out_specs,
        compiler_params=_params("parallel", "arbitrary"),
        name="inproj_even",
    )(xc, mod, w)


def _inproj_odd(xc, mod, w, cos_t, sin_t, gate_b):
    bsz = xc.shape[0]
    ntot = w.shape[1]
    widths = (256, 256, 512, 512, 512, 128, 128, 512)
    out_shape = [jax.ShapeDtypeStruct((bsz, S_ALL, wd), BF16) for wd in widths]
    out_shape.append(jax.ShapeDtypeStruct((bsz, S_ALL, 4 * ML_HEADS), F32))
    out_specs = [pl.BlockSpec((1, ROW_TILE, wd), lambda b, j: (b, j, 0)) for wd in widths]
    out_specs.append(pl.BlockSpec((1, ROW_TILE, 4 * ML_HEADS), lambda b, j: (b, j, 0)))
    return pl.pallas_call(
        _inproj_odd_kernel,
        out_shape=out_shape,
        grid=(bsz, N_ROW_TILES),
        in_specs=[
            pl.BlockSpec((1, ROW_TILE, D_MODEL), lambda b, j: (b, j, 0)),
            pl.BlockSpec((1, 1, 3, D_MODEL), _mod_index),
            pl.BlockSpec((D_MODEL, ntot), lambda b, j: (0, 0)),
            pl.BlockSpec((ROW_TILE, 128), lambda b, j: (j, 0)),
            pl.BlockSpec((ROW_TILE, 128), lambda b, j: (j, 0)),
            pl.BlockSpec((1, 4 * ML_HEADS), lambda b, j: (0, 0)),
        ],
        out_specs=out_specs,
        compiler_params=_params("parallel", "arbitrary"),
        name="inproj_odd",
    )(xc, mod, w, cos_t, sin_t, gate_b)


def _outproj_kernel(ya_ref, yb_ref, x_ref, mod_ref, w_ref, g_ref, b_ref, o_ref):
    half = ya_ref.shape[2]
    y = _dot(ya_ref[0], w_ref[0:half, :]) + _dot(yb_ref[0], w_ref[half:2 * half, :])
    gate = mod_ref[0, 0, 2:3, :]
    r = ALPHA * x_ref[0] + gate * y
    mu = jnp.mean(r, axis=-1, keepdims=True)
    d = r - mu
    var = jnp.mean(d * d, axis=-1, keepdims=True)
    o_ref[0] = d * lax.rsqrt(var + LN_EPS) * g_ref[...] + b_ref[...]


def _outproj(ya, yb, xc, mod, w, ln_g, ln_b, latent_only):
    bsz = xc.shape[0]
    skip = 1 if latent_only else 0
    n_tiles = N_ROW_TILES - skip
    half = ya.shape[2]

    def mod_index(b, j):
        return _mod_index(b, j + skip)

    return pl.pallas_call(
        _outproj_kernel,
        out_shape=jax.ShapeDtypeStruct((bsz, n_tiles * ROW_TILE, D_MODEL), F32),
        grid=(bsz, n_tiles),
        in_specs=[
            pl.BlockSpec((1, ROW_TILE, half), lambda b, j: (b, j + skip, 0)),
            pl.BlockSpec((1, ROW_TILE, half), lambda b, j: (b, j + skip, 0)),
            pl.BlockSpec((1, ROW_TILE, D_MODEL), lambda b, j: (b, j + skip, 0)),
            pl.BlockSpec((1, 1, 3, D_MODEL), mod_index),
            pl.BlockSpec((2 * half, D_MODEL), lambda b, j: (0, 0)),
            pl.BlockSpec((1, D_MODEL), lambda b, j: (0, 0)),
            pl.BlockSpec((1, D_MODEL), lambda b, j: (0, 0)),
        ],
        out_specs=pl.BlockSpec((1, ROW_TILE, D_MODEL), lambda b, j: (b, j, 0)),
        compiler_params=_params("parallel", "arbitrary"),
        name="outproj_ln",
    )(ya, yb, xc, mod, w, ln_g, ln_b)


def _scan_epilogue(of_scr, ob_scr, gate_ref, ng_ref, out_ref):
    def body(t, carry):
        rows = pl.ds(pl.multiple_of(t * ROW_TILE, ROW_TILE), ROW_TILE)
        o = of_scr[rows, :] + ob_scr[rows, :]
        parts = []
        for hh in range(2):
            oh = o[:, hh * 128:(hh + 1) * 128]
            ms = jnp.mean(oh * oh, axis=-1, keepdims=True)
            parts.append(oh * lax.rsqrt(ms + NORM_EPS))
        y = jnp.concatenate(parts, axis=1) * ng_ref[0]
        out_ref[0, rows, :] = (y * _silu(gate_ref[0, rows, :].astype(F32))).astype(BF16)
        return carry

    lax.fori_loop(0, N_ROW_TILES, body, 0)


def _scan_chunk_order(step_fn):
    def ctx_body(c, carry):
        step_fn(c, 0)
        step_fn(CTX_CHUNKS - 1 - c, 1)
        return carry

    def lat_body(c, carry):
        step_fn(c, 0)
        step_fn(N_CHUNKS - 1 + CTX_CHUNKS - c, 1)
        return carry

    lax.fori_loop(0, CTX_CHUNKS, ctx_body, 0)
    lax.fori_loop(CTX_CHUNKS, N_CHUNKS, lat_body, 0)


def _gla_kernel(q_ref, k_ref, v_ref, gate_ref, lr_ref, gw_ref, gb_ref, ng_ref, out_ref,
                g_scr, of_scr, ob_scr, st_scr):
    def pre(t, carry):
        rows = pl.ds(pl.multiple_of(t * ROW_TILE, ROW_TILE), ROW_TILE)
        z = _dot(lr_ref[0, rows, :].astype(BF16), gw_ref[0]) + gb_ref[0]
        g_scr[rows, :] = _log_sigmoid(z) * (1.0 / GLA_TAU)
        return carry

    lax.fori_loop(0, N_ROW_TILES, pre, 0)
    st_scr[...] = jnp.zeros(st_scr.shape, F32)

    def step(cc, d):
        rows = pl.ds(pl.multiple_of(cc * CHUNK, CHUNK), CHUNK)
        g = g_scr[rows, d * 128:(d + 1) * 128]
        b = _cumsum_rows(g, reverse=(d == 1))
        b_last = b[0:1, :] if d == 1 else b[CHUNK - 1:CHUNK, :]
        q = q_ref[0, rows, :].astype(F32)
        k = k_ref[0, rows, :].astype(F32)
        v = v_ref[0, rows, :]
        q_in = q * jnp.exp(b)
        k_in = (k * jnp.exp(-b)).astype(BF16)
        k_dec = (k * jnp.exp(b_last - b)).astype(BF16)

        head0 = lax.broadcasted_iota(jnp.int32, (CHUNK, 128), 1) < GLA_DK
        q_stack = jnp.concatenate([jnp.where(head0, q_in, 0.0), jnp.where(head0, 0.0, q_in)],
                                  axis=0).astype(BF16)
        a = _dot_nt(q_stack, k_in)
        t_i = lax.broadcasted_iota(jnp.int32, (2 * CHUNK, CHUNK), 0) & (CHUNK - 1)
        s_i = lax.broadcasted_iota(jnp.int32, (2 * CHUNK, CHUNK), 1)
        keep = (s_i >= t_i) if d == 1 else (s_i <= t_i)
        a = jnp.where(keep, a, 0.0).astype(BF16)

        vhead0 = lax.broadcasted_iota(jnp.int32, (CHUNK, 2 * GLA_DV), 1) < GLA_DV
        zero = jnp.zeros_like(v)
        st = st_scr[d]
        o = (_dot(a[:CHUNK], jnp.where(vhead0, v, zero))
             + _dot(a[CHUNK:], jnp.where(vhead0, zero, v))
             + _dot_nt(q_in.astype(BF16), st.astype(BF16)))
        if d == 0:
            of_scr[rows, :] = o
        else:
            ob_scr[rows, :] = o

        u = _dot_tn(v, k_dec)
        r_i = lax.broadcasted_iota(jnp.int32, u.shape, 0) < GLA_DV
        c_i = lax.broadcasted_iota(jnp.int32, u.shape, 1) < GLA_DK
        st_scr[d] = st * jnp.exp(b_last) + jnp.where(r_i == c_i, u, 0.0)

    _scan_chunk_order(step)
    _scan_epilogue(of_scr, ob_scr, gate_ref, ng_ref, out_ref)


def _gla(q, k, v, gate, lr, gw, gb, ng):
    bsz = q.shape[0]
    pairs = GLA_HEADS // 2
    return pl.pallas_call(
        _gla_kernel,
        out_shape=jax.ShapeDtypeStruct((bsz, S_ALL, GLA_HEADS * GLA_DV), BF16),
        grid=(bsz, pairs),
        in_specs=[
            pl.BlockSpec((1, S_ALL, 128), lambda b, p: (b, 0, p)),
            pl.BlockSpec((1, S_ALL, 128), lambda b, p: (b, 0, p)),
            pl.BlockSpec((1, S_ALL, 256), lambda b, p: (b, 0, p)),
            pl.BlockSpec((1, S_ALL, 256), lambda b, p: (b, 0, p)),
            pl.BlockSpec((1, S_ALL, 2 * GLA_RANK), lambda b, p: (b, 0, 0)),
            pl.BlockSpec((1, 2 * GLA_RANK, 256), lambda b, p: (p, 0, 0)),
            pl.BlockSpec((1, 1, 256), lambda b, p: (p, 0, 0)),
            pl.BlockSpec((1, 1, 256), lambda b, p: (p, 0, 0)),
        ],
        out_specs=pl.BlockSpec((1, S_ALL, 256), lambda b, p: (b, 0, p)),
        scratch_shapes=[
            pltpu.VMEM((S_ALL, 256), F32),
            pltpu.VMEM((S_ALL, 256), F32),
            pltpu.VMEM((S_ALL, 256), F32),
            pltpu.VMEM((2, 2 * GLA_DV, 2 * GLA_DK), F32),
        ],
        compiler_params=_params("parallel", "arbitrary"),
        name="gla_scan",
    )(q, k, v, gate, lr, gw, gb, ng)


def _mlstm_kernel(q_ref, k_ref, v_ref, gate_ref, gcol_ref, grow_ref, ng_ref, out_ref,
                  bcm_scr, rcm_scr, rrow_scr, blast_scr, wsmax_scr, of_scr, ob_scr,
                  cm_scr, n_scr, m_scr):
    lane128 = lax.broadcasted_iota(jnp.int32, (1, 128), 1)
    half0_row = lane128 < CHUNK

    li = lax.broadcasted_iota(jnp.int32, (128, 128), 0)
    lj = lax.broadcasted_iota(jnp.int32, (128, 128), 1)
    same_half = (li < CHUNK) == (lj < CHUNK)
    ones_half = jnp.where(same_half, 1.0, 0.0).astype(BF16)
    for d in range(2):
        within = (li >= lj) if d == 1 else (li <= lj)
        tri = jnp.where(same_half & within, 1.0, 0.0).astype(BF16)
        lf = _log_sigmoid(grow_ref[0, d, 1, 0])
        b_row = _split_dot_r(lf, tri)
        total = _split_dot_r(lf, ones_half)
        r_row = grow_ref[0, d, 0, 0] - b_row
        rrow_scr[d] = r_row
        blast_scr[d] = total
        mx0 = jnp.max(jnp.where(half0_row, r_row, NEG), axis=-1, keepdims=True)
        mx1 = jnp.max(jnp.where(half0_row, NEG, r_row), axis=-1, keepdims=True)
        wsmax_scr[d] = total + jnp.where(half0_row, mx0, mx1)

    ti = lax.broadcasted_iota(jnp.int32, (ROW_TILE, ROW_TILE), 0)
    tj = lax.broadcasted_iota(jnp.int32, (ROW_TILE, ROW_TILE), 1)
    same_chunk = (ti >> 6) == (tj >> 6)
    tril = jnp.where(same_chunk & (tj <= ti), 1.0, 0.0).astype(BF16)
    triu = jnp.where(same_chunk & (tj >= ti), 1.0, 0.0).astype(BF16)
    half0_tile = lax.broadcasted_iota(jnp.int32, (ROW_TILE, 128), 1) < CHUNK

    def pre(t, carry):
        rows = pl.ds(pl.multiple_of(t * ROW_TILE, ROW_TILE), ROW_TILE)
        gc = gcol_ref[0, 0, rows, :]
        lf = _log_sigmoid(gc)
        for d in range(2):
            bcol = _split_dot(triu if d == 1 else tril, lf)
            b0 = jnp.broadcast_to(bcol[:, d * 4 + 2:d * 4 + 3], (ROW_TILE, 128))
            b1 = jnp.broadcast_to(bcol[:, d * 4 + 3:d * 4 + 4], (ROW_TILE, 128))
            i0 = jnp.broadcast_to(gc[:, d * 4:d * 4 + 1], (ROW_TILE, 128))
            i1 = jnp.broadcast_to(gc[:, d * 4 + 1:d * 4 + 2], (ROW_TILE, 128))
            bc = jnp.where(half0_tile, b0, b1)
            bcm_scr[d, rows, :] = bc
            rcm_scr[d, rows, :] = jnp.where(half0_tile, i0, i1) - bc
        return carry

    lax.fori_loop(0, N_ROW_TILES, pre, 0)
    cm_scr[...] = jnp.zeros(cm_scr.shape, F32)
    n_scr[...] = jnp.zeros(n_scr.shape, F32)
    m_scr[...] = jnp.zeros(m_scr.shape, F32)

    def step(cc, d):
        rows = pl.ds(pl.multiple_of(cc * CHUNK, CHUNK), CHUNK)
        bc = bcm_scr[d, rows, :]
        rc = rcm_scr[d, rows, :]
        rr = rrow_scr[d, pl.ds(cc, 1), :]
        bl = blast_scr[d, pl.ds(cc, 1), :]
        wm = wsmax_scr[d, pl.ds(cc, 1), :]
        m_old = m_scr[d]
        q = q_ref[0, rows, :]
        k = k_ref[0, rows, :]
        v = v_ref[0, rows, :]

        half0 = lax.broadcasted_iota(jnp.int32, (CHUNK, 128), 1) < CHUNK
        t_i = lax.broadcasted_iota(jnp.int32, (CHUNK, 128), 0)
        s_i = lax.broadcasted_iota(jnp.int32, (CHUNK, 128), 1) & (CHUNK - 1)
        keep = (s_i >= t_i) if d == 1 else (s_i <= t_i)
        dm = jnp.where(keep, bc + rr, NEG)
        inter = bc + m_old
        mx0 = jnp.max(jnp.where(half0, dm, NEG), axis=-1, keepdims=True)
        mx1 = jnp.max(jnp.where(half0, NEG, dm), axis=-1, keepdims=True)
        mt = jnp.maximum(jnp.where(half0, mx0, mx1), inter)
        w = jnp.exp(dm - mt)
        w_int = jnp.exp(inter - mt)
        floor = jnp.exp(-mt)

        zero_k = jnp.zeros_like(k)
        k_stack = jnp.concatenate([jnp.where(half0, k, zero_k), jnp.where(half0, zero_k, k)], axis=0)
        s_w = (_dot_nt(q, k_stack) * w).astype(BF16)

        vhead0 = lax.broadcasted_iota(jnp.int32, (CHUNK, 2 * ML_DV), 1) < ML_DV
        zero_v = jnp.zeros_like(v)
        v_stack = jnp.concatenate([jnp.where(vhead0, v, zero_v), jnp.where(vhead0, zero_v, v)], axis=0)
        o_i = lax.broadcasted_iota(jnp.int32, (2 * CHUNK, 2 * ML_DV), 0) < CHUNK
        o_j = lax.broadcasted_iota(jnp.int32, (2 * CHUNK, 2 * ML_DV), 1) < ML_DV
        ones_stack = jnp.where(o_i == o_j, 1.0, 0.0).astype(BF16)

        cmt = cm_scr[d]
        n_row = n_scr[d]
        r_i = lax.broadcasted_iota(jnp.int32, cmt.shape, 0) < ML_DV
        c_i = lax.broadcasted_iota(jnp.int32, cmt.shape, 1) < ML_DQK
        blockdiag = r_i == c_i
        n_bd = jnp.where(blockdiag, jnp.broadcast_to(n_row, cmt.shape), 0.0)

        num_intra = _dot(s_w, v_stack)
        den_intra = _dot(s_w, ones_stack)
        num_state = _dot_nt(q, cmt.astype(BF16))
        den_state = _dot_nt(q, n_bd.astype(BF16))
        parts = []
        for hh in range(2):
            sl = slice(hh * ML_DV, (hh + 1) * ML_DV)
            col = slice(hh * CHUNK, hh * CHUNK + 1)
            num = num_intra[:, sl] + w_int[:, col] * num_state[:, sl]
            den = den_intra[:, sl] + w_int[:, col] * den_state[:, sl]
            parts.append(num / jnp.maximum(jnp.abs(den), floor[:, col]))
        h_out = jnp.concatenate(parts, axis=1)
        if d == 0:
            of_scr[rows, :] = h_out
        else:
            ob_scr[rows, :] = h_out

        m_new = jnp.maximum(bl + m_old, wm)
        dec = jnp.exp(bl + m_old - m_new)
        kws = k.astype(F32) * jnp.exp(rc + (bl - m_new))
        u = _dot_tn(v, kws.astype(BF16))
        cm_scr[d] = cmt * dec + jnp.where(blockdiag, u, 0.0)
        n_scr[d] = n_row * dec + jnp.sum(kws, axis=0, keepdims=True)
        m_scr[d] = m_new

    _scan_chunk_order(step)
    _scan_epilogue(of_scr, ob_scr, gate_ref, ng_ref, out_ref)


def _mlstm(q, k, v, gate, gcol, grow, ng):
    bsz = q.shape[0]
    pairs = ML_HEADS // 2
    return pl.pallas_call(
        _mlstm_kernel,
        out_shape=jax.ShapeDtypeStruct((bsz, S_ALL, ML_HEADS * ML_DV), BF16),
        grid=(bsz, pairs),
        in_specs=[
            pl.BlockSpec((1, S_ALL, 128), lambda b, p: (b, 0, p)),
            pl.BlockSpec((1, S_ALL, 128), lambda b, p: (b, 0, p)),
            pl.BlockSpec((1, S_ALL, 256), lambda b, p: (b, 0, p)),
            pl.BlockSpec((1, S_ALL, 256), lambda b, p: (b, 0, p)),
            pl.BlockSpec((1, 1, S_ALL, 8), lambda b, p: (b, p, 0, 0)),
            pl.BlockSpec((1, 2, 2, 1, N_CHUNKS, 128), lambda b, p: (b, 0, 0, p, 0, 0)),
            pl.BlockSpec((1, 1, 256), lambda b, p: (p, 0, 0)),
        ],
        out_specs=pl.BlockSpec((1, S_ALL, 256), lambda b, p: (b, 0, p)),
        scratch_shapes=[
            pltpu.VMEM((2, S_ALL, 128), F32),
            pltpu.VMEM((2, S_ALL, 128), F32),
            pltpu.VMEM((2, N_CHUNKS, 128), F32),
            pltpu.VMEM((2, N_CHUNKS, 128), F32),
            pltpu.VMEM((2, N_CHUNKS, 128), F32),
            pltpu.VMEM((S_ALL, 256), F32),
            pltpu.VMEM((S_ALL, 256), F32),
            pltpu.VMEM((2, 2 * ML_DV, 2 * ML_DQK), F32),
            pltpu.VMEM((2, 1, 2 * ML_DQK), F32),
            pltpu.VMEM((2, 1, 2 * ML_DQK), F32),
        ],
        compiler_params=_params("parallel", "arbitrary"),
        name="mlstm_scan",
    )(q, k, v, gate, gcol, grow, ng)


def _rpb_table_kernel(rpb_ref, o_ref):
    h = pl.program_id(0)
    n_dc = 2 * NA_KW - 1
    qi = lax.broadcasted_iota(jnp.int32, (GRID_W, 128), 0)
    lane = lax.broadcasted_iota(jnp.int32, (GRID_W, 128), 1)
    wi = lane & (GRID_W - 1)
    second = lane >= GRID_W
    dc = jnp.clip(wi - qi + (NA_KW - 1), 0, n_dc - 1)
    cs = jnp.clip(qi - NA_KW // 2, 0, GRID_W - NA_KW)
    col_ok = (wi >= cs) & (wi < cs + NA_KW)
    for dr in range(2 * NA_KH - 2):
        base = h * ((2 * NA_KH - 1) * n_dc) + dr * n_dc
        acc = jnp.zeros((GRID_W, 128), F32)
        for c in range(n_dc):
            val = jnp.where(second, rpb_ref[base + n_dc + c], rpb_ref[base + c])
            acc = jnp.where(dc == c, val, acc)
        o_ref[0, dr] = jnp.where(col_ok, acc, NEG)


def _rpb_table(rpb):
    flat = rpb.reshape(-1)
    return pl.pallas_call(
        _rpb_table_kernel,
        out_shape=jax.ShapeDtypeStruct((NA_HEADS, 2 * NA_KH - 2, GRID_W, 128), F32),
        grid=(NA_HEADS,),
        in_specs=[pl.BlockSpec(memory_space=pltpu.SMEM)],
        out_specs=pl.BlockSpec((1, 2 * NA_KH - 2, GRID_W, 128), lambda h: (h, 0, 0, 0)),
        compiler_params=_params("arbitrary"),
        name="na_rpb_table",
    )(flat)


def _softmax_pv(s_parts, v_parts, extra_logit=None):
    m = s_parts[0].max(axis=-1, keepdims=True)
    for s in s_parts[1:]:
        m = jnp.maximum(m, s.max(axis=-1, keepdims=True))
    if extra_logit is not None:
        m = jnp.maximum(m, extra_logit)
    l = 0.0 if extra_logit is None else jnp.exp(extra_logit - m)
    o = None
    for s, v in zip(s_parts, v_parts):
        p = jnp.exp(s - m)
        l = l + p.sum(axis=-1, keepdims=True)
        pv = _dot(p.astype(BF16), v)
        o = pv if o is None else o + pv
    return o / l


def _stack_heads(x):
    head0 = lax.broadcasted_iota(jnp.int32, x.shape, 1) < 64
    zero = jnp.zeros_like(x)
    return jnp.concatenate([jnp.where(head0, x, zero), jnp.where(head0, zero, x)], axis=0)


def _merge_heads(o):
    n = o.shape[0] // 2
    head0 = lax.broadcasted_iota(jnp.int32, (n, o.shape[1]), 1) < 64
    return jnp.where(head0, o[:n], o[n:])


def _na_kernel(q_ref, k_ref, v_ref, gate_ref, tab_ref, out_ref):
    kc = k_ref[0, 0:CTX_LEN, :]
    vc = v_ref[0, 0:CTX_LEN, :]
    n_lat = NA_KH * GRID_W

    def row(r, carry):
        rs = jnp.clip(r - NA_KH // 2, 0, GRID_ROWS - NA_KH)
        dr0 = rs - r + NA_KH - 1
        q_rows = pl.ds(pl.multiple_of(CTX_LEN + r * GRID_W, GRID_W), GRID_W)
        k_rows = pl.ds(pl.multiple_of(CTX_LEN + rs * GRID_W, GRID_W), n_lat)
        qs = _stack_heads(q_ref[0, q_rows, :])
        kb = k_ref[0, k_rows, :]
        vb = v_ref[0, k_rows, :]
        s_lat = _dot_nt(qs, kb)
        bias = jnp.concatenate(
            [jnp.concatenate([tab_ref[hh, dr0 + 2 * j] for j in range(NA_KH // 2)], axis=1)
             for hh in range(2)], axis=0)
        s_ctx = _dot_nt(qs, kc)
        o = _merge_heads(_softmax_pv([s_lat + bias, s_ctx], [vb, vc]))
        out_ref[0, q_rows, :] = (o * _silu(gate_ref[0, q_rows, :].astype(F32))).astype(BF16)
        return carry

    lax.fori_loop(0, GRID_ROWS, row, 0)

    for t in range(CTX_LEN // 128):
        rows = slice(t * 128, (t + 1) * 128)
        qs = _stack_heads(q_ref[0, rows, :])
        o = _merge_heads(_softmax_pv([_dot_nt(qs, kc)], [vc]))
        out_ref[0, rows, :] = (o * _silu(gate_ref[0, rows, :].astype(F32))).astype(BF16)


def _na(q, k, v, gate, table):
    bsz = q.shape[0]
    pairs = NA_HEADS // 2
    seq_spec = pl.BlockSpec((1, S_ALL, 128), lambda b, p: (b, 0, p))
    return pl.pallas_call(
        _na_kernel,
        out_shape=jax.ShapeDtypeStruct((bsz, S_ALL, NA_HEADS * NA_DH), BF16),
        grid=(bsz, pairs),
        in_specs=[seq_spec, seq_spec, seq_spec, seq_spec,
                  pl.BlockSpec((2, 2 * NA_KH - 2, GRID_W, 128), lambda b, p: (p, 0, 0, 0))],
        out_specs=seq_spec,
        compiler_params=_params("parallel", "arbitrary"),
        name="na_attn",
    )(q, k, v, gate, table)


def _swa_kernel(sink_ref, q_ref, k_ref, v_ref, gate_ref, out_ref):
    n = pl.program_id(1)
    ctx_blocks = CTX_LEN // SW_BLOCK
    span = SW_BLOCK + 2 * SW_WINDOW
    is_lat = n >= ctx_blocks
    qstart = (n - ctx_blocks) * SW_BLOCK
    kstart = jnp.clip(qstart - SW_WINDOW, 0, SEQ - span)
    k_rows = pl.ds(pl.multiple_of(CTX_LEN + kstart, SW_BLOCK), span)
    kw = k_ref[0, k_rows, :]
    vw = v_ref[0, k_rows, :]
    kc = k_ref[0, 0:CTX_LEN, :]
    vc = v_ref[0, 0:CTX_LEN, :]

    qpos = qstart + (lax.broadcasted_iota(jnp.int32, (2 * SW_BLOCK, span), 0) & (SW_BLOCK - 1))
    kpos = kstart + lax.broadcasted_iota(jnp.int32, (2 * SW_BLOCK, span), 1)
    ok = (jnp.abs(qpos - kpos) <= SW_WINDOW) & is_lat
    first = lax.broadcasted_iota(jnp.int32, (2 * SW_BLOCK, 1), 0) < SW_BLOCK

    for t in range(SW_HEADS // 2):
        cols = slice(t * 128, (t + 1) * 128)
        qs = _stack_heads(q_ref[0, :, cols])
        s_win = jnp.where(ok, _dot_nt(qs, kw), NEG)
        s_ctx = _dot_nt(qs, kc)
        sink = jnp.where(first, sink_ref[t], sink_ref[t + SW_HEADS // 2])
        o = _merge_heads(_softmax_pv([s_win, s_ctx], [vw, vc], extra_logit=sink))
        out_ref[0, :, cols] = (o * _silu(gate_ref[0, :, cols].astype(F32))).astype(BF16)


def _swa(q, k, v, gate, sink):
    bsz = q.shape[0]
    width = SW_HEADS * SW_DH
    blk_spec = pl.BlockSpec((1, SW_BLOCK, width), lambda b, n: (b, n, 0))
    kv_spec = pl.BlockSpec((1, S_ALL, 128), lambda b, n: (b, 0, 0))
    return pl.pallas_call(
        _swa_kernel,
        out_shape=jax.ShapeDtypeStruct((bsz, S_ALL, width), BF16),
        grid=(bsz, S_ALL // SW_BLOCK),
        in_specs=[pl.BlockSpec(memory_space=pltpu.SMEM), blk_spec, kv_spec, kv_spec, blk_spec],
        out_specs=blk_spec,
        compiler_params=_params("parallel", "arbitrary"),
        name="swa_attn",
    )(sink, q, k, v, gate)


def _pad_cols(w, width):
    return jnp.pad(w, ((0, 0), (0, width - w.shape[1])))


def _prep_even(w_in, gate_w, gate_b):
    o = np.cumsum((0, 256, 256, 512, 512, 32, 512, 512, 512, 512))
    gq, gk, gv, gg, lra, nq, nk, nv, ng = [w_in[:, o[i]:o[i + 1]] for i in range(9)]
    w = jnp.concatenate([gq * GLA_DK ** -0.5, gk, gv, gg, nq * NA_DH ** -0.5, nk, nv, ng,
                         _pad_cols(lra, 128)], axis=1).astype(BF16)
    gws, gbs = [], []
    for p in range(GLA_HEADS // 2):
        cols = slice(p * 128, (p + 1) * 128)
        zero = jnp.zeros((GLA_RANK, 128), F32)
        gws.append(jnp.concatenate([jnp.concatenate([gate_w[0][:, cols], zero], axis=1),
                                    jnp.concatenate([zero, gate_w[1][:, cols]], axis=1)], axis=0))
        gbs.append(jnp.concatenate([gate_b[0][cols], gate_b[1][cols]])[None])
    return w, jnp.stack(gws).astype(BF16), jnp.stack(gbs)


def _swa_head_perm():
    order = []
    for t in range(SW_HEADS // 2):
        order += list(range(t * SW_DH, (t + 1) * SW_DH))
        order += list(range((t + SW_HEADS // 2) * SW_DH, (t + SW_HEADS // 2 + 1) * SW_DH))
    return np.asarray(order)


def _rope_swap(w):
    nf = SW_DH // 4
    idx = np.arange(w.shape[1]).reshape(-1, 2, nf)[:, ::-1, :].reshape(-1)
    return w[:, idx]


def _prep_odd(w_in, w_out):
    o = np.cumsum((0, 256, 256, 512, 512, 16, 512, 128, 128, 512))
    mq, mk, mv, mg, gates, sq, sk, sv, sg = [w_in[:, o[i]:o[i + 1]] for i in range(9)]
    perm = _swa_head_perm()
    sq = sq[:, perm] * SW_DH ** -0.5
    w = jnp.concatenate([mq, mk * ML_DQK ** -0.5, mv, mg, sq, _rope_swap(sq), sk, _rope_swap(sk),
                         sv, sg[:, perm], _pad_cols(gates, 128)], axis=1).astype(BF16)
    half = ML_HEADS * ML_DV
    w_o = jnp.concatenate([w_out[:half], w_out[half:][perm]], axis=0).astype(BF16)
    return w, w_o


def _rope_tables():
    nf = SW_DH // 4
    freqs = ROPE_BASE ** (-jnp.arange(nf, dtype=F32) / nf)
    pos = jnp.arange(SEQ)
    rows = (pos // GRID_W).astype(F32)
    cols = (pos % GRID_W).astype(F32)
    ar = rows[:, None] * freqs[None, :]
    ac = cols[:, None] * freqs[None, :]
    cos = jnp.concatenate([jnp.cos(ar), jnp.cos(ar), jnp.cos(ac), jnp.cos(ac)], axis=1)
    sin = jnp.concatenate([-jnp.sin(ar), jnp.sin(ar), -jnp.sin(ac), jnp.sin(ac)], axis=1)
    cos = jnp.concatenate([jnp.ones((CTX_LEN, SW_DH), F32), cos], axis=0)
    sin = jnp.concatenate([jnp.zeros((CTX_LEN, SW_DH), F32), sin], axis=0)
    return jnp.tile(cos, (1, 2)), jnp.tile(sin, (1, 2))


def _gate_layouts(gates):
    bsz = gates.shape[0]
    g = gates.reshape(bsz, S_ALL, 2, 2, 2, 2)
    gcol = g.transpose(0, 4, 1, 2, 3, 5).reshape(bsz, 2, S_ALL, 8)
    g = gates.reshape(bsz, N_CHUNKS, CHUNK, 2, 2, 2, 2)
    grow = g.transpose(0, 3, 4, 5, 1, 6, 2).reshape(bsz, 2, 2, 2, N_CHUNKS, 2 * CHUNK)
    return gcol, grow


def kernel(x, c, ctx, c_ctx, w_ada, b_ada, ln_g, ln_b, w_in_even, w_out_even, gla_gate_w, gla_gate_b,
           gla_norm_g, na_rpb, w_in_odd, w_out_odd, ml_gate_b, ml_norm_g, sw_sink):
    bsz = x.shape[0]
    mod_rows = 16
    cvec = jnp.zeros((mod_rows, D_MODEL), F32).at[:bsz].set(c).at[bsz].set(c_ctx)
    ada = _ada(cvec, w_ada, b_ada)
    xc = jnp.concatenate([ctx, x], axis=1)
    cos_t, sin_t = _rope_tables()
    for l in range(DEPTH):
        i = l // 2
        last = l == DEPTH - 1
        lat = ada[l, :bsz].reshape(bsz, 1, 3, D_MODEL)
        cx = jnp.broadcast_to(ada[l, bsz].reshape(1, 1, 3, D_MODEL), (bsz, 1, 3, D_MODEL))
        mod = jnp.concatenate([lat, cx], axis=1)
        if l % 2 == 0:
            w, gw, gb = _prep_even(w_in_even[i], gla_gate_w[i], gla_gate_b[i])
            gq, gk, gv, gg, nq, nk, nv, ng, lr = _inproj_even(xc, mod, w)
            ya = _gla(gq, gk, gv, gg, lr, gw, gb, gla_norm_g[i].reshape(GLA_HEADS // 2, 1, 256))
            yb = _na(nq, nk, nv, ng, _rpb_table(na_rpb[i]))
            w_o = w_out_even[i].astype(BF16)
        else:
            w, w_o = _prep_odd(w_in_odd[i], w_out_odd[i])
            mq, mk, mv, mg, sq, sk, sv, sg, gates = _inproj_odd(
                xc, mod, w, cos_t, sin_t, ml_gate_b[i].reshape(1, 4 * ML_HEADS))
            gcol, grow = _gate_layouts(gates)
            ya = _mlstm(mq, mk, mv, mg, gcol, grow, ml_norm_g[i].reshape(ML_HEADS // 2, 1, 256))
            yb = _swa(sq, sk, sv, sg, sw_sink[i])
        xc = _outproj(ya, yb, xc, mod, w_o, ln_g[l].reshape(1, D_MODEL), ln_b[l].reshape(1, D_MODEL), last)
    return xc
```

```python
import functools

import numpy as np
import jax
import jax.numpy as jnp
from jax import lax
from jax.experimental import pallas as pl
from jax.experimental.pallas import tpu as pltpu

F32 = jnp.float32
BF16 = jnp.bfloat16

D_MODEL = 1024
SEQ = 4096
DEPTH = 4
GRID_W = 64
CTX_LEN = 256
S_ALL = CTX_LEN + SEQ
ALPHA = (2.0 * DEPTH) ** 0.25
LN_EPS = 1e-5
NORM_EPS = 1e-6
NEG = -1e30
LOG2E = 1.4426950408889634
GLA_HEADS, GLA_DK, GLA_DV, GLA_RANK, GLA_TAU = 4, 64, 128, 16, 16.0
NA_HEADS, NA_DH, NA_KH, NA_KW = 8, 64, 8, 16
ML_HEADS, ML_DQK, ML_DV = 4, 64, 128
SW_HEADS, SW_KV_HEADS, SW_DH, SW_WINDOW, SW_BLOCK = 8, 2, 64, 128, 128
ROPE_BASE = 10000.0
CHUNK = 64
N_CHUNKS = S_ALL // CHUNK
CTX_CHUNKS = CTX_LEN // CHUNK
ROW_TILE = 256
N_ROW_TILES = S_ALL // ROW_TILE
GRID_ROWS = SEQ // GRID_W
NA_UNROLL = 4

VMEM_LIMIT = 56 * 1024 * 1024

NT_DIMS = (((1,), (1,)), ((), ()))
TN_DIMS = (((0,), (0,)), ((), ()))


def _dot(a, b):
    return jnp.dot(a, b, preferred_element_type=F32)


def _dot_nt(a, b):
    return lax.dot_general(a, b, NT_DIMS, preferred_element_type=F32)


def _dot_tn(a, b):
    return lax.dot_general(a, b, TN_DIMS, preferred_element_type=F32)


def _sigmoid(x):
    return 1.0 / (1.0 + jnp.exp(-x))


def _silu(x):
    return x * _sigmoid(x)


def _log_sigmoid(x):
    return jnp.minimum(x, 0.0) - jnp.log1p(jnp.exp(-jnp.abs(x)))


def _cumsum_rows(x, reverse):
    n = x.shape[0]
    row = lax.broadcasted_iota(jnp.int32, x.shape, 0)
    k = 1
    while k < n:
        if reverse:
            x = x + jnp.where(row < n - k, pltpu.roll(x, n - k, axis=0), 0.0)
        else:
            x = x + jnp.where(row >= k, pltpu.roll(x, k, axis=0), 0.0)
        k *= 2
    return x


def _split_dot(mat_bf16, x):
    hi = x.astype(BF16)
    lo = (x - hi.astype(F32)).astype(BF16)
    return _dot(mat_bf16, hi) + _dot(mat_bf16, lo)


def _split_dot_r(x, mat_bf16):
    hi = x.astype(BF16)
    lo = (x - hi.astype(F32)).astype(BF16)
    return _dot(hi, mat_bf16) + _dot(lo, mat_bf16)


def _halves_to_tiles(x):
    half0 = lax.broadcasted_iota(jnp.int32, x.shape, 1) < 64
    sw = pltpu.roll(x, 64, axis=1)
    return jnp.concatenate([jnp.where(half0, x, sw), jnp.where(half0, sw, x)], axis=1)


def _params(*sem):
    return pltpu.CompilerParams(dimension_semantics=sem, vmem_limit_bytes=VMEM_LIMIT)


def _ada_kernel(c_ref, w_ref, b_ref, o_ref):
    cond = _silu(c_ref[...])
    o_ref[0] = jnp.dot(cond, w_ref[0], precision=lax.Precision.HIGHEST,
                       preferred_element_type=F32) + b_ref[0]


def _ada(cvec, w_ada, b_ada):
    rows = cvec.shape[0]
    nblk = 3
    return pl.pallas_call(
        _ada_kernel,
        out_shape=jax.ShapeDtypeStruct((DEPTH, rows, 3 * D_MODEL), F32),
        grid=(DEPTH, nblk),
        in_specs=[
            pl.BlockSpec((rows, D_MODEL), lambda l, n: (0, 0)),
            pl.BlockSpec((1, D_MODEL, D_MODEL), lambda l, n: (l, 0, n)),
            pl.BlockSpec((1, 1, D_MODEL), lambda l, n: (l, 0, n)),
        ],
        out_specs=pl.BlockSpec((1, rows, D_MODEL), lambda l, n: (l, 0, n)),
        compiler_params=_params("arbitrary", "arbitrary"),
        name="ada_mod",
    )(cvec, w_ada, b_ada.reshape(DEPTH, 1, 3 * D_MODEL))


def _modulated(x_ref, mod_ref):
    shift = mod_ref[0, 0, 0:1, :]
    scale = mod_ref[0, 0, 1:2, :]
    return (x_ref[0] * (1.0 + scale) + shift).astype(BF16)


EVEN_SLABS = (256, 256, 512, 512, 512, 512, 512, 512)


def _inproj_even_kernel(x_ref, mod_ref, w_ref, *outs):
    h = _modulated(x_ref, mod_ref)
    off = 0
    for ref, width in zip(outs[:-1], EVEN_SLABS):
        ref[0] = _dot(h, w_ref[:, off:off + width]).astype(ref.dtype)
        off += width
    lr = _dot(h, w_ref[:, off:off + 128])
    outs[-1][0] = lr[:, :2 * GLA_RANK]


ODD_SLABS = (256, 256, 512, 512)


def _inproj_odd_kernel(x_ref, mod_ref, w_ref, cos_ref, sin_ref, gb_ref,
                       mq, mk, mv, mg, sq, sk, sv, sg, gates):
    h = _modulated(x_ref, mod_ref)
    off = 0
    for ref, width in zip((mq, mk, mv, mg), ODD_SLABS):
        ref[0] = _dot(h, w_ref[:, off:off + width]).astype(ref.dtype)
        off += width
    cos = cos_ref[...]
    sin = sin_ref[...]
    a = _dot(h, w_ref[:, off:off + 512])
    b = _dot(h, w_ref[:, off + 512:off + 1024])
    for t in range(4):
        cols = slice(t * 128, (t + 1) * 128)
        sq[0, :, cols] = (a[:, cols] * cos + b[:, cols] * sin).astype(BF16)
    off += 1024
    kvg = _dot(h, w_ref[:, off:off + 512])
    sk[0] = (kvg[:, 0:128] * cos + kvg[:, 128:256] * sin).astype(BF16)
    sv[0] = kvg[:, 256:384].astype(BF16)
    gates[0] = kvg[:, 384:384 + 4 * ML_HEADS] + gb_ref[...]
    off += 512
    sg[0] = _dot(h, w_ref[:, off:off + 512]).astype(BF16)


def _mod_index(b, j):
    return (b, jnp.where(j == 0, 1, 0), 0, 0)


def _inproj_even(xc, mod, w):
    bsz = xc.shape[0]
    ntot = w.shape[1]
    out_shape = [jax.ShapeDtypeStruct((bsz, S_ALL, wd), BF16) for wd in EVEN_SLABS]
    out_shape.append(jax.ShapeDtypeStruct((bsz, S_ALL, 2 * GLA_RANK), F32))
    out_specs = [pl.BlockSpec((1, ROW_TILE, wd), lambda b, j: (b, j, 0)) for wd in EVEN_SLABS]
    out_specs.append(pl.BlockSpec((1, ROW_TILE, 2 * GLA_RANK), lambda b, j: (b, j, 0)))
    return pl.pallas_call(
        _inproj_even_kernel,
        out_shape=out_shape,
        grid=(bsz, N_ROW_TILES),
        in_specs=[
            pl.BlockSpec((1, ROW_TILE, D_MODEL), lambda b, j: (b, j, 0)),
            pl.BlockSpec((1, 1, 3, D_MODEL), _mod_index),
            pl.BlockSpec((D_MODEL, ntot), lambda b, j: (0, 0)),
        ],
        out_specs=out_specs,
        compiler_params=_params("parallel", "arbitrary"),
        name="inproj_even",
    )(xc, mod, w)


def _inproj_odd(xc, mod, w, cos_t, sin_t, gate_b):
    bsz = xc.shape[0]
    ntot = w.shape[1]
    widths = (256, 256, 512, 512, 512, 128, 128, 512)
    out_shape = [jax.ShapeDtypeStruct((bsz, S_ALL, wd), BF16) for wd in widths]
    out_shape.append(jax.ShapeDtypeStruct((bsz, S_ALL, 4 * ML_HEADS), F32))
    out_specs = [pl.BlockSpec((1, ROW_TILE, wd), lambda b, j: (b, j, 0)) for wd in widths]
    out_specs.append(pl.BlockSpec((1, ROW_TILE, 4 * ML_HEADS), lambda b, j: (b, j, 0)))
    return pl.pallas_call(
        _inproj_odd_kernel,
        out_shape=out_shape,
        grid=(bsz, N_ROW_TILES),
        in_specs=[
            pl.BlockSpec((1, ROW_TILE, D_MODEL), lambda b, j: (b, j, 0)),
            pl.BlockSpec((1, 1, 3, D_MODEL), _mod_index),
            pl.BlockSpec((D_MODEL, ntot), lambda b, j: (0, 0)),
            pl.BlockSpec((ROW_TILE, 128), lambda b, j: (j, 0)),
            pl.BlockSpec((ROW_TILE, 128), lambda b, j: (j, 0)),
            pl.BlockSpec((1, 4 * ML_HEADS), lambda b, j: (0, 0)),
        ],
        out_specs=out_specs,
        compiler_params=_params("parallel", "arbitrary"),
        name="inproj_odd",
    )(xc, mod, w, cos_t, sin_t, gate_b)


def _outproj_kernel(ya_ref, yb_ref, x_ref, mod_ref, w_ref, g_ref, b_ref, o_ref):
    half = ya_ref.shape[2]
    y = _dot(ya_ref[0], w_ref[0:half, :]) + _dot(yb_ref[0], w_ref[half:2 * half, :])
    gate = mod_ref[0, 0, 2:3, :]
    r = ALPHA * x_ref[0] + gate * y
    mu = jnp.mean(r, axis=-1, keepdims=True)
    d = r - mu
    var = jnp.mean(d * d, axis=-1, keepdims=True)
    o_ref[0] = d * lax.rsqrt(var + LN_EPS) * g_ref[...] + b_ref[...]


def _outproj(ya, yb, xc, mod, w, ln_g, ln_b, latent_only):
    bsz = xc.shape[0]
    skip = 1 if latent_only else 0
    n_tiles = N_ROW_TILES - skip
    half = ya.shape[2]

    def mod_index(b, j):
        return _mod_index(b, j + skip)

    return pl.pallas_call(
        _outproj_kernel,
        out_shape=jax.ShapeDtypeStruct((bsz, n_tiles * ROW_TILE, D_MODEL), F32),
        grid=(bsz, n_tiles),
        in_specs=[
            pl.BlockSpec((1, ROW_TILE, half), lambda b, j: (b, j + skip, 0)),
            pl.BlockSpec((1, ROW_TILE, half), lambda b, j: (b, j + skip, 0)),
            pl.BlockSpec((1, ROW_TILE, D_MODEL), lambda b, j: (b, j + skip, 0)),
            pl.BlockSpec((1, 1, 3, D_MODEL), mod_index),
            pl.BlockSpec((2 * half, D_MODEL), lambda b, j: (0, 0)),
            pl.BlockSpec((1, D_MODEL), lambda b, j: (0, 0)),
            pl.BlockSpec((1, D_MODEL), lambda b, j: (0, 0)),
        ],
        out_specs=pl.BlockSpec((1, ROW_TILE, D_MODEL), lambda b, j: (b, j, 0)),
        compiler_params=_params("parallel", "arbitrary"),
        name="outproj_ln",
    )(ya, yb, xc, mod, w, ln_g, ln_b)


def _scan_epilogue(of_scr, ob_scr, gate_ref, ng_ref, out_ref):
    def body(t, carry):
        rows = pl.ds(pl.multiple_of(t * ROW_TILE, ROW_TILE), ROW_TILE)
        o = of_scr[rows, :] + ob_scr[rows, :]
        parts = []
        for hh in range(2):
            oh = o[:, hh * 128:(hh + 1) * 128]
            ms = jnp.mean(oh * oh, axis=-1, keepdims=True)
            parts.append(oh * lax.rsqrt(ms + NORM_EPS))
        y = jnp.concatenate(parts, axis=1) * ng_ref[0]
        out_ref[0, rows, :] = (y * _silu(gate_ref[0, rows, :].astype(F32))).astype(BF16)
        return carry

    lax.fori_loop(0, N_ROW_TILES, body, 0)


def _bwd_chunk(i):
    return jnp.where(i < CTX_CHUNKS, CTX_CHUNKS - 1 - i, N_CHUNKS - 1 + CTX_CHUNKS - i)


def _scan_pipeline(local_fn, state_fn):
    local_fn(0, 0, 0)
    local_fn(CTX_CHUNKS - 1, 1, 0)

    def body(j, carry):
        for u in range(2):
            i = 2 * j + u
            nxt = jnp.minimum(i + 1, N_CHUNKS - 1)
            local_fn(nxt, 0, (u + 1) % 2)
            local_fn(_bwd_chunk(nxt), 1, (u + 1) % 2)
            state_fn(i, 0, u % 2)
            state_fn(_bwd_chunk(i), 1, u % 2)
        return carry

    lax.fori_loop(0, N_CHUNKS // 2, body, 0)


def _gla_kernel(q_ref, k_ref, v_ref, gate_ref, lr_ref, gw_ref, gb_ref, ng_ref, out_ref,
                g_scr, of_scr, ob_scr, st_scr, a_scr, qin_scr, u_scr, dl_scr):
    def pre(t, carry):
        rows = pl.ds(pl.multiple_of(t * ROW_TILE, ROW_TILE), ROW_TILE)
        z = _dot(lr_ref[0, rows, :].astype(BF16), gw_ref[0]) + gb_ref[0]
        g_scr[rows, :] = _log_sigmoid(z) * (1.0 / GLA_TAU)
        return carry

    lax.fori_loop(0, N_ROW_TILES, pre, 0)
    st_scr[...] = jnp.zeros(st_scr.shape, F32)

    def local(cc, d, slot):
        rows = pl.ds(pl.multiple_of(cc * CHUNK, CHUNK), CHUNK)
        g = g_scr[rows, d * 128:(d + 1) * 128]
        b = _cumsum_rows(g, reverse=(d == 1))
        b_last = b[0:1, :] if d == 1 else b[CHUNK - 1:CHUNK, :]
        q = q_ref[0, rows, :].astype(F32)
        k = k_ref[0, rows, :].astype(F32)
        q_in = q * jnp.exp(b)
        k_in = (k * jnp.exp(-b)).astype(BF16)
        k_dec = (k * jnp.exp(b_last - b)).astype(BF16)

        head0 = lax.broadcasted_iota(jnp.int32, (CHUNK, 128), 1) < GLA_DK
        q_stack = jnp.concatenate([jnp.where(head0, q_in, 0.0), jnp.where(head0, 0.0, q_in)],
                                  axis=0).astype(BF16)
        a = _dot_nt(q_stack, k_in)
        t_i = lax.broadcasted_iota(jnp.int32, (2 * CHUNK, CHUNK), 0) & (CHUNK - 1)
        s_i = lax.broadcasted_iota(jnp.int32, (2 * CHUNK, CHUNK), 1)
        keep = (s_i >= t_i) if d == 1 else (s_i <= t_i)
        a_scr[slot, d] = jnp.where(keep, a, 0.0).astype(BF16)
        qin_scr[slot, d] = q_in.astype(BF16)

        u = _dot_tn(v_ref[0, rows, :], k_dec)
        r_i = lax.broadcasted_iota(jnp.int32, u.shape, 0) < GLA_DV
        c_i = lax.broadcasted_iota(jnp.int32, u.shape, 1) < GLA_DK
        u_scr[slot, d] = jnp.where(r_i == c_i, u, 0.0)
        dl_scr[slot, d] = jnp.exp(b_last)

    def state(cc, d, slot):
        rows = pl.ds(pl.multiple_of(cc * CHUNK, CHUNK), CHUNK)
        v = v_ref[0, rows, :]
        a = a_scr[slot, d]
        vhead0 = lax.broadcasted_iota(jnp.int32, (CHUNK, 2 * GLA_DV), 1) < GLA_DV
        zero = jnp.zeros_like(v)
        st = st_scr[d]
        o = (_dot(a[:CHUNK], jnp.where(vhead0, v, zero))
             + _dot(a[CHUNK:], jnp.where(vhead0, zero, v))
             + _dot_nt(qin_scr[slot, d], st.astype(BF16)))
        if d == 0:
            of_scr[rows, :] = o
        else:
            ob_scr[rows, :] = o
        st_scr[d] = st * dl_scr[slot, d] + u_scr[slot, d]

    _scan_pipeline(local, state)
    _scan_epilogue(of_scr, ob_scr, gate_ref, ng_ref, out_ref)


def _gla(q, k, v, gate, lr, gw, gb, ng):
    bsz = q.shape[0]
    pairs = GLA_HEADS // 2
    return pl.pallas_call(
        _gla_kernel,
        out_shape=jax.ShapeDtypeStruct((bsz, S_ALL, GLA_HEADS * GLA_DV), BF16),
        grid=(bsz, pairs),
        in_specs=[
            pl.BlockSpec((1, S_ALL, 128), lambda b, p: (b, 0, p)),
            pl.BlockSpec((1, S_ALL, 128), lambda b, p: (b, 0, p)),
            pl.BlockSpec((1, S_ALL, 256), lambda b, p: (b, 0, p)),
            pl.BlockSpec((1, S_ALL, 256), lambda b, p: (b, 0, p)),
            pl.BlockSpec((1, S_ALL, 2 * GLA_RANK), lambda b, p: (b, 0, 0)),
            pl.BlockSpec((1, 2 * GLA_RANK, 256), lambda b, p: (p, 0, 0)),
            pl.BlockSpec((1, 1, 256), lambda b, p: (p, 0, 0)),
            pl.BlockSpec((1, 1, 256), lambda b, p: (p, 0, 0)),
        ],
        out_specs=pl.BlockSpec((1, S_ALL, 256), lambda b, p: (b, 0, p)),
        scratch_shapes=[
            pltpu.VMEM((S_ALL, 256), F32),
            pltpu.VMEM((S_ALL, 256), F32),
            pltpu.VMEM((S_ALL, 256), F32),
            pltpu.VMEM((2, 2 * GLA_DV, 2 * GLA_DK), F32),
            pltpu.VMEM((2, 2, 2 * CHUNK, CHUNK), BF16),
            pltpu.VMEM((2, 2, CHUNK, 2 * GLA_DK), BF16),
            pltpu.VMEM((2, 2, 2 * GLA_DV, 2 * GLA_DK), F32),
            pltpu.VMEM((2, 2, 1, 2 * GLA_DK), F32),
        ],
        compiler_params=_params("parallel", "arbitrary"),
        name="gla_scan",
    )(q, k, v, gate, lr, gw, gb, ng)


def _mlstm_kernel(q_ref, k_ref, v_ref, gate_ref, gcol_ref, grow_ref, ng_ref, out_ref,
                  bcm_scr, rcm_scr, rrow_scr, blast_scr, wsmax_scr, of_scr, ob_scr,
                  cm_scr, n_scr, m_scr, ml_scr, num_scr, den_scr, u_scr, ks_scr,
                  m2_scr, blast2_scr, wsmax2_scr):
    lane128 = lax.broadcasted_iota(jnp.int32, (1, 128), 1)
    half0_row = lane128 < CHUNK

    li = lax.broadcasted_iota(jnp.int32, (128, 128), 0)
    lj = lax.broadcasted_iota(jnp.int32, (128, 128), 1)
    same_half = (li < CHUNK) == (lj < CHUNK)
    ones_half = jnp.where(same_half, 1.0, 0.0).astype(BF16)
    for d in range(2):
        within = (li >= lj) if d == 1 else (li <= lj)
        tri = jnp.where(same_half & within, 1.0, 0.0).astype(BF16)
        lf = _log_sigmoid(grow_ref[0, d, 1, 0])
        b_row = _split_dot_r(lf, tri)
        total = _split_dot_r(lf, ones_half)
        r_row = grow_ref[0, d, 0, 0] - b_row
        rrow_scr[d] = r_row
        blast_scr[d] = total
        mx0 = jnp.max(jnp.where(half0_row, r_row, NEG), axis=-1, keepdims=True)
        mx1 = jnp.max(jnp.where(half0_row, NEG, r_row), axis=-1, keepdims=True)
        wsmax = total + jnp.where(half0_row, mx0, mx1)
        wsmax_scr[d] = wsmax
        blast2_scr[d] = _halves_to_tiles(total)
        wsmax2_scr[d] = _halves_to_tiles(wsmax)

    ti = lax.broadcasted_iota(jnp.int32, (ROW_TILE, ROW_TILE), 0)
    tj = lax.broadcasted_iota(jnp.int32, (ROW_TILE, ROW_TILE), 1)
    same_chunk = (ti >> 6) == (tj >> 6)
    tril = jnp.where(same_chunk & (tj <= ti), 1.0, 0.0).astype(BF16)
    triu = jnp.where(same_chunk & (tj >= ti), 1.0, 0.0).astype(BF16)
    half0_tile = lax.broadcasted_iota(jnp.int32, (ROW_TILE, 128), 1) < CHUNK

    def pre(t, carry):
        rows = pl.ds(pl.multiple_of(t * ROW_TILE, ROW_TILE), ROW_TILE)
        gc = gcol_ref[0, 0, rows, :]
        lf = _log_sigmoid(gc)
        for d in range(2):
            bcol = _split_dot(triu if d == 1 else tril, lf)
            b0 = jnp.broadcast_to(bcol[:, d * 4 + 2:d * 4 + 3], (ROW_TILE, 128))
            b1 = jnp.broadcast_to(bcol[:, d * 4 + 3:d * 4 + 4], (ROW_TILE, 128))
            i0 = jnp.broadcast_to(gc[:, d * 4:d * 4 + 1], (ROW_TILE, 128))
            i1 = jnp.broadcast_to(gc[:, d * 4 + 1:d * 4 + 2], (ROW_TILE, 128))
            bc = jnp.where(half0_tile, b0, b1)
            bcm_scr[d, rows, :] = bc
            rcm_scr[d, rows, :] = jnp.where(half0_tile, i0, i1) - bc
        return carry

    lax.fori_loop(0, N_ROW_TILES, pre, 0)
    cm_scr[...] = jnp.zeros(cm_scr.shape, F32)
    n_scr[...] = jnp.zeros(n_scr.shape, F32)
    m_scr[...] = jnp.zeros(m_scr.shape, F32)
    m2_scr[...] = jnp.zeros(m2_scr.shape, F32)

    def local(cc, d, slot):
        rows = pl.ds(pl.multiple_of(cc * CHUNK, CHUNK), CHUNK)
        bc = bcm_scr[d, rows, :]
        rc = rcm_scr[d, rows, :]
        rr = rrow_scr[d, pl.ds(cc, 1), :]
        bl = blast_scr[d, pl.ds(cc, 1), :]
        wm = wsmax_scr[d, pl.ds(cc, 1), :]
        q = q_ref[0, rows, :]
        k = k_ref[0, rows, :]
        v = v_ref[0, rows, :]

        half0 = lax.broadcasted_iota(jnp.int32, (CHUNK, 128), 1) < CHUNK
        t_i = lax.broadcasted_iota(jnp.int32, (CHUNK, 128), 0)
        s_i = lax.broadcasted_iota(jnp.int32, (CHUNK, 128), 1) & (CHUNK - 1)
        keep = (s_i >= t_i) if d == 1 else (s_i <= t_i)
        dm = jnp.where(keep, bc + rr, NEG)
        mx0 = jnp.max(jnp.where(half0, dm, NEG), axis=-1, keepdims=True)
        mx1 = jnp.max(jnp.where(half0, NEG, dm), axis=-1, keepdims=True)
        m_loc = jnp.where(half0, mx0, mx1)
        ml_scr[slot, d] = jnp.concatenate([jnp.broadcast_to(mx0, (CHUNK, ML_DV)),
                                           jnp.broadcast_to(mx1, (CHUNK, ML_DV))], axis=1)

        zero_k = jnp.zeros_like(k)
        k_stack = jnp.concatenate([jnp.where(half0, k, zero_k), jnp.where(half0, zero_k, k)], axis=0)
        s_w = (_dot_nt(q, k_stack) * jnp.exp(dm - m_loc)).astype(BF16)

        vhead0 = lax.broadcasted_iota(jnp.int32, (CHUNK, 2 * ML_DV), 1) < ML_DV
        zero_v = jnp.zeros_like(v)
        v_stack = jnp.concatenate([jnp.where(vhead0, v, zero_v), jnp.where(vhead0, zero_v, v)], axis=0)
        o_i = lax.broadcasted_iota(jnp.int32, (2 * CHUNK, 2 * ML_DV), 0) < CHUNK
        o_j = lax.broadcasted_iota(jnp.int32, (2 * CHUNK, 2 * ML_DV), 1) < ML_DV
        ones_stack = jnp.where(o_i == o_j, 1.0, 0.0).astype(BF16)
        num_scr[slot, d] = _dot(s_w, v_stack)
        den_scr[slot, d] = _dot(s_w, ones_stack)

        kws = k.astype(F32) * jnp.exp(rc + (bl - wm))
        u = _dot_tn(v, kws.astype(BF16))
        r_i = lax.broadcasted_iota(jnp.int32, u.shape, 0) < ML_DV
        c_i = lax.broadcasted_iota(jnp.int32, u.shape, 1) < ML_DQK
        u_scr[slot, d] = jnp.where(r_i == c_i, u, 0.0)
        ks_scr[slot, d] = jnp.sum(kws, axis=0, keepdims=True)

    def state(cc, d, slot):
        rows = pl.ds(pl.multiple_of(cc * CHUNK, CHUNK), CHUNK)
        bl = blast_scr[d, pl.ds(cc, 1), :]
        wm = wsmax_scr[d, pl.ds(cc, 1), :]
        m_old = m_scr[d]
        m_old2 = m2_scr[d]
        m_loc = ml_scr[slot, d]
        q = q_ref[0, rows, :]

        inter = _halves_to_tiles(bcm_scr[d, rows, :]) + m_old2
        mt = jnp.maximum(m_loc, inter)
        w_loc = jnp.exp(m_loc - mt)
        w_int = jnp.exp(inter - mt)
        floor = jnp.exp(-mt)

        cmt = cm_scr[d]
        n_row = n_scr[d]
        r_i = lax.broadcasted_iota(jnp.int32, cmt.shape, 0) < ML_DV
        c_i = lax.broadcasted_iota(jnp.int32, cmt.shape, 1) < ML_DQK
        n_bd = jnp.where(r_i == c_i, jnp.broadcast_to(n_row, cmt.shape), 0.0)
        num_state = _dot_nt(q, cmt.astype(BF16))
        den_state = _dot_nt(q, n_bd.astype(BF16))
        num = w_loc * num_scr[slot, d] + w_int * num_state
        den = w_loc * den_scr[slot, d] + w_int * den_state
        h_out = num / jnp.maximum(jnp.abs(den), floor)
        if d == 0:
            of_scr[rows, :] = h_out
        else:
            ob_scr[rows, :] = h_out

        m_new = jnp.maximum(bl + m_old, wm)
        dec = jnp.exp(bl + m_old - m_new)
        ws_scale = jnp.exp(wm - m_new)
        cm_scr[d] = cmt * dec + u_scr[slot, d] * ws_scale
        n_scr[d] = n_row * dec + ks_scr[slot, d] * ws_scale
        m_scr[d] = m_new
        m2_scr[d] = jnp.maximum(blast2_scr[d, pl.ds(cc, 1), :] + m_old2, wsmax2_scr[d, pl.ds(cc, 1), :])

    _scan_pipeline(local, state)
    _scan_epilogue(of_scr, ob_scr, gate_ref, ng_ref, out_ref)


def _mlstm(q, k, v, gate, gcol, grow, ng):
    bsz = q.shape[0]
    pairs = ML_HEADS // 2
    return pl.pallas_call(
        _mlstm_kernel,
        out_shape=jax.ShapeDtypeStruct((bsz, S_ALL, ML_HEADS * ML_DV), BF16),
        grid=(bsz, pairs),
        in_specs=[
            pl.BlockSpec((1, S_ALL, 128), lambda b, p: (b, 0, p)),
            pl.BlockSpec((1, S_ALL, 128), lambda b, p: (b, 0, p)),
            pl.BlockSpec((1, S_ALL, 256), lambda b, p: (b, 0, p)),
            pl.BlockSpec((1, S_ALL, 256), lambda b, p: (b, 0, p)),
            pl.BlockSpec((1, 1, S_ALL, 8), lambda b, p: (b, p, 0, 0)),
            pl.BlockSpec((1, 2, 2, 1, N_CHUNKS, 128), lambda b, p: (b, 0, 0, p, 0, 0)),
            pl.BlockSpec((1, 1, 256), lambda b, p: (p, 0, 0)),
        ],
        out_specs=pl.BlockSpec((1, S_ALL, 256), lambda b, p: (b, 0, p)),
        scratch_shapes=[
            pltpu.VMEM((2, S_ALL, 128), F32),
            pltpu.VMEM((2, S_ALL, 128), F32),
            pltpu.VMEM((2, N_CHUNKS, 128), F32),
            pltpu.VMEM((2, N_CHUNKS, 128), F32),
            pltpu.VMEM((2, N_CHUNKS, 128), F32),
            pltpu.VMEM((S_ALL, 256), F32),
            pltpu.VMEM((S_ALL, 256), F32),
            pltpu.VMEM((2, 2 * ML_DV, 2 * ML_DQK), F32),
            pltpu.VMEM((2, 1, 2 * ML_DQK), F32),
            pltpu.VMEM((2, 1, 2 * ML_DQK), F32),
            pltpu.VMEM((2, 2, CHUNK, 2 * ML_DV), F32),
            pltpu.VMEM((2, 2, CHUNK, 2 * ML_DV), F32),
            pltpu.VMEM((2, 2, CHUNK, 2 * ML_DV), F32),
            pltpu.VMEM((2, 2, 2 * ML_DV, 2 * ML_DQK), F32),
            pltpu.VMEM((2, 2, 1, 2 * ML_DQK), F32),
            pltpu.VMEM((2, 1, 2 * ML_DV), F32),
            pltpu.VMEM((2, N_CHUNKS, 2 * ML_DV), F32),
            pltpu.VMEM((2, N_CHUNKS, 2 * ML_DV), F32),
        ],
        compiler_params=_params("parallel", "arbitrary"),
        name="mlstm_scan",
    )(q, k, v, gate, gcol, grow, ng)


def _rpb_table_kernel(rpb_ref, o_ref):
    h = pl.program_id(0)
    n_dc = 2 * NA_KW - 1
    qi = lax.broadcasted_iota(jnp.int32, (GRID_W, 128), 0)
    lane = lax.broadcasted_iota(jnp.int32, (GRID_W, 128), 1)
    wi = lane & (GRID_W - 1)
    second = lane >= GRID_W
    dc = jnp.clip(wi - qi + (NA_KW - 1), 0, n_dc - 1)
    cs = jnp.clip(qi - NA_KW // 2, 0, GRID_W - NA_KW)
    col_ok = (wi >= cs) & (wi < cs + NA_KW)
    for dr in range(2 * NA_KH - 2):
        base = h * ((2 * NA_KH - 1) * n_dc) + dr * n_dc
        acc = jnp.zeros((GRID_W, 128), F32)
        for c in range(n_dc):
            val = jnp.where(second, rpb_ref[base + n_dc + c], rpb_ref[base + c])
            acc = jnp.where(dc == c, val, acc)
        o_ref[0, dr] = jnp.where(col_ok, acc * LOG2E, NEG)


def _rpb_table(rpb):
    flat = rpb.reshape(-1)
    return pl.pallas_call(
        _rpb_table_kernel,
        out_shape=jax.ShapeDtypeStruct((NA_HEADS, 2 * NA_KH - 2, GRID_W, 128), F32),
        grid=(NA_HEADS,),
        in_specs=[pl.BlockSpec(memory_space=pltpu.SMEM)],
        out_specs=pl.BlockSpec((1, 2 * NA_KH - 2, GRID_W, 128), lambda h: (h, 0, 0, 0)),
        compiler_params=_params("arbitrary"),
        name="na_rpb_table",
    )(flat)


def _softmax_pv(s_parts, v_parts, extra_logit=None):
    m = s_parts[0].max(axis=-1, keepdims=True)
    for s in s_parts[1:]:
        m = jnp.maximum(m, s.max(axis=-1, keepdims=True))
    if extra_logit is not None:
        m = jnp.maximum(m, extra_logit)
    l = 0.0 if extra_logit is None else jnp.exp2(extra_logit - m)
    o = None
    for s, v in zip(s_parts, v_parts):
        p = jnp.exp2(s - m)
        l = l + p.sum(axis=-1, keepdims=True)
        pv = _dot(p.astype(BF16), v)
        o = pv if o is None else o + pv
    return o / l


def _stack_heads(x):
    head0 = lax.broadcasted_iota(jnp.int32, x.shape, 1) < 64
    zero = jnp.zeros_like(x)
    return jnp.concatenate([jnp.where(head0, x, zero), jnp.where(head0, zero, x)], axis=0)


def _merge_heads(o):
    n = o.shape[0] // 2
    head0 = lax.broadcasted_iota(jnp.int32, (n, o.shape[1]), 1) < 64
    return jnp.where(head0, o[:n], o[n:])


def _na_kernel(q_ref, k_ref, v_ref, gate_ref, tab_ref, out_ref, s_scr):
    kc = k_ref[0, 0:CTX_LEN, :]
    vc = v_ref[0, 0:CTX_LEN, :]
    n_lat = NA_KH * GRID_W

    def rows_of(r):
        rs = jnp.clip(r - NA_KH // 2, 0, GRID_ROWS - NA_KH)
        q_rows = pl.ds(pl.multiple_of(CTX_LEN + r * GRID_W, GRID_W), GRID_W)
        k_rows = pl.ds(pl.multiple_of(CTX_LEN + rs * GRID_W, GRID_W), n_lat)
        return rs, q_rows, k_rows

    def scores(r, slot):
        rs, q_rows, k_rows = rows_of(r)
        dr0 = rs - r + NA_KH - 1
        qs = _stack_heads(q_ref[0, q_rows, :])
        bias = jnp.concatenate(
            [jnp.concatenate([tab_ref[hh, dr0 + 2 * j] for j in range(NA_KH // 2)], axis=1)
             for hh in range(2)], axis=0)
        s_scr[slot, :, 0:n_lat] = _dot_nt(qs, k_ref[0, k_rows, :]) + bias
        s_scr[slot, :, n_lat:n_lat + CTX_LEN] = _dot_nt(qs, kc)

    def finish(r, slot):
        _, q_rows, k_rows = rows_of(r)
        s_lat = s_scr[slot, :, 0:n_lat]
        s_ctx = s_scr[slot, :, n_lat:n_lat + CTX_LEN]
        o = _merge_heads(_softmax_pv([s_lat, s_ctx], [v_ref[0, k_rows, :], vc]))
        out_ref[0, q_rows, :] = (o * _silu(gate_ref[0, q_rows, :].astype(F32))).astype(BF16)

    scores(0, 0)

    def body(i, carry):
        for u in range(NA_UNROLL):
            r = i * NA_UNROLL + u
            scores(jnp.minimum(r + 1, GRID_ROWS - 1), (u + 1) % 2)
            finish(r, u % 2)
        return carry

    lax.fori_loop(0, GRID_ROWS // NA_UNROLL, body, 0)

    for t in range(CTX_LEN // 128):
        rows = slice(t * 128, (t + 1) * 128)
        qs = _stack_heads(q_ref[0, rows, :])
        o = _merge_heads(_softmax_pv([_dot_nt(qs, kc)], [vc]))
        out_ref[0, rows, :] = (o * _silu(gate_ref[0, rows, :].astype(F32))).astype(BF16)


def _na(q, k, v, gate, table):
    bsz = q.shape[0]
    pairs = NA_HEADS // 2
    seq_spec = pl.BlockSpec((1, S_ALL, 128), lambda b, p: (b, 0, p))
    return pl.pallas_call(
        _na_kernel,
        out_shape=jax.ShapeDtypeStruct((bsz, S_ALL, NA_HEADS * NA_DH), BF16),
        grid=(bsz, pairs),
        in_specs=[seq_spec, seq_spec, seq_spec, seq_spec,
                  pl.BlockSpec((2, 2 * NA_KH - 2, GRID_W, 128), lambda b, p: (p, 0, 0, 0))],
        out_specs=seq_spec,
        scratch_shapes=[pltpu.VMEM((2, 2 * GRID_W, NA_KH * GRID_W + CTX_LEN), F32)],
        compiler_params=_params("parallel", "arbitrary"),
        name="na_attn",
    )(q, k, v, gate, table)


def _swa_mask_table():
    span = SW_BLOCK + 2 * SW_WINDOW
    r = np.arange(SW_BLOCK)[:, None]
    c = np.arange(span)[None, :]
    cases = [np.abs(r + off - c) <= SW_WINDOW for off in (0, SW_WINDOW, 2 * SW_WINDOW)]
    cases.append(np.zeros((SW_BLOCK, span), bool))
    return jnp.asarray(np.where(np.stack(cases), 0.0, NEG), F32)


def _swa_kernel(sink_ref, q_ref, k_ref, v_ref, gate_ref, mask_ref, out_ref, s_scr):
    n_blocks = S_ALL // SW_BLOCK
    ctx_blocks = CTX_LEN // SW_BLOCK
    n_tiles = SW_HEADS // 2
    span = SW_BLOCK + 2 * SW_WINDOW
    kc = k_ref[0, 0:CTX_LEN, :]
    vc = v_ref[0, 0:CTX_LEN, :]
    first = lax.broadcasted_iota(jnp.int32, (2 * SW_BLOCK, 1), 0) < SW_BLOCK

    def rows_of(n):
        qstart = (n - ctx_blocks) * SW_BLOCK
        kstart = jnp.clip(qstart - SW_WINDOW, 0, SEQ - span)
        q_rows = pl.ds(pl.multiple_of(n * SW_BLOCK, SW_BLOCK), SW_BLOCK)
        k_rows = pl.ds(pl.multiple_of(CTX_LEN + kstart, SW_BLOCK), span)
        return q_rows, k_rows

    def scores(n, t, slot):
        q_rows, k_rows = rows_of(n)
        case = jnp.where(n < ctx_blocks, 3,
                         jnp.where(n == ctx_blocks, 0, jnp.where(n == n_blocks - 1, 2, 1)))
        qs = _stack_heads(q_ref[0, q_rows, t * 128:(t + 1) * 128])
        mb = mask_ref[case]
        s_scr[slot, :, 0:span] = _dot_nt(qs, k_ref[0, k_rows, :]) + jnp.concatenate([mb, mb], axis=0)
        s_scr[slot, :, span:span + CTX_LEN] = _dot_nt(qs, kc)

    def finish(n, t, slot):
        q_rows, k_rows = rows_of(n)
        cols = slice(t * 128, (t + 1) * 128)
        sink = jnp.where(first, sink_ref[t], sink_ref[t + n_tiles]) * LOG2E
        o = _merge_heads(_softmax_pv([s_scr[slot, :, 0:span], s_scr[slot, :, span:span + CTX_LEN]],
                                     [v_ref[0, k_rows, :], vc], extra_logit=sink))
        out_ref[0, q_rows, cols] = (o * _silu(gate_ref[0, q_rows, cols].astype(F32))).astype(BF16)

    scores(0, 0, 0)

    def body(n, carry):
        for t in range(n_tiles):
            if t + 1 < n_tiles:
                scores(n, t + 1, (t + 1) % 2)
            else:
                scores(jnp.minimum(n + 1, n_blocks - 1), 0, 0)
            finish(n, t, t % 2)
        return carry

    lax.fori_loop(0, n_blocks, body, 0)


def _swa(q, k, v, gate, sink):
    bsz = q.shape[0]
    width = SW_HEADS * SW_DH
    span = SW_BLOCK + 2 * SW_WINDOW
    seq_spec = pl.BlockSpec((1, S_ALL, width), lambda b: (b, 0, 0))
    kv_spec = pl.BlockSpec((1, S_ALL, 128), lambda b: (b, 0, 0))
    return pl.pallas_call(
        _swa_kernel,
        out_shape=jax.ShapeDtypeStruct((bsz, S_ALL, width), BF16),
        grid=(bsz,),
        in_specs=[pl.BlockSpec(memory_space=pltpu.SMEM), seq_spec, kv_spec, kv_spec, seq_spec,
                  pl.BlockSpec((4, SW_BLOCK, span), lambda b: (0, 0, 0))],
        out_specs=seq_spec,
        scratch_shapes=[pltpu.VMEM((2, 2 * SW_BLOCK, span + CTX_LEN), F32)],
        compiler_params=_params("parallel"),
        name="swa_attn",
    )(sink, q, k, v, gate, _swa_mask_table())


def _pad_cols(w, width):
    return jnp.pad(w, ((0, 0), (0, width - w.shape[1])))


def _prep_even(w_in, gate_w, gate_b):
    o = np.cumsum((0, 256, 256, 512, 512, 32, 512, 512, 512, 512))
    gq, gk, gv, gg, lra, nq, nk, nv, ng = [w_in[:, o[i]:o[i + 1]] for i in range(9)]
    w = jnp.concatenate([gq * GLA_DK ** -0.5, gk, gv, gg, nq * (NA_DH ** -0.5 * LOG2E), nk, nv, ng,
                         _pad_cols(lra, 128)], axis=1).astype(BF16)
    gws, gbs = [], []
    for p in range(GLA_HEADS // 2):
        cols = slice(p * 128, (p + 1) * 128)
        zero = jnp.zeros((GLA_RANK, 128), F32)
        gws.append(jnp.concatenate([jnp.concatenate([gate_w[0][:, cols], zero], axis=1),
                                    jnp.concatenate([zero, gate_w[1][:, cols]], axis=1)], axis=0))
        gbs.append(jnp.concatenate([gate_b[0][cols], gate_b[1][cols]])[None])
    return w, jnp.stack(gws).astype(BF16), jnp.stack(gbs)


def _swa_head_perm():
    order = []
    for t in range(SW_HEADS // 2):
        order += list(range(t * SW_DH, (t + 1) * SW_DH))
        order += list(range((t + SW_HEADS // 2) * SW_DH, (t + SW_HEADS // 2 + 1) * SW_DH))
    return np.asarray(order)


def _rope_swap(w):
    nf = SW_DH // 4
    idx = np.arange(w.shape[1]).reshape(-1, 2, nf)[:, ::-1, :].reshape(-1)
    return w[:, idx]


def _prep_odd(w_in, w_out):
    o = np.cumsum((0, 256, 256, 512, 512, 16, 512, 128, 128, 512))
    mq, mk, mv, mg, gates, sq, sk, sv, sg = [w_in[:, o[i]:o[i + 1]] for i in range(9)]
    perm = _swa_head_perm()
    sq = sq[:, perm] * (SW_DH ** -0.5 * LOG2E)
    w = jnp.concatenate([mq, mk * ML_DQK ** -0.5, mv, mg, sq, _rope_swap(sq), sk, _rope_swap(sk),
                         sv, _pad_cols(gates, 128), sg[:, perm]], axis=1).astype(BF16)
    half = ML_HEADS * ML_DV
    w_o = jnp.concatenate([w_out[:half], w_out[half:][perm]], axis=0).astype(BF16)
    return w, w_o


def _rope_tables():
    nf = SW_DH // 4
    freqs = ROPE_BASE ** (-jnp.arange(nf, dtype=F32) / nf)
    pos = jnp.arange(SEQ)
    rows = (pos // GRID_W).astype(F32)
    cols = (pos % GRID_W).astype(F32)
    ar = rows[:, None] * freqs[None, :]
    ac = cols[:, None] * freqs[None, :]
    cos = jnp.concatenate([jnp.cos(ar), jnp.cos(ar), jnp.cos(ac), jnp.cos(ac)], axis=1)
    sin = jnp.concatenate([-jnp.sin(ar), jnp.sin(ar), -jnp.sin(ac), jnp.sin(ac)], axis=1)
    cos = jnp.concatenate([jnp.ones((CTX_LEN, SW_DH), F32), cos], axis=0)
    sin = jnp.concatenate([jnp.zeros((CTX_LEN, SW_DH), F32), sin], axis=0)
    return jnp.tile(cos, (1, 2)), jnp.tile(sin, (1, 2))


def _gate_layouts(gates):
    bsz = gates.shape[0]
    g = gates.reshape(bsz, S_ALL, 2, 2, 2, 2)
    gcol = g.transpose(0, 4, 1, 2, 3, 5).reshape(bsz, 2, S_ALL, 8)
    g = gates.reshape(bsz, N_CHUNKS, CHUNK, 2, 2, 2, 2)
    grow = g.transpose(0, 3, 4, 5, 1, 6, 2).reshape(bsz, 2, 2, 2, N_CHUNKS, 2 * CHUNK)
    return gcol, grow


def kernel(x, c, ctx, c_ctx, w_ada, b_ada, ln_g, ln_b, w_in_even, w_out_even, gla_gate_w, gla_gate_b,
           gla_norm_g, na_rpb, w_in_odd, w_out_odd, ml_gate_b, ml_norm_g, sw_sink):
    bsz = x.shape[0]
    mod_rows = 16
    cvec = jnp.zeros((mod_rows, D_MODEL), F32).at[:bsz].set(c).at[bsz].set(c_ctx)
    ada = _ada(cvec, w_ada, b_ada)
    xc = jnp.concatenate([ctx, x], axis=1)
    cos_t, sin_t = _rope_tables()
    for l in range(DEPTH):
        i = l // 2
        last = l == DEPTH - 1
        lat = ada[l, :bsz].reshape(bsz, 1, 3, D_MODEL)
        cx = jnp.broadcast_to(ada[l, bsz].reshape(1, 1, 3, D_MODEL), (bsz, 1, 3, D_MODEL))
        mod = jnp.concatenate([lat, cx], axis=1)
        if l % 2 == 0:
            w, gw, gb = _prep_even(w_in_even[i], gla_gate_w[i], gla_gate_b[i])
            gq, gk, gv, gg, nq, nk, nv, ng, lr = _inproj_even(xc, mod, w)
            ya = _gla(gq, gk, gv, gg, lr, gw, gb, gla_norm_g[i].reshape(GLA_HEADS // 2, 1, 256))
            yb = _na(nq, nk, nv, ng, _rpb_table(na_rpb[i]))
            w_o = w_out_even[i].astype(BF16)
        else:
            w, w_o = _prep_odd(w_in_odd[i], w_out_odd[i])
            mq, mk, mv, mg, sq, sk, sv, sg, gates = _inproj_odd(
                xc, mod, w, cos_t, sin_t, ml_gate_b[i].reshape(1, 4 * ML_HEADS))
            gcol, grow = _gate_layouts(gates)
            ya = _mlstm(mq, mk, mv, mg, gcol, grow, ml_norm_g[i].reshape(ML_HEADS // 2, 1, 256))
            yb = _swa(sq, sk, sv, sg, sw_sink[i])
        xc = _outproj(ya, yb, xc, mod, w_o, ln_g[l].reshape(1, D_MODEL), ln_b[l].reshape(1, D_MODEL), last)
    return xc
```

```python
import functools

import numpy as np
import jax
import jax.numpy as jnp
from jax import lax
from jax.experimental import pallas as pl
from jax.experimental.pallas import tpu as pltpu

F32 = jnp.float32
BF16 = jnp.bfloat16

D_MODEL = 1024
SEQ = 4096
DEPTH = 4
GRID_W = 64
CTX_LEN = 256
S_ALL = CTX_LEN + SEQ
ALPHA = (2.0 * DEPTH) ** 0.25
LN_EPS = 1e-5
NORM_EPS = 1e-6
NEG = -1e30
LOG2E = 1.4426950408889634
GLA_HEADS, GLA_DK, GLA_DV, GLA_RANK, GLA_TAU = 4, 64, 128, 16, 16.0
NA_HEADS, NA_DH, NA_KH, NA_KW = 8, 64, 8, 16
ML_HEADS, ML_DQK, ML_DV = 4, 64, 128
SW_HEADS, SW_KV_HEADS, SW_DH, SW_WINDOW, SW_BLOCK = 8, 2, 64, 128, 128
ROPE_BASE = 10000.0
CHUNK = 64
N_CHUNKS = S_ALL // CHUNK
CTX_CHUNKS = CTX_LEN // CHUNK
ROW_TILE = 256
N_ROW_TILES = S_ALL // ROW_TILE
GRID_ROWS = SEQ // GRID_W
NA_UNROLL = 4

VMEM_LIMIT = 56 * 1024 * 1024

NT_DIMS = (((1,), (1,)), ((), ()))
TN_DIMS = (((0,), (0,)), ((), ()))


def _dot(a, b):
    return jnp.dot(a, b, preferred_element_type=F32)


def _dot_nt(a, b):
    return lax.dot_general(a, b, NT_DIMS, preferred_element_type=F32)


def _dot_tn(a, b):
    return lax.dot_general(a, b, TN_DIMS, preferred_element_type=F32)


def _sigmoid(x):
    return 1.0 / (1.0 + jnp.exp(-x))


def _silu(x):
    return x * _sigmoid(x)


def _log_sigmoid(x):
    return jnp.minimum(x, 0.0) - jnp.log1p(jnp.exp(-jnp.abs(x)))


def _cumsum_rows(x, reverse):
    n = x.shape[0]
    row = lax.broadcasted_iota(jnp.int32, x.shape, 0)
    k = 1
    while k < n:
        if reverse:
            x = x + jnp.where(row < n - k, pltpu.roll(x, n - k, axis=0), 0.0)
        else:
            x = x + jnp.where(row >= k, pltpu.roll(x, k, axis=0), 0.0)
        k *= 2
    return x


def _split_dot(mat_bf16, x):
    hi = x.astype(BF16)
    lo = (x - hi.astype(F32)).astype(BF16)
    return _dot(mat_bf16, hi) + _dot(mat_bf16, lo)


def _split_dot_r(x, mat_bf16):
    hi = x.astype(BF16)
    lo = (x - hi.astype(F32)).astype(BF16)
    return _dot(hi, mat_bf16) + _dot(lo, mat_bf16)


def _halves_to_tiles(x):
    half0 = lax.broadcasted_iota(jnp.int32, x.shape, 1) < 64
    sw = pltpu.roll(x, 64, axis=1)
    return jnp.concatenate([jnp.where(half0, x, sw), jnp.where(half0, sw, x)], axis=1)


def _params(*sem):
    return pltpu.CompilerParams(dimension_semantics=sem, vmem_limit_bytes=VMEM_LIMIT)


def _ada_kernel(c_ref, w_ref, b_ref, o_ref):
    cond = _silu(c_ref[...])
    o_ref[0] = jnp.dot(cond, w_ref[0], precision=lax.Precision.HIGHEST,
                       preferred_element_type=F32) + b_ref[0]


def _ada(cvec, w_ada, b_ada):
    rows = cvec.shape[0]
    nblk = 3
    return pl.pallas_call(
        _ada_kernel,
        out_shape=jax.ShapeDtypeStruct((DEPTH, rows, 3 * D_MODEL), F32),
        grid=(DEPTH, nblk),
        in_specs=[
            pl.BlockSpec((rows, D_MODEL), lambda l, n: (0, 0)),
            pl.BlockSpec((1, D_MODEL, D_MODEL), lambda l, n: (l, 0, n)),
            pl.BlockSpec((1, 1, D_MODEL), lambda l, n: (l, 0, n)),
        ],
        out_specs=pl.BlockSpec((1, rows, D_MODEL), lambda l, n: (l, 0, n)),
        compiler_params=_params("arbitrary", "arbitrary"),
        name="ada_mod",
    )(cvec, w_ada, b_ada.reshape(DEPTH, 1, 3 * D_MODEL))


def _row_tile(x_refs, is_ctx):
    if len(x_refs) == 1:
        return x_refs[0][0]
    return jnp.where(is_ctx, x_refs[0][0], x_refs[1][0])


def _x_specs(xs, tile_of):
    def combined(*ids):
        b, j = tile_of(*ids)
        return (b, j, 0)

    def context(*ids):
        return (tile_of(*ids)[0], 0, 0)

    def latent(*ids):
        b, j = tile_of(*ids)
        return (b, jnp.maximum(j - 1, 0), 0)

    if len(xs) == 1:
        return [pl.BlockSpec((1, ROW_TILE, D_MODEL), combined)]
    return [pl.BlockSpec((1, ROW_TILE, D_MODEL), context), pl.BlockSpec((1, ROW_TILE, D_MODEL), latent)]


def _modulated(x_refs, mod_ref):
    shift = mod_ref[0, 0, 0:1, :]
    scale = mod_ref[0, 0, 1:2, :]
    x = _row_tile(x_refs, pl.program_id(1) == 0)
    return (x * (1.0 + scale) + shift).astype(BF16)


EVEN_SLABS = (256, 256, 512, 512, 512, 512, 512, 512)


def _inproj_even_kernel(*refs, n_x):
    x_refs, (mod_ref, w_ref), outs = refs[:n_x], refs[n_x:n_x + 2], refs[n_x + 2:]
    h = _modulated(x_refs, mod_ref)
    off = 0
    for ref, width in zip(outs[:-1], EVEN_SLABS):
        ref[0] = _dot(h, w_ref[:, off:off + width]).astype(ref.dtype)
        off += width
    lr = _dot(h, w_ref[:, off:off + 128])
    outs[-1][0] = lr[:, :2 * GLA_RANK]


ODD_SLABS = (256, 256, 512, 512)


def _inproj_odd_kernel(x_ref, mod_ref, w_ref, cos_ref, sin_ref, gb_ref,
                       mq, mk, mv, mg, sq, sk, sv, sg, gates):
    h = _modulated((x_ref,), mod_ref)
    off = 0
    for ref, width in zip((mq, mk, mv, mg), ODD_SLABS):
        ref[0] = _dot(h, w_ref[:, off:off + width]).astype(ref.dtype)
        off += width
    cos = cos_ref[...]
    sin = sin_ref[...]
    a = _dot(h, w_ref[:, off:off + 512])
    b = _dot(h, w_ref[:, off + 512:off + 1024])
    for t in range(4):
        cols = slice(t * 128, (t + 1) * 128)
        sq[0, :, cols] = (a[:, cols] * cos + b[:, cols] * sin).astype(BF16)
    off += 1024
    kvg = _dot(h, w_ref[:, off:off + 512])
    sk[0] = (kvg[:, 0:128] * cos + kvg[:, 128:256] * sin).astype(BF16)
    sv[0] = kvg[:, 256:384].astype(BF16)
    gates[0] = kvg[:, 384:384 + 4 * ML_HEADS] + gb_ref[...]
    off += 512
    sg[0] = _dot(h, w_ref[:, off:off + 512]).astype(BF16)


def _mod_index(b, j):
    return (b, jnp.where(j == 0, 1, 0), 0, 0)


def _inproj_even(xs, mod, w):
    bsz = xs[0].shape[0]
    ntot = w.shape[1]
    out_shape = [jax.ShapeDtypeStruct((bsz, S_ALL, wd), BF16) for wd in EVEN_SLABS]
    out_shape.append(jax.ShapeDtypeStruct((bsz, S_ALL, 2 * GLA_RANK), F32))
    out_specs = [pl.BlockSpec((1, ROW_TILE, wd), lambda b, j: (b, j, 0)) for wd in EVEN_SLABS]
    out_specs.append(pl.BlockSpec((1, ROW_TILE, 2 * GLA_RANK), lambda b, j: (b, j, 0)))
    return pl.pallas_call(
        functools.partial(_inproj_even_kernel, n_x=len(xs)),
        out_shape=out_shape,
        grid=(bsz, N_ROW_TILES),
        in_specs=_x_specs(xs, lambda b, j: (b, j)) + [
            pl.BlockSpec((1, 1, 3, D_MODEL), _mod_index),
            pl.BlockSpec((D_MODEL, ntot), lambda b, j: (0, 0)),
        ],
        out_specs=out_specs,
        compiler_params=_params("parallel", "arbitrary"),
        name="inproj_even",
    )(*xs, mod, w)


def _inproj_odd(xc, mod, w, cos_t, sin_t, gate_b):
    bsz = xc.shape[0]
    ntot = w.shape[1]
    widths = (256, 256, 512, 512, 512, 128, 128, 512)
    out_shape = [jax.ShapeDtypeStruct((bsz, S_ALL, wd), BF16) for wd in widths]
    out_shape.append(jax.ShapeDtypeStruct((bsz, S_ALL, 4 * ML_HEADS), F32))
    out_specs = [pl.BlockSpec((1, ROW_TILE, wd), lambda b, j: (b, j, 0)) for wd in widths]
    out_specs.append(pl.BlockSpec((1, ROW_TILE, 4 * ML_HEADS), lambda b, j: (b, j, 0)))
    return pl.pallas_call(
        _inproj_odd_kernel,
        out_shape=out_shape,
        grid=(bsz, N_ROW_TILES),
        in_specs=[
            pl.BlockSpec((1, ROW_TILE, D_MODEL), lambda b, j: (b, j, 0)),
            pl.BlockSpec((1, 1, 3, D_MODEL), _mod_index),
            pl.BlockSpec((D_MODEL, ntot), lambda b, j: (0, 0)),
            pl.BlockSpec((ROW_TILE, 128), lambda b, j: (j, 0)),
            pl.BlockSpec((ROW_TILE, 128), lambda b, j: (j, 0)),
            pl.BlockSpec((1, 4 * ML_HEADS), lambda b, j: (0, 0)),
        ],
        out_specs=out_specs,
        compiler_params=_params("parallel", "arbitrary"),
        name="inproj_odd",
    )(xc, mod, w, cos_t, sin_t, gate_b)


def _outproj_kernel(*refs, n_x, n_tiles):
    ya_ref, yb_ref = refs[:2]
    x_refs = refs[2:2 + n_x]
    mod_ref, w_ref, g_ref, b_ref, o_ref, r_scr = refs[2 + n_x:]
    t = pl.program_id(0)

    @pl.when(t == 0)
    def _():
        r_scr[...] = jnp.zeros(r_scr.shape, F32)

    r = r_scr[...]
    mu = jnp.mean(r, axis=-1, keepdims=True)
    d = r - mu
    var = jnp.mean(d * d, axis=-1, keepdims=True)
    o_ref[0] = d * lax.rsqrt(var + LN_EPS) * g_ref[...] + b_ref[...]

    half = ya_ref.shape[2]
    y = _dot(ya_ref[0], w_ref[0:half, :]) + _dot(yb_ref[0], w_ref[half:2 * half, :])
    x = _row_tile(x_refs, lax.rem(t, n_tiles) == 0)
    r_scr[...] = ALPHA * x + mod_ref[0, 0, 2:3, :] * y


def _outproj(ya, yb, xs, mod, w, ln_g, ln_b, latent_only):
    bsz = xs[0].shape[0]
    skip = 1 if latent_only else 0
    n_tiles = N_ROW_TILES - skip
    total = bsz * n_tiles
    half = ya.shape[2]

    def tile_of(t):
        tt = jnp.minimum(t, total - 1)
        return tt // n_tiles, lax.rem(tt, n_tiles) + skip

    def y_index(t):
        b, j = tile_of(t)
        return (b, j, 0)

    def mod_index(t):
        return _mod_index(*tile_of(t))

    def out_index(t):
        tt = jnp.maximum(t - 1, 0)
        return (tt // n_tiles, lax.rem(tt, n_tiles), 0)

    const = lambda t: (0, 0)
    return pl.pallas_call(
        functools.partial(_outproj_kernel, n_x=len(xs), n_tiles=n_tiles),
        out_shape=jax.ShapeDtypeStruct((bsz, n_tiles * ROW_TILE, D_MODEL), F32),
        grid=(total + 1,),
        in_specs=[pl.BlockSpec((1, ROW_TILE, half), y_index), pl.BlockSpec((1, ROW_TILE, half), y_index)]
        + _x_specs(xs, tile_of) + [
            pl.BlockSpec((1, 1, 3, D_MODEL), mod_index),
            pl.BlockSpec((2 * half, D_MODEL), const),
            pl.BlockSpec((1, D_MODEL), const),
            pl.BlockSpec((1, D_MODEL), const),
        ],
        out_specs=pl.BlockSpec((1, ROW_TILE, D_MODEL), out_index),
        scratch_shapes=[pltpu.VMEM((ROW_TILE, D_MODEL), F32)],
        compiler_params=_params("arbitrary"),
        name="outproj_ln",
    )(ya, yb, *xs, mod, w, ln_g, ln_b)


def _scan_epilogue(of_scr, ob_scr, gate_ref, ng_ref, out_ref):
    def body(t, carry):
        rows = pl.ds(pl.multiple_of(t * ROW_TILE, ROW_TILE), ROW_TILE)
        o = of_scr[rows, :] + ob_scr[rows, :]
        parts = []
        for hh in range(2):
            oh = o[:, hh * 128:(hh + 1) * 128]
            ms = jnp.mean(oh * oh, axis=-1, keepdims=True)
            parts.append(oh * lax.rsqrt(ms + NORM_EPS))
        y = jnp.concatenate(parts, axis=1) * ng_ref[0]
        out_ref[0, rows, :] = (y * _silu(gate_ref[0, rows, :].astype(F32))).astype(BF16)
        return carry

    lax.fori_loop(0, N_ROW_TILES, body, 0)


def _bwd_chunk(i):
    return jnp.where(i < CTX_CHUNKS, CTX_CHUNKS - 1 - i, N_CHUNKS - 1 + CTX_CHUNKS - i)


def _scan_pipeline(local_fn, state_fn):
    local_fn(0, 0, 0)
    local_fn(CTX_CHUNKS - 1, 1, 0)

    def body(j, carry):
        for u in range(2):
            i = 2 * j + u
            nxt = jnp.minimum(i + 1, N_CHUNKS - 1)
            local_fn(nxt, 0, (u + 1) % 2)
            local_fn(_bwd_chunk(nxt), 1, (u + 1) % 2)
            state_fn(i, 0, u % 2)
            state_fn(_bwd_chunk(i), 1, u % 2)
        return carry

    lax.fori_loop(0, N_CHUNKS // 2, body, 0)


def _gla_kernel(q_ref, k_ref, v_ref, gate_ref, lr_ref, gw_ref, gb_ref, ng_ref, out_ref,
                g_scr, of_scr, ob_scr, st_scr, a_scr, qin_scr, u_scr, dl_scr):
    def pre(t, carry):
        rows = pl.ds(pl.multiple_of(t * ROW_TILE, ROW_TILE), ROW_TILE)
        z = _dot(lr_ref[0, rows, :].astype(BF16), gw_ref[0]) + gb_ref[0]
        g_scr[rows, :] = _log_sigmoid(z) * (1.0 / GLA_TAU)
        return carry

    lax.fori_loop(0, N_ROW_TILES, pre, 0)
    st_scr[...] = jnp.zeros(st_scr.shape, F32)

    def local(cc, d, slot):
        rows = pl.ds(pl.multiple_of(cc * CHUNK, CHUNK), CHUNK)
        g = g_scr[rows, d * 128:(d + 1) * 128]
        b = _cumsum_rows(g, reverse=(d == 1))
        b_last = b[0:1, :] if d == 1 else b[CHUNK - 1:CHUNK, :]
        q = q_ref[0, rows, :].astype(F32)
        k = k_ref[0, rows, :].astype(F32)
        q_in = q * jnp.exp(b)
        k_in = (k * jnp.exp(-b)).astype(BF16)
        k_dec = (k * jnp.exp(b_last - b)).astype(BF16)

        head0 = lax.broadcasted_iota(jnp.int32, (CHUNK, 128), 1) < GLA_DK
        q_stack = jnp.concatenate([jnp.where(head0, q_in, 0.0), jnp.where(head0, 0.0, q_in)],
                                  axis=0).astype(BF16)
        a = _dot_nt(q_stack, k_in)
        t_i = lax.broadcasted_iota(jnp.int32, (2 * CHUNK, CHUNK), 0) & (CHUNK - 1)
        s_i = lax.broadcasted_iota(jnp.int32, (2 * CHUNK, CHUNK), 1)
        keep = (s_i >= t_i) if d == 1 else (s_i <= t_i)
        a_scr[slot, d] = jnp.where(keep, a, 0.0).astype(BF16)
        qin_scr[slot, d] = q_in.astype(BF16)

        u = _dot_tn(v_ref[0, rows, :], k_dec)
        r_i = lax.broadcasted_iota(jnp.int32, u.shape, 0) < GLA_DV
        c_i = lax.broadcasted_iota(jnp.int32, u.shape, 1) < GLA_DK
        u_scr[slot, d] = jnp.where(r_i == c_i, u, 0.0)
        dl_scr[slot, d] = jnp.exp(b_last)

    def state(cc, d, slot):
        rows = pl.ds(pl.multiple_of(cc * CHUNK, CHUNK), CHUNK)
        v = v_ref[0, rows, :]
        a = a_scr[slot, d]
        vhead0 = lax.broadcasted_iota(jnp.int32, (CHUNK, 2 * GLA_DV), 1) < GLA_DV
        zero = jnp.zeros_like(v)
        st = st_scr[d]
        o = (_dot(a[:CHUNK], jnp.where(vhead0, v, zero))
             + _dot(a[CHUNK:], jnp.where(vhead0, zero, v))
             + _dot_nt(qin_scr[slot, d], st.astype(BF16)))
        if d == 0:
            of_scr[rows, :] = o
        else:
            ob_scr[rows, :] = o
        st_scr[d] = st * dl_scr[slot, d] + u_scr[slot, d]

    _scan_pipeline(local, state)
    _scan_epilogue(of_scr, ob_scr, gate_ref, ng_ref, out_ref)


def _gla(q, k, v, gate, lr, gw, gb, ng):
    bsz = q.shape[0]
    pairs = GLA_HEADS // 2
    return pl.pallas_call(
        _gla_kernel,
        out_shape=jax.ShapeDtypeStruct((bsz, S_ALL, GLA_HEADS * GLA_DV), BF16),
        grid=(bsz, pairs),
        in_specs=[
            pl.BlockSpec((1, S_ALL, 128), lambda b, p: (b, 0, p)),
            pl.BlockSpec((1, S_ALL, 128), lambda b, p: (b, 0, p)),
            pl.BlockSpec((1, S_ALL, 256), lambda b, p: (b, 0, p)),
            pl.BlockSpec((1, S_ALL, 256), lambda b, p: (b, 0, p)),
            pl.BlockSpec((1, S_ALL, 2 * GLA_RANK), lambda b, p: (b, 0, 0)),
            pl.BlockSpec((1, 2 * GLA_RANK, 256), lambda b, p: (p, 0, 0)),
            pl.BlockSpec((1, 1, 256), lambda b, p: (p, 0, 0)),
            pl.BlockSpec((1, 1, 256), lambda b, p: (p, 0, 0)),
        ],
        out_specs=pl.BlockSpec((1, S_ALL, 256), lambda b, p: (b, 0, p)),
        scratch_shapes=[
            pltpu.VMEM((S_ALL, 256), F32),
            pltpu.VMEM((S_ALL, 256), F32),
            pltpu.VMEM((S_ALL, 256), F32),
            pltpu.VMEM((2, 2 * GLA_DV, 2 * GLA_DK), F32),
            pltpu.VMEM((2, 2, 2 * CHUNK, CHUNK), BF16),
            pltpu.VMEM((2, 2, CHUNK, 2 * GLA_DK), BF16),
            pltpu.VMEM((2, 2, 2 * GLA_DV, 2 * GLA_DK), F32),
            pltpu.VMEM((2, 2, 1, 2 * GLA_DK), F32),
        ],
        compiler_params=_params("parallel", "arbitrary"),
        name="gla_scan",
    )(q, k, v, gate, lr, gw, gb, ng)


def _cummax_rows_chunked(x, reverse):
    n = x.shape[0]
    pos = lax.broadcasted_iota(jnp.int32, x.shape, 0) & (CHUNK - 1)
    k = 1
    while k < CHUNK:
        if reverse:
            x = jnp.maximum(x, jnp.where(pos < CHUNK - k, pltpu.roll(x, n - k, axis=0), NEG))
        else:
            x = jnp.maximum(x, jnp.where(pos >= k, pltpu.roll(x, k, axis=0), NEG))
        k *= 2
    return x


def _mlstm_kernel(q_ref, k_ref, v_ref, gate_ref, gcol_ref, grow_ref, ng_ref, out_ref,
                  bcm_scr, rcm_scr, cmx_scr, rrow_scr, blast_scr, wsmax_scr, blast2_scr, wsmax2_scr,
                  of_scr, ob_scr, st_scr, m2_scr, nd_scr, u_scr):
    lane128 = lax.broadcasted_iota(jnp.int32, (1, 128), 1)
    half0_row = lane128 < CHUNK

    li = lax.broadcasted_iota(jnp.int32, (128, 128), 0)
    lj = lax.broadcasted_iota(jnp.int32, (128, 128), 1)
    same_half = (li < CHUNK) == (lj < CHUNK)
    ones_half = jnp.where(same_half, 1.0, 0.0).astype(BF16)
    for d in range(2):
        within = (li >= lj) if d == 1 else (li <= lj)
        tri = jnp.where(same_half & within, 1.0, 0.0).astype(BF16)
        lf = _log_sigmoid(grow_ref[0, d, 1, 0])
        b_row = _split_dot_r(lf, tri)
        total = _split_dot_r(lf, ones_half)
        r_row = grow_ref[0, d, 0, 0] - b_row
        rrow_scr[d] = r_row
        blast_scr[d] = total
        mx0 = jnp.max(jnp.where(half0_row, r_row, NEG), axis=-1, keepdims=True)
        mx1 = jnp.max(jnp.where(half0_row, NEG, r_row), axis=-1, keepdims=True)
        wsmax = total + jnp.where(half0_row, mx0, mx1)
        wsmax_scr[d] = wsmax
        blast2_scr[d] = _halves_to_tiles(total)
        wsmax2_scr[d] = _halves_to_tiles(wsmax)

    ti = lax.broadcasted_iota(jnp.int32, (ROW_TILE, ROW_TILE), 0)
    tj = lax.broadcasted_iota(jnp.int32, (ROW_TILE, ROW_TILE), 1)
    same_chunk = (ti >> 6) == (tj >> 6)
    tril = jnp.where(same_chunk & (tj <= ti), 1.0, 0.0).astype(BF16)
    triu = jnp.where(same_chunk & (tj >= ti), 1.0, 0.0).astype(BF16)
    half0_tile = lax.broadcasted_iota(jnp.int32, (ROW_TILE, 128), 1) < CHUNK

    def pre(t, carry):
        rows = pl.ds(pl.multiple_of(t * ROW_TILE, ROW_TILE), ROW_TILE)
        gc = gcol_ref[0, 0, rows, :]
        lf = _log_sigmoid(gc)
        for d in range(2):
            bcol = _split_dot(triu if d == 1 else tril, lf)
            b0 = jnp.broadcast_to(bcol[:, d * 4 + 2:d * 4 + 3], (ROW_TILE, 128))
            b1 = jnp.broadcast_to(bcol[:, d * 4 + 3:d * 4 + 4], (ROW_TILE, 128))
            i0 = jnp.broadcast_to(gc[:, d * 4:d * 4 + 1], (ROW_TILE, 128))
            i1 = jnp.broadcast_to(gc[:, d * 4 + 1:d * 4 + 2], (ROW_TILE, 128))
            bc = jnp.where(half0_tile, b0, b1)
            rc = jnp.where(half0_tile, i0, i1) - bc
            bcm_scr[d, rows, :] = bc
            rcm_scr[d, rows, :] = rc
            cmx_scr[d, rows, :] = _cummax_rows_chunked(rc, reverse=(d == 1))
        return carry

    lax.fori_loop(0, N_ROW_TILES, pre, 0)
    st_scr[...] = jnp.zeros(st_scr.shape, F32)
    m2_scr[...] = jnp.zeros(m2_scr.shape, F32)

    def local(cc, d, slot):
        rows = pl.ds(pl.multiple_of(cc * CHUNK, CHUNK), CHUNK)
        rc = rcm_scr[d, rows, :]
        cmx = cmx_scr[d, rows, :]
        rr = rrow_scr[d, pl.ds(cc, 1), :]
        bl = blast_scr[d, pl.ds(cc, 1), :]
        wm = wsmax_scr[d, pl.ds(cc, 1), :]
        q = q_ref[0, rows, :]
        k = k_ref[0, rows, :]
        v = v_ref[0, rows, :]

        half0 = lax.broadcasted_iota(jnp.int32, (CHUNK, 128), 1) < CHUNK
        t_i = lax.broadcasted_iota(jnp.int32, (CHUNK, 128), 0)
        s_i = lax.broadcasted_iota(jnp.int32, (CHUNK, 128), 1) & (CHUNK - 1)
        keep = (s_i >= t_i) if d == 1 else (s_i <= t_i)
        w = jnp.exp(jnp.where(keep, rr - cmx, NEG))

        zero_k = jnp.zeros_like(k)
        k_stack = jnp.concatenate([jnp.where(half0, k, zero_k), jnp.where(half0, zero_k, k)], axis=0)
        s_w = (_dot_nt(q, k_stack) * w).astype(BF16)

        vhead0 = lax.broadcasted_iota(jnp.int32, (CHUNK, 2 * ML_DV), 1) < ML_DV
        zero_v = jnp.zeros_like(v)
        one_v = jnp.ones_like(v)
        ones0 = jnp.where(vhead0, 1.0, 0.0).astype(BF16)
        ones1 = jnp.where(vhead0, 0.0, 1.0).astype(BF16)
        vo_stack = jnp.concatenate(
            [jnp.concatenate([jnp.where(vhead0, v, zero_v), ones0], axis=1),
             jnp.concatenate([jnp.where(vhead0, zero_v, v), ones1], axis=1)],
            axis=0)
        nd_scr[slot, d] = _dot(s_w, vo_stack)

        kws = (k.astype(F32) * jnp.exp(rc + (bl - wm))).astype(BF16)
        u = _dot_tn(kws, jnp.concatenate([v, one_v], axis=1))
        for blk in range(4):
            r0 = (blk // 2) * ML_DQK
            c0 = (blk % 2) * 2 * ML_DV + (blk // 2) * ML_DV
            u_scr[slot, d, blk] = u[r0:r0 + ML_DQK, c0:c0 + ML_DV]

    def state(cc, d, slot):
        rows = pl.ds(pl.multiple_of(cc * CHUNK, CHUNK), CHUNK)
        cmx2 = _halves_to_tiles(cmx_scr[d, rows, :])
        bc2 = _halves_to_tiles(bcm_scr[d, rows, :])
        m_old = m2_scr[d]
        g = jnp.maximum(cmx2, m_old)
        w_loc = jnp.exp(cmx2 - g)
        w_int = jnp.exp(m_old - g)
        floor = jnp.exp(-(bc2 + g))

        blocks = [st_scr[d, blk] for blk in range(4)]
        zero = jnp.zeros((ML_DQK, ML_DV), BF16)
        cm0, n0, cm1, n1 = [x.astype(BF16) for x in blocks]
        st = jnp.concatenate([jnp.concatenate([cm0, zero, n0, zero], axis=1),
                              jnp.concatenate([zero, cm1, zero, n1], axis=1)], axis=0)
        sd = _dot(q_ref[0, rows, :], st)
        nd = nd_scr[slot, d]
        num = w_loc * nd[:, :2 * ML_DV] + w_int * sd[:, :2 * ML_DV]
        den = w_loc * nd[:, 2 * ML_DV:] + w_int * sd[:, 2 * ML_DV:]
        h_out = num / jnp.maximum(jnp.abs(den), floor)
        if d == 0:
            of_scr[rows, :] = h_out
        else:
            ob_scr[rows, :] = h_out

        bl = blast2_scr[d, pl.ds(cc, 1), :]
        wm = wsmax2_scr[d, pl.ds(cc, 1), :]
        m_new = jnp.maximum(bl + m_old, wm)
        dec = jnp.exp(bl + m_old - m_new)
        ws_scale = jnp.exp(wm - m_new)
        for blk in range(4):
            head = slice((blk // 2) * ML_DV, (blk // 2 + 1) * ML_DV)
            st_scr[d, blk] = blocks[blk] * dec[:, head] + u_scr[slot, d, blk] * ws_scale[:, head]
        m2_scr[d] = m_new

    _scan_pipeline(local, state)
    _scan_epilogue(of_scr, ob_scr, gate_ref, ng_ref, out_ref)


def _mlstm(q, k, v, gate, gcol, grow, ng):
    bsz = q.shape[0]
    pairs = ML_HEADS // 2
    return pl.pallas_call(
        _mlstm_kernel,
        out_shape=jax.ShapeDtypeStruct((bsz, S_ALL, ML_HEADS * ML_DV), BF16),
        grid=(bsz, pairs),
        in_specs=[
            pl.BlockSpec((1, S_ALL, 128), lambda b, p: (b, 0, p)),
            pl.BlockSpec((1, S_ALL, 128), lambda b, p: (b, 0, p)),
            pl.BlockSpec((1, S_ALL, 256), lambda b, p: (b, 0, p)),
            pl.BlockSpec((1, S_ALL, 256), lambda b, p: (b, 0, p)),
            pl.BlockSpec((1, 1, S_ALL, 8), lambda b, p: (b, p, 0, 0)),
            pl.BlockSpec((1, 2, 2, 1, N_CHUNKS, 128), lambda b, p: (b, 0, 0, p, 0, 0)),
            pl.BlockSpec((1, 1, 256), lambda b, p: (p, 0, 0)),
        ],
        out_specs=pl.BlockSpec((1, S_ALL, 256), lambda b, p: (b, 0, p)),
        scratch_shapes=[
            pltpu.VMEM((2, S_ALL, 128), F32),
            pltpu.VMEM((2, S_ALL, 128), F32),
            pltpu.VMEM((2, S_ALL, 128), F32),
            pltpu.VMEM((2, N_CHUNKS, 128), F32),
            pltpu.VMEM((2, N_CHUNKS, 128), F32),
            pltpu.VMEM((2, N_CHUNKS, 128), F32),
            pltpu.VMEM((2, N_CHUNKS, 2 * ML_DV), F32),
            pltpu.VMEM((2, N_CHUNKS, 2 * ML_DV), F32),
            pltpu.VMEM((S_ALL, 2 * ML_DV), F32),
            pltpu.VMEM((S_ALL, 2 * ML_DV), F32),
            pltpu.VMEM((2, 4, ML_DQK, ML_DV), F32),
            pltpu.VMEM((2, 1, 2 * ML_DV), F32),
            pltpu.VMEM((2, 2, CHUNK, 4 * ML_DV), F32),
            pltpu.VMEM((2, 2, 4, ML_DQK, ML_DV), F32),
        ],
        compiler_params=_params("parallel", "arbitrary"),
        name="mlstm_scan",
    )(q, k, v, gate, gcol, grow, ng)


def _rpb_table_kernel(rpb_ref, o_ref):
    h = pl.program_id(0)
    n_dc = 2 * NA_KW - 1
    qi = lax.broadcasted_iota(jnp.int32, (GRID_W, 128), 0)
    lane = lax.broadcasted_iota(jnp.int32, (GRID_W, 128), 1)
    wi = lane & (GRID_W - 1)
    second = lane >= GRID_W
    dc = jnp.clip(wi - qi + (NA_KW - 1), 0, n_dc - 1)
    cs = jnp.clip(qi - NA_KW // 2, 0, GRID_W - NA_KW)
    col_ok = (wi >= cs) & (wi < cs + NA_KW)
    for dr in range(2 * NA_KH - 2):
        base = h * ((2 * NA_KH - 1) * n_dc) + dr * n_dc
        acc = jnp.zeros((GRID_W, 128), F32)
        for c in range(n_dc):
            val = jnp.where(second, rpb_ref[base + n_dc + c], rpb_ref[base + c])
            acc = jnp.where(dc == c, val, acc)
        o_ref[0, dr] = jnp.where(col_ok, acc * LOG2E, NEG)


def _rpb_table(rpb):
    flat = rpb.reshape(-1)
    return pl.pallas_call(
        _rpb_table_kernel,
        out_shape=jax.ShapeDtypeStruct((NA_HEADS, 2 * NA_KH - 2, GRID_W, 128), F32),
        grid=(NA_HEADS,),
        in_specs=[pl.BlockSpec(memory_space=pltpu.SMEM)],
        out_specs=pl.BlockSpec((1, 2 * NA_KH - 2, GRID_W, 128), lambda h: (h, 0, 0, 0)),
        compiler_params=_params("arbitrary"),
        name="na_rpb_table",
    )(flat)


def _softmax_pv(s_parts, v_parts, extra_logit=None):
    m = s_parts[0].max(axis=-1, keepdims=True)
    for s in s_parts[1:]:
        m = jnp.maximum(m, s.max(axis=-1, keepdims=True))
    if extra_logit is not None:
        m = jnp.maximum(m, extra_logit)
    l = 0.0 if extra_logit is None else jnp.exp2(extra_logit - m)
    o = None
    for s, v in zip(s_parts, v_parts):
        p = jnp.exp2(s - m)
        l = l + p.sum(axis=-1, keepdims=True)
        pv = _dot(p.astype(BF16), v)
        o = pv if o is None else o + pv
    return o / l


def _stack_heads(x):
    head0 = lax.broadcasted_iota(jnp.int32, x.shape, 1) < 64
    zero = jnp.zeros_like(x)
    return jnp.concatenate([jnp.where(head0, x, zero), jnp.where(head0, zero, x)], axis=0)


def _merge_heads(o):
    n = o.shape[0] // 2
    head0 = lax.broadcasted_iota(jnp.int32, (n, o.shape[1]), 1) < 64
    return jnp.where(head0, o[:n], o[n:])


def _na_kernel(q_ref, k_ref, v_ref, gate_ref, tab_ref, out_ref, s_scr):
    kc = k_ref[0, 0:CTX_LEN, :]
    vc = v_ref[0, 0:CTX_LEN, :]
    n_lat = NA_KH * GRID_W

    def rows_of(r):
        rs = jnp.clip(r - NA_KH // 2, 0, GRID_ROWS - NA_KH)
        q_rows = pl.ds(pl.multiple_of(CTX_LEN + r * GRID_W, GRID_W), GRID_W)
        k_rows = pl.ds(pl.multiple_of(CTX_LEN + rs * GRID_W, GRID_W), n_lat)
        return rs, q_rows, k_rows

    def scores(r, slot):
        rs, q_rows, k_rows = rows_of(r)
        dr0 = rs - r + NA_KH - 1
        qs = _stack_heads(q_ref[0, q_rows, :])
        bias = jnp.concatenate(
            [jnp.concatenate([tab_ref[hh, dr0 + 2 * j] for j in range(NA_KH // 2)], axis=1)
             for hh in range(2)], axis=0)
        s_scr[slot, :, 0:n_lat] = _dot_nt(qs, k_ref[0, k_rows, :]) + bias
        s_scr[slot, :, n_lat:n_lat + CTX_LEN] = _dot_nt(qs, kc)

    def finish(r, slot):
        _, q_rows, k_rows = rows_of(r)
        s_lat = s_scr[slot, :, 0:n_lat]
        s_ctx = s_scr[slot, :, n_lat:n_lat + CTX_LEN]
        o = _merge_heads(_softmax_pv([s_lat, s_ctx], [v_ref[0, k_rows, :], vc]))
        out_ref[0, q_rows, :] = (o * _silu(gate_ref[0, q_rows, :].astype(F32))).astype(BF16)

    scores(0, 0)

    def body(i, carry):
        for u in range(NA_UNROLL):
            r = i * NA_UNROLL + u
            scores(jnp.minimum(r + 1, GRID_ROWS - 1), (u + 1) % 2)
            finish(r, u % 2)
        return carry

    lax.fori_loop(0, GRID_ROWS // NA_UNROLL, body, 0)

    for t in range(CTX_LEN // 128):
        rows = slice(t * 128, (t + 1) * 128)
        qs = _stack_heads(q_ref[0, rows, :])
        o = _merge_heads(_softmax_pv([_dot_nt(qs, kc)], [vc]))
        out_ref[0, rows, :] = (o * _silu(gate_ref[0, rows, :].astype(F32))).astype(BF16)


def _na(q, k, v, gate, table):
    bsz = q.shape[0]
    pairs = NA_HEADS // 2
    seq_spec = pl.BlockSpec((1, S_ALL, 128), lambda b, p: (b, 0, p))
    return pl.pallas_call(
        _na_kernel,
        out_shape=jax.ShapeDtypeStruct((bsz, S_ALL, NA_HEADS * NA_DH), BF16),
        grid=(bsz, pairs),
        in_specs=[seq_spec, seq_spec, seq_spec, seq_spec,
                  pl.BlockSpec((2, 2 * NA_KH - 2, GRID_W, 128), lambda b, p: (p, 0, 0, 0))],
        out_specs=seq_spec,
        scratch_shapes=[pltpu.VMEM((2, 2 * GRID_W, NA_KH * GRID_W + CTX_LEN), F32)],
        compiler_params=_params("parallel", "arbitrary"),
        name="na_attn",
    )(q, k, v, gate, table)


def _swa_mask_table():
    span = SW_BLOCK + 2 * SW_WINDOW
    r = np.arange(SW_BLOCK)[:, None]
    c = np.arange(span)[None, :]
    cases = [np.abs(r + off - c) <= SW_WINDOW for off in (0, SW_WINDOW, 2 * SW_WINDOW)]
    cases.append(np.zeros((SW_BLOCK, span), bool))
    return jnp.asarray(np.where(np.stack(cases), 0.0, NEG), F32)


def _swa_kernel(sink_ref, q_ref, k_ref, v_ref, gate_ref, mask_ref, out_ref, s_scr):
    n_blocks = S_ALL // SW_BLOCK
    ctx_blocks = CTX_LEN // SW_BLOCK
    n_tiles = SW_HEADS // 2
    span = SW_BLOCK + 2 * SW_WINDOW
    kc = k_ref[0, 0:CTX_LEN, :]
    vc = v_ref[0, 0:CTX_LEN, :]
    first = lax.broadcasted_iota(jnp.int32, (2 * SW_BLOCK, 1), 0) < SW_BLOCK

    def rows_of(n):
        qstart = (n - ctx_blocks) * SW_BLOCK
        kstart = jnp.clip(qstart - SW_WINDOW, 0, SEQ - span)
        q_rows = pl.ds(pl.multiple_of(n * SW_BLOCK, SW_BLOCK), SW_BLOCK)
        k_rows = pl.ds(pl.multiple_of(CTX_LEN + kstart, SW_BLOCK), span)
        return q_rows, k_rows

    def scores(n, t, slot):
        q_rows, k_rows = rows_of(n)
        case = jnp.where(n < ctx_blocks, 3,
                         jnp.where(n == ctx_blocks, 0, jnp.where(n == n_blocks - 1, 2, 1)))
        qs = _stack_heads(q_ref[0, q_rows, t * 128:(t + 1) * 128])
        mb = mask_ref[case]
        s_scr[slot, :, 0:span] = _dot_nt(qs, k_ref[0, k_rows, :]) + jnp.concatenate([mb, mb], axis=0)
        s_scr[slot, :, span:span + CTX_LEN] = _dot_nt(qs, kc)

    def finish(n, t, slot):
        q_rows, k_rows = rows_of(n)
        cols = slice(t * 128, (t + 1) * 128)
        sink = jnp.where(first, sink_ref[t], sink_ref[t + n_tiles]) * LOG2E
        o = _merge_heads(_softmax_pv([s_scr[slot, :, 0:span], s_scr[slot, :, span:span + CTX_LEN]],
                                     [v_ref[0, k_rows, :], vc], extra_logit=sink))
        out_ref[0, q_rows, cols] = (o * _silu(gate_ref[0, q_rows, cols].astype(F32))).astype(BF16)

    scores(0, 0, 0)

    def body(n, carry):
        for t in range(n_tiles):
            if t + 1 < n_tiles:
                scores(n, t + 1, (t + 1) % 2)
            else:
                scores(jnp.minimum(n + 1, n_blocks - 1), 0, 0)
            finish(n, t, t % 2)
        return carry

    lax.fori_loop(0, n_blocks, body, 0)


def _swa(q, k, v, gate, sink):
    bsz = q.shape[0]
    width = SW_HEADS * SW_DH
    span = SW_BLOCK + 2 * SW_WINDOW
    seq_spec = pl.BlockSpec((1, S_ALL, width), lambda b: (b, 0, 0))
    kv_spec = pl.BlockSpec((1, S_ALL, 128), lambda b: (b, 0, 0))
    return pl.pallas_call(
        _swa_kernel,
        out_shape=jax.ShapeDtypeStruct((bsz, S_ALL, width), BF16),
        grid=(bsz,),
        in_specs=[pl.BlockSpec(memory_space=pltpu.SMEM), seq_spec, kv_spec, kv_spec, seq_spec,
                  pl.BlockSpec((4, SW_BLOCK, span), lambda b: (0, 0, 0))],
        out_specs=seq_spec,
        scratch_shapes=[pltpu.VMEM((2, 2 * SW_BLOCK, span + CTX_LEN), F32)],
        compiler_params=_params("parallel"),
        name="swa_attn",
    )(sink, q, k, v, gate, _swa_mask_table())


def _pad_cols(w, width):
    return jnp.pad(w, ((0, 0), (0, width - w.shape[1])))


def _prep_even(w_in, gate_w, gate_b):
    o = np.cumsum((0, 256, 256, 512, 512, 32, 512, 512, 512, 512))
    gq, gk, gv, gg, lra, nq, nk, nv, ng = [w_in[:, o[i]:o[i + 1]] for i in range(9)]
    w = jnp.concatenate([gq * GLA_DK ** -0.5, gk, gv, gg, nq * (NA_DH ** -0.5 * LOG2E), nk, nv, ng,
                         _pad_cols(lra, 128)], axis=1).astype(BF16)
    gws, gbs = [], []
    for p in range(GLA_HEADS // 2):
        cols = slice(p * 128, (p + 1) * 128)
        zero = jnp.zeros((GLA_RANK, 128), F32)
        gws.append(jnp.concatenate([jnp.concatenate([gate_w[0][:, cols], zero], axis=1),
                                    jnp.concatenate([zero, gate_w[1][:, cols]], axis=1)], axis=0))
        gbs.append(jnp.concatenate([gate_b[0][cols], gate_b[1][cols]])[None])
    return w, jnp.stack(gws).astype(BF16), jnp.stack(gbs)


def _swa_head_perm():
    order = []
    for t in range(SW_HEADS // 2):
        order += list(range(t * SW_DH, (t + 1) * SW_DH))
        order += list(range((t + SW_HEADS // 2) * SW_DH, (t + SW_HEADS // 2 + 1) * SW_DH))
    return np.asarray(order)


def _rope_swap(w):
    nf = SW_DH // 4
    idx = np.arange(w.shape[1]).reshape(-1, 2, nf)[:, ::-1, :].reshape(-1)
    return w[:, idx]


def _prep_odd(w_in, w_out):
    o = np.cumsum((0, 256, 256, 512, 512, 16, 512, 128, 128, 512))
    mq, mk, mv, mg, gates, sq, sk, sv, sg = [w_in[:, o[i]:o[i + 1]] for i in range(9)]
    perm = _swa_head_perm()
    sq = sq[:, perm] * (SW_DH ** -0.5 * LOG2E)
    w = jnp.concatenate([mq, mk * ML_DQK ** -0.5, mv, mg, sq, _rope_swap(sq), sk, _rope_swap(sk),
                         sv, _pad_cols(gates, 128), sg[:, perm]], axis=1).astype(BF16)
    half = ML_HEADS * ML_DV
    w_o = jnp.concatenate([w_out[:half], w_out[half:][perm]], axis=0).astype(BF16)
    return w, w_o


def _rope_tables():
    nf = SW_DH // 4
    freqs = ROPE_BASE ** (-jnp.arange(nf, dtype=F32) / nf)
    pos = jnp.arange(SEQ)
    rows = (pos // GRID_W).astype(F32)
    cols = (pos % GRID_W).astype(F32)
    ar = rows[:, None] * freqs[None, :]
    ac = cols[:, None] * freqs[None, :]
    cos = jnp.concatenate([jnp.cos(ar), jnp.cos(ar), jnp.cos(ac), jnp.cos(ac)], axis=1)
    sin = jnp.concatenate([-jnp.sin(ar), jnp.sin(ar), -jnp.sin(ac), jnp.sin(ac)], axis=1)
    cos = jnp.concatenate([jnp.ones((CTX_LEN, SW_DH), F32), cos], axis=0)
    sin = jnp.concatenate([jnp.zeros((CTX_LEN, SW_DH), F32), sin], axis=0)
    return jnp.tile(cos, (1, 2)), jnp.tile(sin, (1, 2))


def _gate_layouts(gates):
    bsz = gates.shape[0]
    g = gates.reshape(bsz, S_ALL, 2, 2, 2, 2)
    gcol = g.transpose(0, 4, 1, 2, 3, 5).reshape(bsz, 2, S_ALL, 8)
    g = gates.reshape(bsz, N_CHUNKS, CHUNK, 2, 2, 2, 2)
    grow = g.transpose(0, 3, 4, 5, 1, 6, 2).reshape(bsz, 2, 2, 2, N_CHUNKS, 2 * CHUNK)
    return gcol, grow


def kernel(x, c, ctx, c_ctx, w_ada, b_ada, ln_g, ln_b, w_in_even, w_out_even, gla_gate_w, gla_gate_b,
           gla_norm_g, na_rpb, w_in_odd, w_out_odd, ml_gate_b, ml_norm_g, sw_sink):
    bsz = x.shape[0]
    mod_rows = 16
    cvec = jnp.zeros((mod_rows, D_MODEL), F32).at[:bsz].set(c).at[bsz].set(c_ctx)
    ada = _ada(cvec, w_ada, b_ada)
    xs = (ctx, x)
    cos_t, sin_t = _rope_tables()
    for l in range(DEPTH):
        i = l // 2
        last = l == DEPTH - 1
        lat = ada[l, :bsz].reshape(bsz, 1, 3, D_MODEL)
        cx = jnp.broadcast_to(ada[l, bsz].reshape(1, 1, 3, D_MODEL), (bsz, 1, 3, D_MODEL))
        mod = jnp.concatenate([lat, cx], axis=1)
        if l % 2 == 0:
            w, gw, gb = _prep_even(w_in_even[i], gla_gate_w[i], gla_gate_b[i])
            gq, gk, gv, gg, nq, nk, nv, ng, lr = _inproj_even(xs, mod, w)
            ya = _gla(gq, gk, gv, gg, lr, gw, gb, gla_norm_g[i].reshape(GLA_HEADS // 2, 1, 256))
            yb = _na(nq, nk, nv, ng, _rpb_table(na_rpb[i]))
            w_o = w_out_even[i].astype(BF16)
        else:
            w, w_o = _prep_odd(w_in_odd[i], w_out_odd[i])
            mq, mk, mv, mg, sq, sk, sv, sg, gates = _inproj_odd(
                xs[0], mod, w, cos_t, sin_t, ml_gate_b[i].reshape(1, 4 * ML_HEADS))
            gcol, grow = _gate_layouts(gates)
            ya = _mlstm(mq, mk, mv, mg, gcol, grow, ml_norm_g[i].reshape(ML_HEADS // 2, 1, 256))
            yb = _swa(sq, sk, sv, sg, sw_sink[i])
        xs = (_outproj(ya, yb, xs, mod, w_o, ln_g[l].reshape(1, D_MODEL), ln_b[l].reshape(1, D_MODEL), last),)
    return xs[0]
```

```python
import functools

import numpy as np
import jax
import jax.numpy as jnp
from jax import lax
from jax.experimental import pallas as pl
from jax.experimental.pallas import tpu as pltpu

F32 = jnp.float32
BF16 = jnp.bfloat16

D_MODEL = 1024
SEQ = 4096
DEPTH = 4
GRID_W = 64
CTX_LEN = 256
S_ALL = CTX_LEN + SEQ
ALPHA = (2.0 * DEPTH) ** 0.25
LN_EPS = 1e-5
NORM_EPS = 1e-6
NEG = -1e30
LOG2E = 1.4426950408889634
GLA_HEADS, GLA_DK, GLA_DV, GLA_RANK, GLA_TAU = 4, 64, 128, 16, 16.0
NA_HEADS, NA_DH, NA_KH, NA_KW = 8, 64, 8, 16
ML_HEADS, ML_DQK, ML_DV = 4, 64, 128
SW_HEADS, SW_KV_HEADS, SW_DH, SW_WINDOW, SW_BLOCK = 8, 2, 64, 128, 128
ROPE_BASE = 10000.0
CHUNK = 64
N_CHUNKS = S_ALL // CHUNK
CTX_CHUNKS = CTX_LEN // CHUNK
ROW_TILE = 256
N_ROW_TILES = S_ALL // ROW_TILE
GRID_ROWS = SEQ // GRID_W
SCAN_SLOTS = 4
SCAN_LOOKAHEAD = 2
NA_UNROLL = 4
NA_LOOKAHEAD = 2
SW_LOOKAHEAD = 2

VMEM_LIMIT = 56 * 1024 * 1024

NT_DIMS = (((1,), (1,)), ((), ()))
TN_DIMS = (((0,), (0,)), ((), ()))


def _dot(a, b):
    return jnp.dot(a, b, preferred_element_type=F32)


def _dot_nt(a, b):
    return lax.dot_general(a, b, NT_DIMS, preferred_element_type=F32)


def _dot_tn(a, b):
    return lax.dot_general(a, b, TN_DIMS, preferred_element_type=F32)


def _silu(x):
    h = 0.5 * x
    return h + h * jnp.tanh(h)


def _log_sigmoid(x):
    return jnp.minimum(x, 0.0) - jnp.log(1.0 + jnp.exp(-jnp.abs(x)))


def _cumsum_rows(x, reverse):
    n = x.shape[0]
    row = lax.broadcasted_iota(jnp.int32, x.shape, 0)
    k = 1
    while k < n:
        if reverse:
            x = x + jnp.where(row < n - k, pltpu.roll(x, n - k, axis=0), 0.0)
        else:
            x = x + jnp.where(row >= k, pltpu.roll(x, k, axis=0), 0.0)
        k *= 2
    return x


def _split_dot(mat_bf16, x):
    hi = x.astype(BF16)
    lo = (x - hi.astype(F32)).astype(BF16)
    return _dot(mat_bf16, hi) + _dot(mat_bf16, lo)


def _split_dot_r(x, mat_bf16):
    hi = x.astype(BF16)
    lo = (x - hi.astype(F32)).astype(BF16)
    return _dot(hi, mat_bf16) + _dot(lo, mat_bf16)


def _halves_to_tiles(x):
    half0 = lax.broadcasted_iota(jnp.int32, x.shape, 1) < 64
    sw = pltpu.roll(x, 64, axis=1)
    return jnp.concatenate([jnp.where(half0, x, sw), jnp.where(half0, sw, x)], axis=1)


def _params(*sem):
    return pltpu.CompilerParams(dimension_semantics=sem, vmem_limit_bytes=VMEM_LIMIT)


def _ada_kernel(c_ref, w_ref, b_ref, o_ref):
    cond = _silu(c_ref[...])
    o_ref[0] = jnp.dot(cond, w_ref[0], precision=lax.Precision.HIGHEST,
                       preferred_element_type=F32) + b_ref[0]


def _ada(cvec, w_ada, b_ada):
    rows = cvec.shape[0]
    nblk = 3
    return pl.pallas_call(
        _ada_kernel,
        out_shape=jax.ShapeDtypeStruct((DEPTH, rows, 3 * D_MODEL), F32),
        grid=(DEPTH, nblk),
        in_specs=[
            pl.BlockSpec((rows, D_MODEL), lambda l, n: (0, 0)),
            pl.BlockSpec((1, D_MODEL, D_MODEL), lambda l, n: (l, 0, n)),
            pl.BlockSpec((1, 1, D_MODEL), lambda l, n: (l, 0, n)),
        ],
        out_specs=pl.BlockSpec((1, rows, D_MODEL), lambda l, n: (l, 0, n)),
        compiler_params=_params("arbitrary", "arbitrary"),
        name="ada_mod",
    )(cvec, w_ada, b_ada.reshape(DEPTH, 1, 3 * D_MODEL))


def _row_tile(x_refs, is_ctx):
    if len(x_refs) == 1:
        return x_refs[0][0]
    return jnp.where(is_ctx, x_refs[0][0], x_refs[1][0])


def _x_specs(xs, tile_of):
    def combined(*ids):
        b, j = tile_of(*ids)
        return (b, j, 0)

    def context(*ids):
        return (tile_of(*ids)[0], 0, 0)

    def latent(*ids):
        b, j = tile_of(*ids)
        return (b, jnp.maximum(j - 1, 0), 0)

    if len(xs) == 1:
        return [pl.BlockSpec((1, ROW_TILE, D_MODEL), combined)]
    return [pl.BlockSpec((1, ROW_TILE, D_MODEL), context), pl.BlockSpec((1, ROW_TILE, D_MODEL), latent)]


def _modulated(x_refs, mod_ref):
    shift = mod_ref[0, 0, 0:1, :]
    scale = mod_ref[0, 0, 1:2, :]
    x = _row_tile(x_refs, pl.program_id(1) == 0)
    return (x * (1.0 + scale) + shift).astype(BF16)


EVEN_SLABS = (256, 256, 512, 512, 512, 512, 512, 512)


def _inproj_even_kernel(*refs, n_x):
    x_refs, (mod_ref, w_ref), outs = refs[:n_x], refs[n_x:n_x + 2], refs[n_x + 2:]
    h = _modulated(x_refs, mod_ref)
    off = 0
    for ref, width in zip(outs[:-1], EVEN_SLABS):
        ref[0] = _dot(h, w_ref[:, off:off + width]).astype(ref.dtype)
        off += width
    lr = _dot(h, w_ref[:, off:off + 128])
    outs[-1][0] = lr[:, :2 * GLA_RANK]


ODD_SLABS = (256, 256, 512, 512)


def _inproj_odd_kernel(x_ref, mod_ref, w_ref, cos_ref, sin_ref, gb_ref,
                       mq, mk, mv, mg, sq, sk, sv, sg, gates):
    h = _modulated((x_ref,), mod_ref)
    off = 0
    for ref, width in zip((mq, mk, mv, mg), ODD_SLABS):
        ref[0] = _dot(h, w_ref[:, off:off + width]).astype(ref.dtype)
        off += width
    cos = cos_ref[...]
    sin = sin_ref[...]
    a = _dot(h, w_ref[:, off:off + 512])
    b = _dot(h, w_ref[:, off + 512:off + 1024])
    for t in range(4):
        cols = slice(t * 128, (t + 1) * 128)
        sq[0, :, cols] = (a[:, cols] * cos + b[:, cols] * sin).astype(BF16)
    off += 1024
    kvg = _dot(h, w_ref[:, off:off + 512])
    sk[0] = (kvg[:, 0:128] * cos + kvg[:, 128:256] * sin).astype(BF16)
    sv[0] = kvg[:, 256:384].astype(BF16)
    gates[0] = kvg[:, 384:384 + 4 * ML_HEADS] + gb_ref[...]
    off += 512
    sg[0] = _dot(h, w_ref[:, off:off + 512]).astype(BF16)


def _mod_index(b, j):
    return (b, jnp.where(j == 0, 1, 0), 0, 0)


def _inproj_even(xs, mod, w):
    bsz = xs[0].shape[0]
    ntot = w.shape[1]
    out_shape = [jax.ShapeDtypeStruct((bsz, S_ALL, wd), BF16) for wd in EVEN_SLABS]
    out_shape.append(jax.ShapeDtypeStruct((bsz, S_ALL, 2 * GLA_RANK), F32))
    out_specs = [pl.BlockSpec((1, ROW_TILE, wd), lambda b, j: (b, j, 0)) for wd in EVEN_SLABS]
    out_specs.append(pl.BlockSpec((1, ROW_TILE, 2 * GLA_RANK), lambda b, j: (b, j, 0)))
    return pl.pallas_call(
        functools.partial(_inproj_even_kernel, n_x=len(xs)),
        out_shape=out_shape,
        grid=(bsz, N_ROW_TILES),
        in_specs=_x_specs(xs, lambda b, j: (b, j)) + [
            pl.BlockSpec((1, 1, 3, D_MODEL), _mod_index),
            pl.BlockSpec((D_MODEL, ntot), lambda b, j: (0, 0)),
        ],
        out_specs=out_specs,
        compiler_params=_params("parallel", "arbitrary"),
        name="inproj_even",
    )(*xs, mod, w)


def _inproj_odd(xc, mod, w, cos_t, sin_t, gate_b):
    bsz = xc.shape[0]
    ntot = w.shape[1]
    widths = (256, 256, 512, 512, 512, 128, 128, 512)
    out_shape = [jax.ShapeDtypeStruct((bsz, S_ALL, wd), BF16) for wd in widths]
    out_shape.append(jax.ShapeDtypeStruct((bsz, S_ALL, 4 * ML_HEADS), F32))
    out_specs = [pl.BlockSpec((1, ROW_TILE, wd), lambda b, j: (b, j, 0)) for wd in widths]
    out_specs.append(pl.BlockSpec((1, ROW_TILE, 4 * ML_HEADS), lambda b, j: (b, j, 0)))
    return pl.pallas_call(
        _inproj_odd_kernel,
        out_shape=out_shape,
        grid=(bsz, N_ROW_TILES),
        in_specs=[
            pl.BlockSpec((1, ROW_TILE, D_MODEL), lambda b, j: (b, j, 0)),
            pl.BlockSpec((1, 1, 3, D_MODEL), _mod_index),
            pl.BlockSpec((D_MODEL, ntot), lambda b, j: (0, 0)),
            pl.BlockSpec((ROW_TILE, 128), lambda b, j: (j, 0)),
            pl.BlockSpec((ROW_TILE, 128), lambda b, j: (j, 0)),
            pl.BlockSpec((1, 4 * ML_HEADS), lambda b, j: (0, 0)),
        ],
        out_specs=out_specs,
        compiler_params=_params("parallel", "arbitrary"),
        name="inproj_odd",
    )(xc, mod, w, cos_t, sin_t, gate_b)


def _outproj_kernel(*refs, n_x, n_tiles):
    ya_ref, yb_ref = refs[:2]
    x_refs = refs[2:2 + n_x]
    mod_ref, w_ref, g_ref, b_ref, o_ref, r_scr = refs[2 + n_x:]
    t = pl.program_id(0)

    @pl.when(t == 0)
    def _():
        r_scr[...] = jnp.zeros(r_scr.shape, F32)

    r = r_scr[...]
    mu = jnp.mean(r, axis=-1, keepdims=True)
    d = r - mu
    var = jnp.mean(d * d, axis=-1, keepdims=True)
    o_ref[0] = d * lax.rsqrt(var + LN_EPS) * g_ref[...] + b_ref[...]

    half = ya_ref.shape[2]
    y = _dot(ya_ref[0], w_ref[0:half, :]) + _dot(yb_ref[0], w_ref[half:2 * half, :])
    x = _row_tile(x_refs, lax.rem(t, n_tiles) == 0)
    r_scr[...] = ALPHA * x + mod_ref[0, 0, 2:3, :] * y


def _outproj(ya, yb, xs, mod, w, ln_g, ln_b, latent_only):
    bsz = xs[0].shape[0]
    skip = 1 if latent_only else 0
    n_tiles = N_ROW_TILES - skip
    total = bsz * n_tiles
    half = ya.shape[2]

    def tile_of(t):
        tt = jnp.minimum(t, total - 1)
        return tt // n_tiles, lax.rem(tt, n_tiles) + skip

    def y_index(t):
        b, j = tile_of(t)
        return (b, j, 0)

    def mod_index(t):
        return _mod_index(*tile_of(t))

    def out_index(t):
        tt = jnp.maximum(t - 1, 0)
        return (tt // n_tiles, lax.rem(tt, n_tiles), 0)

    const = lambda t: (0, 0)
    return pl.pallas_call(
        functools.partial(_outproj_kernel, n_x=len(xs), n_tiles=n_tiles),
        out_shape=jax.ShapeDtypeStruct((bsz, n_tiles * ROW_TILE, D_MODEL), F32),
        grid=(total + 1,),
        in_specs=[pl.BlockSpec((1, ROW_TILE, half), y_index), pl.BlockSpec((1, ROW_TILE, half), y_index)]
        + _x_specs(xs, tile_of) + [
            pl.BlockSpec((1, 1, 3, D_MODEL), mod_index),
            pl.BlockSpec((2 * half, D_MODEL), const),
            pl.BlockSpec((1, D_MODEL), const),
            pl.BlockSpec((1, D_MODEL), const),
        ],
        out_specs=pl.BlockSpec((1, ROW_TILE, D_MODEL), out_index),
        scratch_shapes=[pltpu.VMEM((ROW_TILE, D_MODEL), F32)],
        compiler_params=_params("arbitrary"),
        name="outproj_ln",
    )(ya, yb, *xs, mod, w, ln_g, ln_b)


def _scan_epilogue(of_scr, ob_scr, gate_ref, ng_ref, out_ref):
    def body(t, carry):
        rows = pl.ds(pl.multiple_of(t * ROW_TILE, ROW_TILE), ROW_TILE)
        o = of_scr[rows, :] + ob_scr[rows, :]
        parts = []
        for hh in range(2):
            oh = o[:, hh * 128:(hh + 1) * 128]
            ms = jnp.mean(oh * oh, axis=-1, keepdims=True)
            parts.append(oh * lax.rsqrt(ms + NORM_EPS))
        y = jnp.concatenate(parts, axis=1) * ng_ref[0]
        out_ref[0, rows, :] = (y * _silu(gate_ref[0, rows, :].astype(F32))).astype(BF16)
        return carry

    lax.fori_loop(0, N_ROW_TILES, body, 0)


def _bwd_chunk(i):
    return jnp.where(i < CTX_CHUNKS, CTX_CHUNKS - 1 - i, N_CHUNKS - 1 + CTX_CHUNKS - i)


def _scan_pipeline(local_fn, state_fn):
    for i in range(SCAN_LOOKAHEAD):
        local_fn(i, 0, i)
        local_fn(_bwd_chunk(i), 1, i)

    def body(j, carry):
        for u in range(SCAN_SLOTS):
            i = SCAN_SLOTS * j + u
            nxt = jnp.minimum(i + SCAN_LOOKAHEAD, N_CHUNKS - 1)
            local_fn(nxt, 0, (u + SCAN_LOOKAHEAD) % SCAN_SLOTS)
            local_fn(_bwd_chunk(nxt), 1, (u + SCAN_LOOKAHEAD) % SCAN_SLOTS)
            state_fn(i, 0, u)
            state_fn(_bwd_chunk(i), 1, u)
        return carry

    lax.fori_loop(0, N_CHUNKS // SCAN_SLOTS, body, 0)


def _gla_kernel(q_ref, k_ref, v_ref, gate_ref, lr_ref, gw_ref, gb_ref, ng_ref, out_ref,
                g_scr, of_scr, ob_scr, st_scr, a_scr, qin_scr, u_scr, dl_scr):
    def pre(t, carry):
        rows = pl.ds(pl.multiple_of(t * ROW_TILE, ROW_TILE), ROW_TILE)
        z = _dot(lr_ref[0, rows, :].astype(BF16), gw_ref[0]) + gb_ref[0]
        g_scr[rows, :] = _log_sigmoid(z) * (1.0 / GLA_TAU)
        return carry

    lax.fori_loop(0, N_ROW_TILES, pre, 0)
    st_scr[...] = jnp.zeros(st_scr.shape, F32)

    def local(cc, d, slot):
        rows = pl.ds(pl.multiple_of(cc * CHUNK, CHUNK), CHUNK)
        g = g_scr[rows, d * 128:(d + 1) * 128]
        b = _cumsum_rows(g, reverse=(d == 1))
        b_last = b[0:1, :] if d == 1 else b[CHUNK - 1:CHUNK, :]
        q = q_ref[0, rows, :].astype(F32)
        k = k_ref[0, rows, :].astype(F32)
        q_in = q * jnp.exp(b)
        k_in = (k * jnp.exp(-b)).astype(BF16)
        k_dec = (k * jnp.exp(b_last - b)).astype(BF16)

        head0 = lax.broadcasted_iota(jnp.int32, (CHUNK, 128), 1) < GLA_DK
        q_stack = jnp.concatenate([jnp.where(head0, q_in, 0.0), jnp.where(head0, 0.0, q_in)],
                                  axis=0).astype(BF16)
        a = _dot_nt(q_stack, k_in)
        t_i = lax.broadcasted_iota(jnp.int32, (2 * CHUNK, CHUNK), 0) & (CHUNK - 1)
        s_i = lax.broadcasted_iota(jnp.int32, (2 * CHUNK, CHUNK), 1)
        keep = (s_i >= t_i) if d == 1 else (s_i <= t_i)
        a_scr[slot, d] = jnp.where(keep, a, 0.0).astype(BF16)
        qin_scr[slot, d] = q_in.astype(BF16)

        u = _dot_tn(v_ref[0, rows, :], k_dec)
        r_i = lax.broadcasted_iota(jnp.int32, u.shape, 0) < GLA_DV
        c_i = lax.broadcasted_iota(jnp.int32, u.shape, 1) < GLA_DK
        u_scr[slot, d] = jnp.where(r_i == c_i, u, 0.0)
        dl_scr[slot, d] = jnp.exp(b_last)

    def state(cc, d, slot):
        rows = pl.ds(pl.multiple_of(cc * CHUNK, CHUNK), CHUNK)
        v = v_ref[0, rows, :]
        a = a_scr[slot, d]
        vhead0 = lax.broadcasted_iota(jnp.int32, (CHUNK, 2 * GLA_DV), 1) < GLA_DV
        zero = jnp.zeros_like(v)
        st = st_scr[d]
        o = (_dot(a[:CHUNK], jnp.where(vhead0, v, zero))
             + _dot(a[CHUNK:], jnp.where(vhead0, zero, v))
             + _dot_nt(qin_scr[slot, d], st.astype(BF16)))
        if d == 0:
            of_scr[rows, :] = o
        else:
            ob_scr[rows, :] = o
        st_scr[d] = st * dl_scr[slot, d] + u_scr[slot, d]

    _scan_pipeline(local, state)
    _scan_epilogue(of_scr, ob_scr, gate_ref, ng_ref, out_ref)


def _gla(q, k, v, gate, lr, gw, gb, ng):
    bsz = q.shape[0]
    pairs = GLA_HEADS // 2
    return pl.pallas_call(
        _gla_kernel,
        out_shape=jax.ShapeDtypeStruct((bsz, S_ALL, GLA_HEADS * GLA_DV), BF16),
        grid=(bsz, pairs),
        in_specs=[
            pl.BlockSpec((1, S_ALL, 128), lambda b, p: (b, 0, p)),
            pl.BlockSpec((1, S_ALL, 128), lambda b, p: (b, 0, p)),
            pl.BlockSpec((1, S_ALL, 256), lambda b, p: (b, 0, p)),
            pl.BlockSpec((1, S_ALL, 256), lambda b, p: (b, 0, p)),
            pl.BlockSpec((1, S_ALL, 2 * GLA_RANK), lambda b, p: (b, 0, 0)),
            pl.BlockSpec((1, 2 * GLA_RANK, 256), lambda b, p: (p, 0, 0)),
            pl.BlockSpec((1, 1, 256), lambda b, p: (p, 0, 0)),
            pl.BlockSpec((1, 1, 256), lambda b, p: (p, 0, 0)),
        ],
        out_specs=pl.BlockSpec((1, S_ALL, 256), lambda b, p: (b, 0, p)),
        scratch_shapes=[
            pltpu.VMEM((S_ALL, 256), F32),
            pltpu.VMEM((S_ALL, 256), F32),
            pltpu.VMEM((S_ALL, 256), F32),
            pltpu.VMEM((2, 2 * GLA_DV, 2 * GLA_DK), F32),
            pltpu.VMEM((SCAN_SLOTS, 2, 2 * CHUNK, CHUNK), BF16),
            pltpu.VMEM((SCAN_SLOTS, 2, CHUNK, 2 * GLA_DK), BF16),
            pltpu.VMEM((SCAN_SLOTS, 2, 2 * GLA_DV, 2 * GLA_DK), F32),
            pltpu.VMEM((SCAN_SLOTS, 2, 1, 2 * GLA_DK), F32),
        ],
        compiler_params=_params("parallel", "arbitrary"),
        name="gla_scan",
    )(q, k, v, gate, lr, gw, gb, ng)


def _cummax_rows_chunked(x, reverse):
    n = x.shape[0]
    pos = lax.broadcasted_iota(jnp.int32, x.shape, 0) & (CHUNK - 1)
    k = 1
    while k < CHUNK:
        if reverse:
            x = jnp.maximum(x, jnp.where(pos < CHUNK - k, pltpu.roll(x, n - k, axis=0), NEG))
        else:
            x = jnp.maximum(x, jnp.where(pos >= k, pltpu.roll(x, k, axis=0), NEG))
        k *= 2
    return x


def _mlstm_kernel(q_ref, k_ref, v_ref, gate_ref, gcol_ref, grow_ref, ng_ref, out_ref,
                  bcm_scr, rcm_scr, cmx_scr, rrow_scr, blast_scr, wsmax_scr, blast2_scr, wsmax2_scr,
                  of_scr, ob_scr, st_scr, m2_scr, nd_scr, u_scr):
    lane128 = lax.broadcasted_iota(jnp.int32, (1, 128), 1)
    half0_row = lane128 < CHUNK

    li = lax.broadcasted_iota(jnp.int32, (128, 128), 0)
    lj = lax.broadcasted_iota(jnp.int32, (128, 128), 1)
    same_half = (li < CHUNK) == (lj < CHUNK)
    ones_half = jnp.where(same_half, 1.0, 0.0).astype(BF16)
    for d in range(2):
        within = (li >= lj) if d == 1 else (li <= lj)
        tri = jnp.where(same_half & within, 1.0, 0.0).astype(BF16)
        lf = _log_sigmoid(grow_ref[0, d, 1, 0])
        b_row = _split_dot_r(lf, tri)
        total = _split_dot_r(lf, ones_half)
        r_row = grow_ref[0, d, 0, 0] - b_row
        rrow_scr[d] = r_row
        blast_scr[d] = total
        mx0 = jnp.max(jnp.where(half0_row, r_row, NEG), axis=-1, keepdims=True)
        mx1 = jnp.max(jnp.where(half0_row, NEG, r_row), axis=-1, keepdims=True)
        wsmax = total + jnp.where(half0_row, mx0, mx1)
        wsmax_scr[d] = wsmax
        blast2_scr[d] = _halves_to_tiles(total)
        wsmax2_scr[d] = _halves_to_tiles(wsmax)

    ti = lax.broadcasted_iota(jnp.int32, (ROW_TILE, ROW_TILE), 0)
    tj = lax.broadcasted_iota(jnp.int32, (ROW_TILE, ROW_TILE), 1)
    same_chunk = (ti >> 6) == (tj >> 6)
    tril = jnp.where(same_chunk & (tj <= ti), 1.0, 0.0).astype(BF16)
    triu = jnp.where(same_chunk & (tj >= ti), 1.0, 0.0).astype(BF16)
    half0_tile = lax.broadcasted_iota(jnp.int32, (ROW_TILE, 128), 1) < CHUNK

    def pre(t, carry):
        rows = pl.ds(pl.multiple_of(t * ROW_TILE, ROW_TILE), ROW_TILE)
        gc = gcol_ref[0, 0, rows, :]
        lf = _log_sigmoid(gc)
        for d in range(2):
            bcol = _split_dot(triu if d == 1 else tril, lf)
            b0 = jnp.broadcast_to(bcol[:, d * 4 + 2:d * 4 + 3], (ROW_TILE, 128))
            b1 = jnp.broadcast_to(bcol[:, d * 4 + 3:d * 4 + 4], (ROW_TILE, 128))
            i0 = jnp.broadcast_to(gc[:, d * 4:d * 4 + 1], (ROW_TILE, 128))
            i1 = jnp.broadcast_to(gc[:, d * 4 + 1:d * 4 + 2], (ROW_TILE, 128))
            bc = jnp.where(half0_tile, b0, b1)
            rc = jnp.where(half0_tile, i0, i1) - bc
            bcm_scr[d, rows, :] = bc
            rcm_scr[d, rows, :] = rc
            cmx_scr[d, rows, :] = _cummax_rows_chunked(rc, reverse=(d == 1))
        return carry

    lax.fori_loop(0, N_ROW_TILES, pre, 0)
    st_scr[...] = jnp.zeros(st_scr.shape, F32)
    m2_scr[...] = jnp.zeros(m2_scr.shape, F32)

    def local(cc, d, slot):
        rows = pl.ds(pl.multiple_of(cc * CHUNK, CHUNK), CHUNK)
        rc = rcm_scr[d, rows, :]
        cmx = cmx_scr[d, rows, :]
        rr = rrow_scr[d, pl.ds(cc, 1), :]
        bl = blast_scr[d, pl.ds(cc, 1), :]
        wm = wsmax_scr[d, pl.ds(cc, 1), :]
        q = q_ref[0, rows, :]
        k = k_ref[0, rows, :]
        v = v_ref[0, rows, :]

        half0 = lax.broadcasted_iota(jnp.int32, (CHUNK, 128), 1) < CHUNK
        t_i = lax.broadcasted_iota(jnp.int32, (CHUNK, 128), 0)
        s_i = lax.broadcasted_iota(jnp.int32, (CHUNK, 128), 1) & (CHUNK - 1)
        keep = (s_i >= t_i) if d == 1 else (s_i <= t_i)
        w = jnp.exp(jnp.where(keep, rr - cmx, NEG))

        zero_k = jnp.zeros_like(k)
        k_stack = jnp.concatenate([jnp.where(half0, k, zero_k), jnp.where(half0, zero_k, k)], axis=0)
        s_w = (_dot_nt(q, k_stack) * w).astype(BF16)

        vhead0 = lax.broadcasted_iota(jnp.int32, (CHUNK, 2 * ML_DV), 1) < ML_DV
        zero_v = jnp.zeros_like(v)
        one_v = jnp.ones_like(v)
        ones0 = jnp.where(vhead0, 1.0, 0.0).astype(BF16)
        ones1 = jnp.where(vhead0, 0.0, 1.0).astype(BF16)
        vo_stack = jnp.concatenate(
            [jnp.concatenate([jnp.where(vhead0, v, zero_v), ones0], axis=1),
             jnp.concatenate([jnp.where(vhead0, zero_v, v), ones1], axis=1)],
            axis=0)
        nd_scr[slot, d] = _dot(s_w, vo_stack)

        kws = (k.astype(F32) * jnp.exp(rc + (bl - wm))).astype(BF16)
        u = _dot_tn(kws, jnp.concatenate([v, one_v], axis=1))
        for blk in range(4):
            r0 = (blk // 2) * ML_DQK
            c0 = (blk % 2) * 2 * ML_DV + (blk // 2) * ML_DV
            u_scr[slot, d, blk] = u[r0:r0 + ML_DQK, c0:c0 + ML_DV]

    def state(cc, d, slot):
        rows = pl.ds(pl.multiple_of(cc * CHUNK, CHUNK), CHUNK)
        cmx2 = _halves_to_tiles(cmx_scr[d, rows, :])
        bc2 = _halves_to_tiles(bcm_scr[d, rows, :])
        m_old = m2_scr[d]
        g = jnp.maximum(cmx2, m_old)
        w_loc = jnp.exp(cmx2 - g)
        w_int = jnp.exp(m_old - g)
        floor = jnp.exp(-(bc2 + g))

        blocks = [st_scr[d, blk] for blk in range(4)]
        zero = jnp.zeros((ML_DQK, ML_DV), BF16)
        cm0, n0, cm1, n1 = [x.astype(BF16) for x in blocks]
        st = jnp.concatenate([jnp.concatenate([cm0, zero, n0, zero], axis=1),
                              jnp.concatenate([zero, cm1, zero, n1], axis=1)], axis=0)
        sd = _dot(q_ref[0, rows, :], st)
        nd = nd_scr[slot, d]
        num = w_loc * nd[:, :2 * ML_DV] + w_int * sd[:, :2 * ML_DV]
        den = w_loc * nd[:, 2 * ML_DV:] + w_int * sd[:, 2 * ML_DV:]
        h_out = num / jnp.maximum(jnp.abs(den), floor)
        if d == 0:
            of_scr[rows, :] = h_out
        else:
            ob_scr[rows, :] = h_out

        bl = blast2_scr[d, pl.ds(cc, 1), :]
        wm = wsmax2_scr[d, pl.ds(cc, 1), :]
        m_new = jnp.maximum(bl + m_old, wm)
        dec = jnp.exp(bl + m_old - m_new)
        ws_scale = jnp.exp(wm - m_new)
        for blk in range(4):
            head = slice((blk // 2) * ML_DV, (blk // 2 + 1) * ML_DV)
            st_scr[d, blk] = blocks[blk] * dec[:, head] + u_scr[slot, d, blk] * ws_scale[:, head]
        m2_scr[d] = m_new

    _scan_pipeline(local, state)
    _scan_epilogue(of_scr, ob_scr, gate_ref, ng_ref, out_ref)


def _mlstm(q, k, v, gate, gcol, grow, ng):
    bsz = q.shape[0]
    pairs = ML_HEADS // 2
    return pl.pallas_call(
        _mlstm_kernel,
        out_shape=jax.ShapeDtypeStruct((bsz, S_ALL, ML_HEADS * ML_DV), BF16),
        grid=(bsz, pairs),
        in_specs=[
            pl.BlockSpec((1, S_ALL, 128), lambda b, p: (b, 0, p)),
            pl.BlockSpec((1, S_ALL, 128), lambda b, p: (b, 0, p)),
            pl.BlockSpec((1, S_ALL, 256), lambda b, p: (b, 0, p)),
            pl.BlockSpec((1, S_ALL, 256), lambda b, p: (b, 0, p)),
            pl.BlockSpec((1, 1, S_ALL, 8), lambda b, p: (b, p, 0, 0)),
            pl.BlockSpec((1, 2, 2, 1, N_CHUNKS, 128), lambda b, p: (b, 0, 0, p, 0, 0)),
            pl.BlockSpec((1, 1, 256), lambda b, p: (p, 0, 0)),
        ],
        out_specs=pl.BlockSpec((1, S_ALL, 256), lambda b, p: (b, 0, p)),
        scratch_shapes=[
            pltpu.VMEM((2, S_ALL, 128), F32),
            pltpu.VMEM((2, S_ALL, 128), F32),
            pltpu.VMEM((2, S_ALL, 128), F32),
            pltpu.VMEM((2, N_CHUNKS, 128), F32),
            pltpu.VMEM((2, N_CHUNKS, 128), F32),
            pltpu.VMEM((2, N_CHUNKS, 128), F32),
            pltpu.VMEM((2, N_CHUNKS, 2 * ML_DV), F32),
            pltpu.VMEM((2, N_CHUNKS, 2 * ML_DV), F32),
            pltpu.VMEM((S_ALL, 2 * ML_DV), F32),
            pltpu.VMEM((S_ALL, 2 * ML_DV), F32),
            pltpu.VMEM((2, 4, ML_DQK, ML_DV), F32),
            pltpu.VMEM((2, 1, 2 * ML_DV), F32),
            pltpu.VMEM((SCAN_SLOTS, 2, CHUNK, 4 * ML_DV), F32),
            pltpu.VMEM((SCAN_SLOTS, 2, 4, ML_DQK, ML_DV), F32),
        ],
        compiler_params=_params("parallel", "arbitrary"),
        name="mlstm_scan",
    )(q, k, v, gate, gcol, grow, ng)


def _rpb_table_kernel(rpb_ref, o_ref):
    h = pl.program_id(0)
    n_dc = 2 * NA_KW - 1
    qi = lax.broadcasted_iota(jnp.int32, (GRID_W, 128), 0)
    lane = lax.broadcasted_iota(jnp.int32, (GRID_W, 128), 1)
    wi = lane & (GRID_W - 1)
    second = lane >= GRID_W
    dc = jnp.clip(wi - qi + (NA_KW - 1), 0, n_dc - 1)
    cs = jnp.clip(qi - NA_KW // 2, 0, GRID_W - NA_KW)
    col_ok = (wi >= cs) & (wi < cs + NA_KW)
    for dr in range(2 * NA_KH - 2):
        base = h * ((2 * NA_KH - 1) * n_dc) + dr * n_dc
        acc = jnp.zeros((GRID_W, 128), F32)
        for c in range(n_dc):
            val = jnp.where(second, rpb_ref[base + n_dc + c], rpb_ref[base + c])
            acc = jnp.where(dc == c, val, acc)
        o_ref[0, dr] = jnp.where(col_ok, acc * LOG2E, NEG)


def _rpb_table(rpb):
    flat = rpb.reshape(-1)
    return pl.pallas_call(
        _rpb_table_kernel,
        out_shape=jax.ShapeDtypeStruct((NA_HEADS, 2 * NA_KH - 2, GRID_W, 128), F32),
        grid=(NA_HEADS,),
        in_specs=[pl.BlockSpec(memory_space=pltpu.SMEM)],
        out_specs=pl.BlockSpec((1, 2 * NA_KH - 2, GRID_W, 128), lambda h: (h, 0, 0, 0)),
        compiler_params=_params("arbitrary"),
        name="na_rpb_table",
    )(flat)


def _softmax_weights(s_parts, extra_logit=None):
    def lane_tiles(a):
        return [a[:, i:i + 128] for i in range(0, a.shape[1], 128)]

    m = functools.reduce(jnp.maximum, [t for s in s_parts for t in lane_tiles(s)]).max(axis=-1, keepdims=True)
    if extra_logit is not None:
        m = jnp.maximum(m, extra_logit)
    p_parts, p_tiles = [], []
    for s in s_parts:
        p = jnp.exp2(s - m)
        p_tiles += lane_tiles(p)
        p_parts.append(p.astype(BF16))
    l = functools.reduce(jnp.add, p_tiles).sum(axis=-1, keepdims=True)
    if extra_logit is not None:
        l = l + jnp.exp2(extra_logit - m)
    return p_parts, l


def _softmax_pv(s_parts, v_parts, extra_logit=None):
    p_parts, l = _softmax_weights(s_parts, extra_logit)
    o = None
    for p, v in zip(p_parts, v_parts):
        pv = _dot(p, v)
        o = pv if o is None else o + pv
    return o / l


def _stack_heads(x):
    head0 = lax.broadcasted_iota(jnp.int32, x.shape, 1) < 64
    zero = jnp.zeros_like(x)
    return jnp.concatenate([jnp.where(head0, x, zero), jnp.where(head0, zero, x)], axis=0)


def _merge_heads(o):
    n = o.shape[0] // 2
    head0 = lax.broadcasted_iota(jnp.int32, (n, o.shape[1]), 1) < 64
    return jnp.where(head0, o[:n], o[n:])


def _na_kernel(q_ref, k_ref, v_ref, gate_ref, tab_ref, out_ref, s_scr):
    kc = k_ref[0, 0:CTX_LEN, :]
    vc = v_ref[0, 0:CTX_LEN, :]
    n_lat = NA_KH * GRID_W

    def rows_of(r):
        rs = jnp.clip(r - NA_KH // 2, 0, GRID_ROWS - NA_KH)
        q_rows = pl.ds(pl.multiple_of(CTX_LEN + r * GRID_W, GRID_W), GRID_W)
        k_rows = pl.ds(pl.multiple_of(CTX_LEN + rs * GRID_W, GRID_W), n_lat)
        return rs, q_rows, k_rows

    def scores(r, slot):
        rs, q_rows, k_rows = rows_of(r)
        dr0 = rs - r + NA_KH - 1
        qs = _stack_heads(q_ref[0, q_rows, :])
        bias = jnp.concatenate(
            [jnp.concatenate([tab_ref[hh, dr0 + 2 * j] for j in range(NA_KH // 2)], axis=1)
             for hh in range(2)], axis=0)
        s_scr[slot, :, 0:n_lat] = _dot_nt(qs, k_ref[0, k_rows, :]) + bias
        s_scr[slot, :, n_lat:n_lat + CTX_LEN] = _dot_nt(qs, kc)

    def finish(r, slot):
        _, q_rows, k_rows = rows_of(r)
        s_lat = s_scr[slot, :, 0:n_lat]
        s_ctx = s_scr[slot, :, n_lat:n_lat + CTX_LEN]
        o = _merge_heads(_softmax_pv([s_lat, s_ctx], [v_ref[0, k_rows, :], vc]))
        out_ref[0, q_rows, :] = (o * _silu(gate_ref[0, q_rows, :].astype(F32))).astype(BF16)

    for r in range(NA_LOOKAHEAD):
        scores(r, r)

    def body(i, carry):
        for u in range(NA_UNROLL):
            r = i * NA_UNROLL + u
            scores(jnp.minimum(r + NA_LOOKAHEAD, GRID_ROWS - 1), (u + NA_LOOKAHEAD) % NA_UNROLL)
            finish(r, u)
        return carry

    lax.fori_loop(0, GRID_ROWS // NA_UNROLL, body, 0)

    for t in range(CTX_LEN // 128):
        rows = slice(t * 128, (t + 1) * 128)
        qs = _stack_heads(q_ref[0, rows, :])
        o = _merge_heads(_softmax_pv([_dot_nt(qs, kc)], [vc]))
        out_ref[0, rows, :] = (o * _silu(gate_ref[0, rows, :].astype(F32))).astype(BF16)


def _na(q, k, v, gate, table):
    bsz = q.shape[0]
    pairs = NA_HEADS // 2
    seq_spec = pl.BlockSpec((1, S_ALL, 128), lambda b, p: (b, 0, p))
    return pl.pallas_call(
        _na_kernel,
        out_shape=jax.ShapeDtypeStruct((bsz, S_ALL, NA_HEADS * NA_DH), BF16),
        grid=(bsz, pairs),
        in_specs=[seq_spec, seq_spec, seq_spec, seq_spec,
                  pl.BlockSpec((2, 2 * NA_KH - 2, GRID_W, 128), lambda b, p: (p, 0, 0, 0))],
        out_specs=seq_spec,
        scratch_shapes=[pltpu.VMEM((NA_UNROLL, 2 * GRID_W, NA_KH * GRID_W + CTX_LEN), F32)],
        compiler_params=_params("parallel", "arbitrary"),
        name="na_attn",
    )(q, k, v, gate, table)


def _swa_mask_table():
    span = SW_BLOCK + 2 * SW_WINDOW
    r = np.arange(SW_BLOCK)[:, None]
    c = np.arange(span)[None, :]
    cases = [np.abs(r + off - c) <= SW_WINDOW for off in (0, SW_WINDOW, 2 * SW_WINDOW)]
    cases.append(np.zeros((SW_BLOCK, span), bool))
    return jnp.asarray(np.where(np.stack(cases), 0.0, NEG), F32)


def _swa_kernel(sink_ref, q_ref, k_ref, v_ref, gate_ref, mask_ref, out_ref, s_scr):
    n_blocks = S_ALL // SW_BLOCK
    ctx_blocks = CTX_LEN // SW_BLOCK
    n_tiles = SW_HEADS // 2
    span = SW_BLOCK + 2 * SW_WINDOW
    kc = k_ref[0, 0:CTX_LEN, :]
    vc = v_ref[0, 0:CTX_LEN, :]
    first = lax.broadcasted_iota(jnp.int32, (2 * SW_BLOCK, 1), 0) < SW_BLOCK

    def rows_of(n):
        qstart = (n - ctx_blocks) * SW_BLOCK
        kstart = jnp.clip(qstart - SW_WINDOW, 0, SEQ - span)
        q_rows = pl.ds(pl.multiple_of(n * SW_BLOCK, SW_BLOCK), SW_BLOCK)
        k_rows = pl.ds(pl.multiple_of(CTX_LEN + kstart, SW_BLOCK), span)
        return q_rows, k_rows

    def scores(n, t, slot):
        q_rows, k_rows = rows_of(n)
        case = jnp.where(n < ctx_blocks, 3,
                         jnp.where(n == ctx_blocks, 0, jnp.where(n == n_blocks - 1, 2, 1)))
        qs = _stack_heads(q_ref[0, q_rows, t * 128:(t + 1) * 128])
        mb = mask_ref[case]
        s_scr[slot, :, 0:span] = _dot_nt(qs, k_ref[0, k_rows, :]) + jnp.concatenate([mb, mb], axis=0)
        s_scr[slot, :, span:span + CTX_LEN] = _dot_nt(qs, kc)

    def finish(n, t, slot):
        q_rows, k_rows = rows_of(n)
        cols = slice(t * 128, (t + 1) * 128)
        sink = jnp.where(first, sink_ref[t], sink_ref[t + n_tiles]) * LOG2E
        o = _merge_heads(_softmax_pv([s_scr[slot, :, 0:span], s_scr[slot, :, span:span + CTX_LEN]],
                                     [v_ref[0, k_rows, :], vc], extra_logit=sink))
        out_ref[0, q_rows, cols] = (o * _silu(gate_ref[0, q_rows, cols].astype(F32))).astype(BF16)

    for t in range(SW_LOOKAHEAD):
        scores(0, t, t)

    def body(n, carry):
        for t in range(n_tiles):
            ahead = t + SW_LOOKAHEAD
            if ahead < n_tiles:
                scores(n, ahead, ahead)
            else:
                scores(jnp.minimum(n + 1, n_blocks - 1), ahead - n_tiles, ahead - n_tiles)
            finish(n, t, t)
        return carry

    lax.fori_loop(0, n_blocks, body, 0)


def _swa(q, k, v, gate, sink):
    bsz = q.shape[0]
    width = SW_HEADS * SW_DH
    span = SW_BLOCK + 2 * SW_WINDOW
    seq_spec = pl.BlockSpec((1, S_ALL, width), lambda b: (b, 0, 0))
    kv_spec = pl.BlockSpec((1, S_ALL, 128), lambda b: (b, 0, 0))
    return pl.pallas_call(
        _swa_kernel,
        out_shape=jax.ShapeDtypeStruct((bsz, S_ALL, width), BF16),
        grid=(bsz,),
        in_specs=[pl.BlockSpec(memory_space=pltpu.SMEM), seq_spec, kv_spec, kv_spec, seq_spec,
                  pl.BlockSpec((4, SW_BLOCK, span), lambda b: (0, 0, 0))],
        out_specs=seq_spec,
        scratch_shapes=[pltpu.VMEM((SW_HEADS // 2, 2 * SW_BLOCK, span + CTX_LEN), F32)],
        compiler_params=_params("parallel"),
        name="swa_attn",
    )(sink, q, k, v, gate, _swa_mask_table())


def _pad_cols(w, width):
    return jnp.pad(w, ((0, 0), (0, width - w.shape[1])))


def _prep_even(w_in, gate_w, gate_b):
    o = np.cumsum((0, 256, 256, 512, 512, 32, 512, 512, 512, 512))
    gq, gk, gv, gg, lra, nq, nk, nv, ng = [w_in[:, o[i]:o[i + 1]] for i in range(9)]
    w = jnp.concatenate([gq * GLA_DK ** -0.5, gk, gv, gg, nq * (NA_DH ** -0.5 * LOG2E), nk, nv, ng,
                         _pad_cols(lra, 128)], axis=1).astype(BF16)
    gws, gbs = [], []
    for p in range(GLA_HEADS // 2):
        cols = slice(p * 128, (p + 1) * 128)
        zero = jnp.zeros((GLA_RANK, 128), F32)
        gws.append(jnp.concatenate([jnp.concatenate([gate_w[0][:, cols], zero], axis=1),
                                    jnp.concatenate([zero, gate_w[1][:, cols]], axis=1)], axis=0))
        gbs.append(jnp.concatenate([gate_b[0][cols], gate_b[1][cols]])[None])
    return w, jnp.stack(gws).astype(BF16), jnp.stack(gbs)


def _swa_head_perm():
    order = []
    for t in range(SW_HEADS // 2):
        order += list(range(t * SW_DH, (t + 1) * SW_DH))
        order += list(range((t + SW_HEADS // 2) * SW_DH, (t + SW_HEADS // 2 + 1) * SW_DH))
    return np.asarray(order)


def _rope_swap(w):
    nf = SW_DH // 4
    idx = np.arange(w.shape[1]).reshape(-1, 2, nf)[:, ::-1, :].reshape(-1)
    return w[:, idx]


def _prep_odd(w_in, w_out):
    o = np.cumsum((0, 256, 256, 512, 512, 16, 512, 128, 128, 512))
    mq, mk, mv, mg, gates, sq, sk, sv, sg = [w_in[:, o[i]:o[i + 1]] for i in range(9)]
    perm = _swa_head_perm()
    sq = sq[:, perm] * (SW_DH ** -0.5 * LOG2E)
    w = jnp.concatenate([mq, mk * ML_DQK ** -0.5, mv, mg, sq, _rope_swap(sq), sk, _rope_swap(sk),
                         sv, _pad_cols(gates, 128), sg[:, perm]], axis=1).astype(BF16)
    half = ML_HEADS * ML_DV
    w_o = jnp.concatenate([w_out[:half], w_out[half:][perm]], axis=0).astype(BF16)
    return w, w_o


def _rope_tables():
    nf = SW_DH // 4
    freqs = ROPE_BASE ** (-jnp.arange(nf, dtype=F32) / nf)
    pos = jnp.arange(SEQ)
    rows = (pos // GRID_W).astype(F32)
    cols = (pos % GRID_W).astype(F32)
    ar = rows[:, None] * freqs[None, :]
    ac = cols[:, None] * freqs[None, :]
    cos = jnp.concatenate([jnp.cos(ar), jnp.cos(ar), jnp.cos(ac), jnp.cos(ac)], axis=1)
    sin = jnp.concatenate([-jnp.sin(ar), jnp.sin(ar), -jnp.sin(ac), jnp.sin(ac)], axis=1)
    cos = jnp.concatenate([jnp.ones((CTX_LEN, SW_DH), F32), cos], axis=0)
    sin = jnp.concatenate([jnp.zeros((CTX_LEN, SW_DH), F32), sin], axis=0)
    return jnp.tile(cos, (1, 2)), jnp.tile(sin, (1, 2))


def _gate_layouts(gates):
    bsz = gates.shape[0]
    g = gates.reshape(bsz, S_ALL, 2, 2, 2, 2)
    gcol = g.transpose(0, 4, 1, 2, 3, 5).reshape(bsz, 2, S_ALL, 8)
    g = gates.reshape(bsz, N_CHUNKS, CHUNK, 2, 2, 2, 2)
    grow = g.transpose(0, 3, 4, 5, 1, 6, 2).reshape(bsz, 2, 2, 2, N_CHUNKS, 2 * CHUNK)
    return gcol, grow


def kernel(x, c, ctx, c_ctx, w_ada, b_ada, ln_g, ln_b, w_in_even, w_out_even, gla_gate_w, gla_gate_b,
           gla_norm_g, na_rpb, w_in_odd, w_out_odd, ml_gate_b, ml_norm_g, sw_sink):
    bsz = x.shape[0]
    mod_rows = 16
    cvec = jnp.zeros((mod_rows, D_MODEL), F32).at[:bsz].set(c).at[bsz].set(c_ctx)
    ada = _ada(cvec, w_ada, b_ada)
    xs = (ctx, x)
    cos_t, sin_t = _rope_tables()
    for l in range(DEPTH):
        i = l // 2
        last = l == DEPTH - 1
        lat = ada[l, :bsz].reshape(bsz, 1, 3, D_MODEL)
        cx = jnp.broadcast_to(ada[l, bsz].reshape(1, 1, 3, D_MODEL), (bsz, 1, 3, D_MODEL))
        mod = jnp.concatenate([lat, cx], axis=1)
        if l % 2 == 0:
            w, gw, gb = _prep_even(w_in_even[i], gla_gate_w[i], gla_gate_b[i])
            gq, gk, gv, gg, nq, nk, nv, ng, lr = _inproj_even(xs, mod, w)
            ya = _gla(gq, gk, gv, gg, lr, gw, gb, gla_norm_g[i].reshape(GLA_HEADS // 2, 1, 256))
            yb = _na(nq, nk, nv, ng, _rpb_table(na_rpb[i]))
            w_o = w_out_even[i].astype(BF16)
        else:
            w, w_o = _prep_odd(w_in_odd[i], w_out_odd[i])
            mq, mk, mv, mg, sq, sk, sv, sg, gates = _inproj_odd(
                xs[0], mod, w, cos_t, sin_t, ml_gate_b[i].reshape(1, 4 * ML_HEADS))
            gcol, grow = _gate_layouts(gates)
            ya = _mlstm(mq, mk, mv, mg, gcol, grow, ml_norm_g[i].reshape(ML_HEADS // 2, 1, 256))
            yb = _swa(sq, sk, sv, sg, sw_sink[i])
        xs = (_outproj(ya, yb, xs, mod, w_o, ln_g[l].reshape(1, D_MODEL), ln_b[l].reshape(1, D_MODEL), last),)
    return xs[0]
```

```python
import functools
import math

import numpy as np
import jax
import jax.numpy as jnp
from jax import lax
from jax.experimental import pallas as pl
from jax.experimental.pallas import tpu as pltpu

F32 = jnp.float32
BF16 = jnp.bfloat16

D_MODEL = 1024
SEQ = 4096
DEPTH = 4
GRID_W = 64
CTX_LEN = 256
S_ALL = CTX_LEN + SEQ
ALPHA = (2.0 * DEPTH) ** 0.25
LN_EPS = 1e-5
NORM_EPS = 1e-6
NEG = -1e30
LOG2E = 1.4426950408889634
GLA_HEADS, GLA_DK, GLA_DV, GLA_RANK, GLA_TAU = 4, 64, 128, 16, 16.0
NA_HEADS, NA_DH, NA_KH, NA_KW = 8, 64, 8, 16
ML_HEADS, ML_DQK, ML_DV = 4, 64, 128
SW_HEADS, SW_KV_HEADS, SW_DH, SW_WINDOW, SW_BLOCK = 8, 2, 64, 128, 128
ROPE_BASE = 10000.0
CHUNK = 64
N_CHUNKS = S_ALL // CHUNK
CTX_CHUNKS = CTX_LEN // CHUNK
ROW_TILE = 256
N_ROW_TILES = S_ALL // ROW_TILE
GRID_ROWS = SEQ // GRID_W
OUT_SUB = 4
SCAN_SLOTS = 4
SCAN_LOOKAHEAD = 2
NA_UNROLL = 4
NA_LOOKAHEAD = 2
SW_LOOKAHEAD = 1

VMEM_LIMIT = 56 * 1024 * 1024

NT_DIMS = (((1,), (1,)), ((), ()))
TN_DIMS = (((0,), (0,)), ((), ()))


def _dot(a, b):
    return jnp.dot(a, b, preferred_element_type=F32)


def _dot_nt(a, b):
    return lax.dot_general(a, b, NT_DIMS, preferred_element_type=F32)


def _dot_tn(a, b):
    return lax.dot_general(a, b, TN_DIMS, preferred_element_type=F32)


def _silu(x):
    h = 0.5 * x
    return h + h * jnp.tanh(h)


def _log_sigmoid(x):
    return jnp.minimum(x, 0.0) - jnp.log(1.0 + jnp.exp(-jnp.abs(x)))


def _cumsum_rows(x, reverse):
    n = x.shape[0]
    row = lax.broadcasted_iota(jnp.int32, x.shape, 0)
    k = 1
    while k < n:
        if reverse:
            x = x + jnp.where(row < n - k, pltpu.roll(x, n - k, axis=0), 0.0)
        else:
            x = x + jnp.where(row >= k, pltpu.roll(x, k, axis=0), 0.0)
        k *= 2
    return x


def _split_dot(mat_bf16, x):
    hi = x.astype(BF16)
    lo = (x - hi.astype(F32)).astype(BF16)
    return _dot(mat_bf16, hi) + _dot(mat_bf16, lo)


def _split_dot_r(x, mat_bf16):
    hi = x.astype(BF16)
    lo = (x - hi.astype(F32)).astype(BF16)
    return _dot(hi, mat_bf16) + _dot(lo, mat_bf16)


def _halves_to_tiles(x):
    half0 = lax.broadcasted_iota(jnp.int32, x.shape, 1) < 64
    sw = pltpu.roll(x, 64, axis=1)
    return jnp.concatenate([jnp.where(half0, x, sw), jnp.where(half0, sw, x)], axis=1)


def _params(*sem):
    return pltpu.CompilerParams(dimension_semantics=sem, vmem_limit_bytes=VMEM_LIMIT)


def _ada_kernel(c_ref, w_ref, b_ref, o_ref):
    cond = _silu(c_ref[...])
    o_ref[0] = jnp.dot(cond, w_ref[0], precision=lax.Precision.HIGHEST,
                       preferred_element_type=F32) + b_ref[0]


def _ada(cvec, w_ada, b_ada):
    rows = cvec.shape[0]
    nblk = 3
    return pl.pallas_call(
        _ada_kernel,
        out_shape=jax.ShapeDtypeStruct((DEPTH, rows, 3 * D_MODEL), F32),
        grid=(DEPTH, nblk),
        in_specs=[
            pl.BlockSpec((rows, D_MODEL), lambda l, n: (0, 0)),
            pl.BlockSpec((1, D_MODEL, D_MODEL), lambda l, n: (l, 0, n)),
            pl.BlockSpec((1, 1, D_MODEL), lambda l, n: (l, 0, n)),
        ],
        out_specs=pl.BlockSpec((1, rows, D_MODEL), lambda l, n: (l, 0, n)),
        compiler_params=_params("arbitrary", "arbitrary"),
        name="ada_mod",
    )(cvec, w_ada, b_ada.reshape(DEPTH, 1, 3 * D_MODEL))


def _row_tile(x_refs, is_ctx):
    if len(x_refs) == 1:
        return x_refs[0][0]
    return jnp.where(is_ctx, x_refs[0][0], x_refs[1][0])


def _x_specs(xs, tile_of):
    def combined(*ids):
        b, j = tile_of(*ids)
        return (b, j, 0)

    def context(*ids):
        return (tile_of(*ids)[0], 0, 0)

    def latent(*ids):
        b, j = tile_of(*ids)
        return (b, jnp.maximum(j - 1, 0), 0)

    if len(xs) == 1:
        return [pl.BlockSpec((1, ROW_TILE, D_MODEL), combined)]
    return [pl.BlockSpec((1, ROW_TILE, D_MODEL), context), pl.BlockSpec((1, ROW_TILE, D_MODEL), latent)]


def _modulated(x_refs, mod_ref):
    shift = mod_ref[0, 0, 0:1, :]
    scale = mod_ref[0, 0, 1:2, :]
    x = _row_tile(x_refs, pl.program_id(1) == 0)
    return (x * (1.0 + scale) + shift).astype(BF16)


EVEN_SLABS = (256, 256, 512, 512, 512, 512, 512, 512)


def _inproj_even_kernel(*refs, n_x):
    x_refs, (mod_ref, w_ref), outs = refs[:n_x], refs[n_x:n_x + 2], refs[n_x + 2:]
    h = _modulated(x_refs, mod_ref)
    off = 0
    for ref, width in zip(outs[:-1], EVEN_SLABS):
        ref[0] = _dot(h, w_ref[:, off:off + width]).astype(ref.dtype)
        off += width
    lr = _dot(h, w_ref[:, off:off + 128])
    outs[-1][0] = lr[:, :2 * GLA_RANK]


ODD_SLABS = (256, 256, 512, 512)


def _inproj_odd_kernel(x_ref, mod_ref, w_ref, cos_ref, sin_ref, gb_ref,
                       mq, mk, mv, mg, sq, sk, sv, sg, gates):
    h = _modulated((x_ref,), mod_ref)
    off = 0
    for ref, width in zip((mq, mk, mv, mg), ODD_SLABS):
        ref[0] = _dot(h, w_ref[:, off:off + width]).astype(ref.dtype)
        off += width
    cos = cos_ref[...]
    sin = sin_ref[...]
    a = _dot(h, w_ref[:, off:off + 512])
    b = _dot(h, w_ref[:, off + 512:off + 1024])
    for t in range(4):
        cols = slice(t * 128, (t + 1) * 128)
        sq[0, :, cols] = (a[:, cols] * cos + b[:, cols] * sin).astype(BF16)
    off += 1024
    kvg = _dot(h, w_ref[:, off:off + 512])
    sk[0] = (kvg[:, 0:128] * cos + kvg[:, 128:256] * sin).astype(BF16)
    sv[0] = kvg[:, 256:384].astype(BF16)
    gates[0] = kvg[:, 384:384 + 4 * ML_HEADS] + gb_ref[...]
    off += 512
    sg[0] = _dot(h, w_ref[:, off:off + 512]).astype(BF16)


def _mod_index(b, j):
    return (b, jnp.where(j == 0, 1, 0), 0, 0)


def _inproj_even(xs, mod, w):
    bsz = xs[0].shape[0]
    ntot = w.shape[1]
    out_shape = [jax.ShapeDtypeStruct((bsz, S_ALL, wd), BF16) for wd in EVEN_SLABS]
    out_shape.append(jax.ShapeDtypeStruct((bsz, S_ALL, 2 * GLA_RANK), F32))
    out_specs = [pl.BlockSpec((1, ROW_TILE, wd), lambda b, j: (b, j, 0)) for wd in EVEN_SLABS]
    out_specs.append(pl.BlockSpec((1, ROW_TILE, 2 * GLA_RANK), lambda b, j: (b, j, 0)))
    return pl.pallas_call(
        functools.partial(_inproj_even_kernel, n_x=len(xs)),
        out_shape=out_shape,
        grid=(bsz, N_ROW_TILES),
        in_specs=_x_specs(xs, lambda b, j: (b, j)) + [
            pl.BlockSpec((1, 1, 3, D_MODEL), _mod_index),
            pl.BlockSpec((D_MODEL, ntot), lambda b, j: (0, 0)),
        ],
        out_specs=out_specs,
        compiler_params=_params("parallel", "arbitrary"),
        name="inproj_even",
    )(*xs, mod, w)


def _inproj_odd(xc, mod, w, cos_t, sin_t, gate_b):
    bsz = xc.shape[0]
    ntot = w.shape[1]
    widths = (256, 256, 512, 512, 512, 128, 128, 512)
    out_shape = [jax.ShapeDtypeStruct((bsz, S_ALL, wd), BF16) for wd in widths]
    out_shape.append(jax.ShapeDtypeStruct((bsz, S_ALL, 4 * ML_HEADS), F32))
    out_specs = [pl.BlockSpec((1, ROW_TILE, wd), lambda b, j: (b, j, 0)) for wd in widths]
    out_specs.append(pl.BlockSpec((1, ROW_TILE, 4 * ML_HEADS), lambda b, j: (b, j, 0)))
    return pl.pallas_call(
        _inproj_odd_kernel,
        out_shape=out_shape,
        grid=(bsz, N_ROW_TILES),
        in_specs=[
            pl.BlockSpec((1, ROW_TILE, D_MODEL), lambda b, j: (b, j, 0)),
            pl.BlockSpec((1, 1, 3, D_MODEL), _mod_index),
            pl.BlockSpec((D_MODEL, ntot), lambda b, j: (0, 0)),
            pl.BlockSpec((ROW_TILE, 128), lambda b, j: (j, 0)),
            pl.BlockSpec((ROW_TILE, 128), lambda b, j: (j, 0)),
            pl.BlockSpec((1, 4 * ML_HEADS), lambda b, j: (0, 0)),
        ],
        out_specs=out_specs,
        compiler_params=_params("parallel", "arbitrary"),
        name="inproj_odd",
    )(xc, mod, w, cos_t, sin_t, gate_b)


def _outproj_kernel(*refs, n_x, n_tiles, n_sub):
    ya_refs, yb_refs = refs[:n_sub], refs[n_sub:2 * n_sub]
    x_refs = refs[2 * n_sub:2 * n_sub + n_sub * n_x]
    mod_refs = refs[2 * n_sub + n_sub * n_x:3 * n_sub + n_sub * n_x]
    w_ref, g_ref, b_ref = refs[3 * n_sub + n_sub * n_x:3 * n_sub + n_sub * n_x + 3]
    o_ref, r_scr = refs[-2:]
    t = pl.program_id(0)

    @pl.when(t == 0)
    def _():
        r_scr[...] = jnp.zeros(r_scr.shape, F32)

    for s in range(n_sub):
        r = r_scr[s]
        mu = jnp.mean(r, axis=-1, keepdims=True)
        d = r - mu
        var = jnp.mean(d * d, axis=-1, keepdims=True)
        o_ref[s * ROW_TILE:(s + 1) * ROW_TILE, :] = d * lax.rsqrt(var + LN_EPS) * g_ref[...] + b_ref[...]

    half = ya_refs[0].shape[2]
    ya = jnp.concatenate([ref[0] for ref in ya_refs], axis=0)
    yb = jnp.concatenate([ref[0] for ref in yb_refs], axis=0)
    y = _dot(ya, w_ref[0:half, :]) + _dot(yb, w_ref[half:2 * half, :])
    for s in range(n_sub):
        x = _row_tile(x_refs[s * n_x:(s + 1) * n_x], lax.rem(n_sub * t + s, n_tiles) == 0)
        r_scr[s] = ALPHA * x + mod_refs[s][0, 0, 2:3, :] * y[s * ROW_TILE:(s + 1) * ROW_TILE]


def _outproj(ya, yb, xs, mod, w, ln_g, ln_b, latent_only):
    bsz = xs[0].shape[0]
    skip = 1 if latent_only else 0
    n_tiles = N_ROW_TILES - skip
    total = bsz * n_tiles
    n_sub = math.gcd(total, OUT_SUB)
    half = ya.shape[2]

    def tile_of(s):
        def fn(t):
            tt = jnp.minimum(n_sub * t + s, total - 1)
            return tt // n_tiles, lax.rem(tt, n_tiles) + skip
        return fn

    def y_index(s):
        def fn(t):
            b, j = tile_of(s)(t)
            return (b, j, 0)
        return fn

    def mod_index(s):
        return lambda t: _mod_index(*tile_of(s)(t))

    subs = range(n_sub)
    const = lambda t: (0, 0)
    x_specs = [spec for s in subs for spec in _x_specs(xs, tile_of(s))]
    out = pl.pallas_call(
        functools.partial(_outproj_kernel, n_x=len(xs), n_tiles=n_tiles, n_sub=n_sub),
        out_shape=jax.ShapeDtypeStruct((total * ROW_TILE, D_MODEL), F32),
        grid=(total // n_sub + 1,),
        in_specs=[pl.BlockSpec((1, ROW_TILE, half), y_index(s)) for s in subs]
        + [pl.BlockSpec((1, ROW_TILE, half), y_index(s)) for s in subs]
        + x_specs
        + [pl.BlockSpec((1, 1, 3, D_MODEL), mod_index(s)) for s in subs]
        + [pl.BlockSpec((2 * half, D_MODEL), const),
           pl.BlockSpec((1, D_MODEL), const),
           pl.BlockSpec((1, D_MODEL), const)],
        out_specs=pl.BlockSpec((n_sub * ROW_TILE, D_MODEL), lambda t: (jnp.maximum(t - 1, 0), 0)),
        scratch_shapes=[pltpu.VMEM((n_sub, ROW_TILE, D_MODEL), F32)],
        compiler_params=_params("arbitrary"),
        name="outproj_ln",
    )(*([ya] * n_sub), *([yb] * n_sub), *(list(xs) * n_sub), *([mod] * n_sub), w, ln_g, ln_b)
    return out.reshape(bsz, n_tiles * ROW_TILE, D_MODEL)


def _scan_epilogue(of_scr, ob_scr, gate_ref, ng_ref, out_ref):
    def body(t, carry):
        rows = pl.ds(pl.multiple_of(t * ROW_TILE, ROW_TILE), ROW_TILE)
        o = of_scr[rows, :] + ob_scr[rows, :]
        parts = []
        for hh in range(2):
            oh = o[:, hh * 128:(hh + 1) * 128]
            ms = jnp.mean(oh * oh, axis=-1, keepdims=True)
            parts.append(oh * lax.rsqrt(ms + NORM_EPS))
        y = jnp.concatenate(parts, axis=1) * ng_ref[0]
        out_ref[0, rows, :] = (y * _silu(gate_ref[0, rows, :].astype(F32))).astype(BF16)
        return carry

    lax.fori_loop(0, N_ROW_TILES, body, 0)


def _bwd_chunk(i):
    return jnp.where(i < CTX_CHUNKS, CTX_CHUNKS - 1 - i, N_CHUNKS - 1 + CTX_CHUNKS - i)


def _scan_pipeline(local_fn, state_fn):
    for i in range(SCAN_LOOKAHEAD):
        local_fn(i, 0, i)
        local_fn(_bwd_chunk(i), 1, i)

    def body(j, carry):
        for u in range(SCAN_SLOTS):
            i = SCAN_SLOTS * j + u
            nxt = jnp.minimum(i + SCAN_LOOKAHEAD, N_CHUNKS - 1)
            local_fn(nxt, 0, (u + SCAN_LOOKAHEAD) % SCAN_SLOTS)
            local_fn(_bwd_chunk(nxt), 1, (u + SCAN_LOOKAHEAD) % SCAN_SLOTS)
            state_fn(i, 0, u)
            state_fn(_bwd_chunk(i), 1, u)
        return carry

    lax.fori_loop(0, N_CHUNKS // SCAN_SLOTS, body, 0)


def _gla_kernel(q_ref, k_ref, v_ref, gate_ref, lr_ref, gw_ref, gb_ref, ng_ref, out_ref,
                g_scr, of_scr, ob_scr, st_scr, a_scr, qin_scr, u_scr, dl_scr):
    def pre(t, carry):
        rows = pl.ds(pl.multiple_of(t * ROW_TILE, ROW_TILE), ROW_TILE)
        z = _dot(lr_ref[0, rows, :].astype(BF16), gw_ref[0]) + gb_ref[0]
        g_scr[rows, :] = _log_sigmoid(z) * (1.0 / GLA_TAU)
        return carry

    lax.fori_loop(0, N_ROW_TILES, pre, 0)
    st_scr[...] = jnp.zeros(st_scr.shape, F32)

    def local(cc, d, slot):
        rows = pl.ds(pl.multiple_of(cc * CHUNK, CHUNK), CHUNK)
        g = g_scr[rows, d * 128:(d + 1) * 128]
        b = _cumsum_rows(g, reverse=(d == 1))
        b_last = b[0:1, :] if d == 1 else b[CHUNK - 1:CHUNK, :]
        q = q_ref[0, rows, :].astype(F32)
        k = k_ref[0, rows, :].astype(F32)
        q_in = q * jnp.exp(b)
        k_in = (k * jnp.exp(-b)).astype(BF16)
        k_dec = (k * jnp.exp(b_last - b)).astype(BF16)

        head0 = lax.broadcasted_iota(jnp.int32, (CHUNK, 128), 1) < GLA_DK
        q_stack = jnp.concatenate([jnp.where(head0, q_in, 0.0), jnp.where(head0, 0.0, q_in)],
                                  axis=0).astype(BF16)
        a = _dot_nt(q_stack, k_in)
        t_i = lax.broadcasted_iota(jnp.int32, (2 * CHUNK, CHUNK), 0) & (CHUNK - 1)
        s_i = lax.broadcasted_iota(jnp.int32, (2 * CHUNK, CHUNK), 1)
        keep = (s_i >= t_i) if d == 1 else (s_i <= t_i)
        a_scr[slot, d] = jnp.where(keep, a, 0.0).astype(BF16)
        qin_scr[slot, d] = q_in.astype(BF16)

        u = _dot_tn(v_ref[0, rows, :], k_dec)
        r_i = lax.broadcasted_iota(jnp.int32, u.shape, 0) < GLA_DV
        c_i = lax.broadcasted_iota(jnp.int32, u.shape, 1) < GLA_DK
        u_scr[slot, d] = jnp.where(r_i == c_i, u, 0.0)
        dl_scr[slot, d] = jnp.exp(b_last)

    def state(cc, d, slot):
        rows = pl.ds(pl.multiple_of(cc * CHUNK, CHUNK), CHUNK)
        v = v_ref[0, rows, :]
        a = a_scr[slot, d]
        vhead0 = lax.broadcasted_iota(jnp.int32, (CHUNK, 2 * GLA_DV), 1) < GLA_DV
        zero = jnp.zeros_like(v)
        st = st_scr[d]
        o = (_dot(a[:CHUNK], jnp.where(vhead0, v, zero))
             + _dot(a[CHUNK:], jnp.where(vhead0, zero, v))
             + _dot_nt(qin_scr[slot, d], st.astype(BF16)))
        if d == 0:
            of_scr[rows, :] = o
        else:
            ob_scr[rows, :] = o
        st_scr[d] = st * dl_scr[slot, d] + u_scr[slot, d]

    _scan_pipeline(local, state)
    _scan_epilogue(of_scr, ob_scr, gate_ref, ng_ref, out_ref)


def _gla(q, k, v, gate, lr, gw, gb, ng):
    bsz = q.shape[0]
    pairs = GLA_HEADS // 2
    return pl.pallas_call(
        _gla_kernel,
        out_shape=jax.ShapeDtypeStruct((bsz, S_ALL, GLA_HEADS * GLA_DV), BF16),
        grid=(bsz, pairs),
        in_specs=[
            pl.BlockSpec((1, S_ALL, 128), lambda b, p: (b, 0, p)),
            pl.BlockSpec((1, S_ALL, 128), lambda b, p: (b, 0, p)),
            pl.BlockSpec((1, S_ALL, 256), lambda b, p: (b, 0, p)),
            pl.BlockSpec((1, S_ALL, 256), lambda b, p: (b, 0, p)),
            pl.BlockSpec((1, S_ALL, 2 * GLA_RANK), lambda b, p: (b, 0, 0)),
            pl.BlockSpec((1, 2 * GLA_RANK, 256), lambda b, p: (p, 0, 0)),
            pl.BlockSpec((1, 1, 256), lambda b, p: (p, 0, 0)),
            pl.BlockSpec((1, 1, 256), lambda b, p: (p, 0, 0)),
        ],
        out_specs=pl.BlockSpec((1, S_ALL, 256), lambda b, p: (b, 0, p)),
        scratch_shapes=[
            pltpu.VMEM((S_ALL, 256), F32),
            pltpu.VMEM((S_ALL, 256), F32),
            pltpu.VMEM((S_ALL, 256), F32),
            pltpu.VMEM((2, 2 * GLA_DV, 2 * GLA_DK), F32),
            pltpu.VMEM((SCAN_SLOTS, 2, 2 * CHUNK, CHUNK), BF16),
            pltpu.VMEM((SCAN_SLOTS, 2, CHUNK, 2 * GLA_DK), BF16),
            pltpu.VMEM((SCAN_SLOTS, 2, 2 * GLA_DV, 2 * GLA_DK), F32),
            pltpu.VMEM((SCAN_SLOTS, 2, 1, 2 * GLA_DK), F32),
        ],
        compiler_params=_params("parallel", "arbitrary"),
        name="gla_scan",
    )(q, k, v, gate, lr, gw, gb, ng)


def _cummax_rows_chunked(x, reverse):
    n = x.shape[0]
    pos = lax.broadcasted_iota(jnp.int32, x.shape, 0) & (CHUNK - 1)
    k = 1
    while k < CHUNK:
        if reverse:
            x = jnp.maximum(x, jnp.where(pos < CHUNK - k, pltpu.roll(x, n - k, axis=0), NEG))
        else:
            x = jnp.maximum(x, jnp.where(pos >= k, pltpu.roll(x, k, axis=0), NEG))
        k *= 2
    return x


def _mlstm_kernel(q_ref, k_ref, v_ref, gate_ref, gcol_ref, grow_ref, ng_ref, out_ref,
                  bcm_scr, rcm_scr, cmx_scr, rrow_scr, blast_scr, wsmax_scr, blast2_scr, wsmax2_scr,
                  of_scr, ob_scr, st_scr, m2_scr, nd_scr, u_scr):
    lane128 = lax.broadcasted_iota(jnp.int32, (1, 128), 1)
    half0_row = lane128 < CHUNK

    li = lax.broadcasted_iota(jnp.int32, (128, 128), 0)
    lj = lax.broadcasted_iota(jnp.int32, (128, 128), 1)
    same_half = (li < CHUNK) == (lj < CHUNK)
    ones_half = jnp.where(same_half, 1.0, 0.0).astype(BF16)
    for d in range(2):
        within = (li >= lj) if d == 1 else (li <= lj)
        tri = jnp.where(same_half & within, 1.0, 0.0).astype(BF16)
        lf = _log_sigmoid(grow_ref[0, d, 1, 0])
        b_row = _split_dot_r(lf, tri)
        total = _split_dot_r(lf, ones_half)
        r_row = grow_ref[0, d, 0, 0] - b_row
        rrow_scr[d] = r_row
        blast_scr[d] = total
        mx0 = jnp.max(jnp.where(half0_row, r_row, NEG), axis=-1, keepdims=True)
        mx1 = jnp.max(jnp.where(half0_row, NEG, r_row), axis=-1, keepdims=True)
        wsmax = total + jnp.where(half0_row, mx0, mx1)
        wsmax_scr[d] = wsmax
        blast2_scr[d] = _halves_to_tiles(total)
        wsmax2_scr[d] = _halves_to_tiles(wsmax)

    ti = lax.broadcasted_iota(jnp.int32, (ROW_TILE, ROW_TILE), 0)
    tj = lax.broadcasted_iota(jnp.int32, (ROW_TILE, ROW_TILE), 1)
    same_chunk = (ti >> 6) == (tj >> 6)
    tril = jnp.where(same_chunk & (tj <= ti), 1.0, 0.0).astype(BF16)
    triu = jnp.where(same_chunk & (tj >= ti), 1.0, 0.0).astype(BF16)
    half0_tile = lax.broadcasted_iota(jnp.int32, (ROW_TILE, 128), 1) < CHUNK

    def pre(t, carry):
        rows = pl.ds(pl.multiple_of(t * ROW_TILE, ROW_TILE), ROW_TILE)
        gc = gcol_ref[0, 0, rows, :]
        lf = _log_sigmoid(gc)
        for d in range(2):
            bcol = _split_dot(triu if d == 1 else tril, lf)
            b0 = jnp.broadcast_to(bcol[:, d * 4 + 2:d * 4 + 3], (ROW_TILE, 128))
            b1 = jnp.broadcast_to(bcol[:, d * 4 + 3:d * 4 + 4], (ROW_TILE, 128))
            i0 = jnp.broadcast_to(gc[:, d * 4:d * 4 + 1], (ROW_TILE, 128))
            i1 = jnp.broadcast_to(gc[:, d * 4 + 1:d * 4 + 2], (ROW_TILE, 128))
            bc = jnp.where(half0_tile, b0, b1)
            rc = jnp.where(half0_tile, i0, i1) - bc
            bcm_scr[d, rows, :] = bc
            rcm_scr[d, rows, :] = rc
            cmx_scr[d, rows, :] = _cummax_rows_chunked(rc, reverse=(d == 1))
        return carry

    lax.fori_loop(0, N_ROW_TILES, pre, 0)
    st_scr[...] = jnp.zeros(st_scr.shape, F32)
    m2_scr[...] = jnp.zeros(m2_scr.shape, F32)

    def local(cc, d, slot):
        rows = pl.ds(pl.multiple_of(cc * CHUNK, CHUNK), CHUNK)
        rc = rcm_scr[d, rows, :]
        cmx = cmx_scr[d, rows, :]
        rr = rrow_scr[d, pl.ds(cc, 1), :]
        bl = blast_scr[d, pl.ds(cc, 1), :]
        wm = wsmax_scr[d, pl.ds(cc, 1), :]
        q = q_ref[0, rows, :]
        k = k_ref[0, rows, :]
        v = v_ref[0, rows, :]

        half0 = lax.broadcasted_iota(jnp.int32, (CHUNK, 128), 1) < CHUNK
        t_i = lax.broadcasted_iota(jnp.int32, (CHUNK, 128), 0)
        s_i = lax.broadcasted_iota(jnp.int32, (CHUNK, 128), 1) & (CHUNK - 1)
        keep = (s_i >= t_i) if d == 1 else (s_i <= t_i)
        w = jnp.exp(jnp.where(keep, rr - cmx, NEG))

        zero_k = jnp.zeros_like(k)
        k_stack = jnp.concatenate([jnp.where(half0, k, zero_k), jnp.where(half0, zero_k, k)], axis=0)
        s_w = (_dot_nt(q, k_stack) * w).astype(BF16)

        vhead0 = lax.broadcasted_iota(jnp.int32, (CHUNK, 2 * ML_DV), 1) < ML_DV
        zero_v = jnp.zeros_like(v)
        one_v = jnp.ones_like(v)
        ones0 = jnp.where(vhead0, 1.0, 0.0).astype(BF16)
        ones1 = jnp.where(vhead0, 0.0, 1.0).astype(BF16)
        vo_stack = jnp.concatenate(
            [jnp.concatenate([jnp.where(vhead0, v, zero_v), ones0], axis=1),
             jnp.concatenate([jnp.where(vhead0, zero_v, v), ones1], axis=1)],
            axis=0)
        nd_scr[slot, d] = _dot(s_w, vo_stack)

        kws = (k.astype(F32) * jnp.exp(rc + (bl - wm))).astype(BF16)
        u = _dot_tn(kws, jnp.concatenate([v, one_v], axis=1))
        for blk in range(4):
            r0 = (blk // 2) * ML_DQK
            c0 = (blk % 2) * 2 * ML_DV + (blk // 2) * ML_DV
            u_scr[slot, d, blk] = u[r0:r0 + ML_DQK, c0:c0 + ML_DV]

    def state(cc, d, slot):
        rows = pl.ds(pl.multiple_of(cc * CHUNK, CHUNK), CHUNK)
        cmx2 = _halves_to_tiles(cmx_scr[d, rows, :])
        bc2 = _halves_to_tiles(bcm_scr[d, rows, :])
        m_old = m2_scr[d]
        g = jnp.maximum(cmx2, m_old)
        w_loc = jnp.exp(cmx2 - g)
        w_int = jnp.exp(m_old - g)
        floor = jnp.exp(-(bc2 + g))

        blocks = [st_scr[d, blk] for blk in range(4)]
        zero = jnp.zeros((ML_DQK, ML_DV), BF16)
        cm0, n0, cm1, n1 = [x.astype(BF16) for x in blocks]
        st = jnp.concatenate([jnp.concatenate([cm0, zero, n0, zero], axis=1),
                              jnp.concatenate([zero, cm1, zero, n1], axis=1)], axis=0)
        sd = _dot(q_ref[0, rows, :], st)
        nd = nd_scr[slot, d]
        num = w_loc * nd[:, :2 * ML_DV] + w_int * sd[:, :2 * ML_DV]
        den = w_loc * nd[:, 2 * ML_DV:] + w_int * sd[:, 2 * ML_DV:]
        h_out = num / jnp.maximum(jnp.abs(den), floor)
        if d == 0:
            of_scr[rows, :] = h_out
        else:
            ob_scr[rows, :] = h_out

        bl = blast2_scr[d, pl.ds(cc, 1), :]
        wm = wsmax2_scr[d, pl.ds(cc, 1), :]
        m_new = jnp.maximum(bl + m_old, wm)
        dec = jnp.exp(bl + m_old - m_new)
        ws_scale = jnp.exp(wm - m_new)
        for blk in range(4):
            head = slice((blk // 2) * ML_DV, (blk // 2 + 1) * ML_DV)
            st_scr[d, blk] = blocks[blk] * dec[:, head] + u_scr[slot, d, blk] * ws_scale[:, head]
        m2_scr[d] = m_new

    _scan_pipeline(local, state)
    _scan_epilogue(of_scr, ob_scr, gate_ref, ng_ref, out_ref)


def _mlstm(q, k, v, gate, gcol, grow, ng):
    bsz = q.shape[0]
    pairs = ML_HEADS // 2
    return pl.pallas_call(
        _mlstm_kernel,
        out_shape=jax.ShapeDtypeStruct((bsz, S_ALL, ML_HEADS * ML_DV), BF16),
        grid=(bsz, pairs),
        in_specs=[
            pl.BlockSpec((1, S_ALL, 128), lambda b, p: (b, 0, p)),
            pl.BlockSpec((1, S_ALL, 128), lambda b, p: (b, 0, p)),
            pl.BlockSpec((1, S_ALL, 256), lambda b, p: (b, 0, p)),
            pl.BlockSpec((1, S_ALL, 256), lambda b, p: (b, 0, p)),
            pl.BlockSpec((1, 1, S_ALL, 8), lambda b, p: (b, p, 0, 0)),
            pl.BlockSpec((1, 2, 2, 1, N_CHUNKS, 128), lambda b, p: (b, 0, 0, p, 0, 0)),
            pl.BlockSpec((1, 1, 256), lambda b, p: (p, 0, 0)),
        ],
        out_specs=pl.BlockSpec((1, S_ALL, 256), lambda b, p: (b, 0, p)),
        scratch_shapes=[
            pltpu.VMEM((2, S_ALL, 128), F32),
            pltpu.VMEM((2, S_ALL, 128), F32),
            pltpu.VMEM((2, S_ALL, 128), F32),
            pltpu.VMEM((2, N_CHUNKS, 128), F32),
            pltpu.VMEM((2, N_CHUNKS, 128), F32),
            pltpu.VMEM((2, N_CHUNKS, 128), F32),
            pltpu.VMEM((2, N_CHUNKS, 2 * ML_DV), F32),
            pltpu.VMEM((2, N_CHUNKS, 2 * ML_DV), F32),
            pltpu.VMEM((S_ALL, 2 * ML_DV), F32),
            pltpu.VMEM((S_ALL, 2 * ML_DV), F32),
            pltpu.VMEM((2, 4, ML_DQK, ML_DV), F32),
            pltpu.VMEM((2, 1, 2 * ML_DV), F32),
            pltpu.VMEM((SCAN_SLOTS, 2, CHUNK, 4 * ML_DV), F32),
            pltpu.VMEM((SCAN_SLOTS, 2, 4, ML_DQK, ML_DV), F32),
        ],
        compiler_params=_params("parallel", "arbitrary"),
        name="mlstm_scan",
    )(q, k, v, gate, gcol, grow, ng)


def _rpb_table_kernel(rpb_ref, o_ref):
    h = pl.program_id(0)
    n_dc = 2 * NA_KW - 1
    qi = lax.broadcasted_iota(jnp.int32, (GRID_W, 128), 0)
    lane = lax.broadcasted_iota(jnp.int32, (GRID_W, 128), 1)
    wi = lane & (GRID_W - 1)
    second = lane >= GRID_W
    dc = jnp.clip(wi - qi + (NA_KW - 1), 0, n_dc - 1)
    cs = jnp.clip(qi - NA_KW // 2, 0, GRID_W - NA_KW)
    col_ok = (wi >= cs) & (wi < cs + NA_KW)
    for dr in range(2 * NA_KH - 2):
        base = h * ((2 * NA_KH - 1) * n_dc) + dr * n_dc
        acc = jnp.zeros((GRID_W, 128), F32)
        for c in range(n_dc):
            val = jnp.where(second, rpb_ref[base + n_dc + c], rpb_ref[base + c])
            acc = jnp.where(dc == c, val, acc)
        o_ref[0, dr] = jnp.where(col_ok, acc * LOG2E, NEG)


def _rpb_table(rpb):
    flat = rpb.reshape(-1)
    return pl.pallas_call(
        _rpb_table_kernel,
        out_shape=jax.ShapeDtypeStruct((NA_HEADS, 2 * NA_KH - 2, GRID_W, 128), F32),
        grid=(NA_HEADS,),
        in_specs=[pl.BlockSpec(memory_space=pltpu.SMEM)],
        out_specs=pl.BlockSpec((1, 2 * NA_KH - 2, GRID_W, 128), lambda h: (h, 0, 0, 0)),
        compiler_params=_params("arbitrary"),
        name="na_rpb_table",
    )(flat)


def _softmax_weights(s_parts, extra_logit=None):
    def lane_tiles(a):
        return [a[:, i:i + 128] for i in range(0, a.shape[1], 128)]

    m = functools.reduce(jnp.maximum, [t for s in s_parts for t in lane_tiles(s)]).max(axis=-1, keepdims=True)
    if extra_logit is not None:
        m = jnp.maximum(m, extra_logit)
    p_parts, p_tiles = [], []
    for s in s_parts:
        p = jnp.exp2(s - m)
        p_tiles += lane_tiles(p)
        p_parts.append(p.astype(BF16))
    l = functools.reduce(jnp.add, p_tiles).sum(axis=-1, keepdims=True)
    if extra_logit is not None:
        l = l + jnp.exp2(extra_logit - m)
    return p_parts, l


def _softmax_pv(s_parts, v_parts, extra_logit=None):
    p_parts, l = _softmax_weights(s_parts, extra_logit)
    o = None
    for p, v in zip(p_parts, v_parts):
        pv = _dot(p, v)
        o = pv if o is None else o + pv
    return o / l


def _stack_heads(x):
    head0 = lax.broadcasted_iota(jnp.int32, x.shape, 1) < 64
    zero = jnp.zeros_like(x)
    return jnp.concatenate([jnp.where(head0, x, zero), jnp.where(head0, zero, x)], axis=0)


def _merge_heads(o):
    n = o.shape[0] // 2
    head0 = lax.broadcasted_iota(jnp.int32, (n, o.shape[1]), 1) < 64
    return jnp.where(head0, o[:n], o[n:])


def _na_kernel(q_ref, k_ref, v_ref, gate_ref, tab_ref, out_ref, s_scr, sc_scr):
    kc = k_ref[0, 0:CTX_LEN, :]
    vc = v_ref[0, 0:CTX_LEN, :]
    n_lat = NA_KH * GRID_W
    rows_q = 2 * GRID_W
    n_groups = GRID_ROWS // NA_UNROLL

    def rows_of(r):
        rs = jnp.clip(r - NA_KH // 2, 0, GRID_ROWS - NA_KH)
        q_rows = pl.ds(pl.multiple_of(CTX_LEN + r * GRID_W, GRID_W), GRID_W)
        k_rows = pl.ds(pl.multiple_of(CTX_LEN + rs * GRID_W, GRID_W), n_lat)
        return rs, q_rows, k_rows

    def window_scores(r, slot):
        rs, q_rows, k_rows = rows_of(r)
        dr0 = rs - r + NA_KH - 1
        qs = _stack_heads(q_ref[0, q_rows, :])
        bias = jnp.concatenate(
            [jnp.concatenate([tab_ref[hh, dr0 + 2 * j] for j in range(NA_KH // 2)], axis=1)
             for hh in range(2)], axis=0)
        s_scr[slot] = _dot_nt(qs, k_ref[0, k_rows, :]) + bias

    def context_scores(g, slot):
        qs = jnp.concatenate([_stack_heads(q_ref[0, rows_of(g * NA_UNROLL + u)[1], :])
                              for u in range(NA_UNROLL)], axis=0)
        sc_scr[slot] = _dot_nt(qs, kc)

    context_scores(0, 0)
    for r in range(NA_LOOKAHEAD):
        window_scores(r, r)

    def body(i, carry):
        for g_slot in range(2):
            g = 2 * i + g_slot
            context_scores(jnp.minimum(g + 1, n_groups - 1), 1 - g_slot)
            for u in range(NA_UNROLL):
                r = g * NA_UNROLL + u
                window_scores(jnp.minimum(r + NA_LOOKAHEAD, GRID_ROWS - 1), (u + NA_LOOKAHEAD) % NA_UNROLL)
                _, q_rows, k_rows = rows_of(r)
                s_ctx = sc_scr[g_slot, u * rows_q:(u + 1) * rows_q, :]
                o = _merge_heads(_softmax_pv([s_scr[u], s_ctx], [v_ref[0, k_rows, :], vc]))
                out_ref[0, q_rows, :] = (o * _silu(gate_ref[0, q_rows, :].astype(F32))).astype(BF16)
        return carry

    lax.fori_loop(0, n_groups // 2, body, 0)

    for t in range(CTX_LEN // 128):
        rows = slice(t * 128, (t + 1) * 128)
        qs = _stack_heads(q_ref[0, rows, :])
        o = _merge_heads(_softmax_pv([_dot_nt(qs, kc)], [vc]))
        out_ref[0, rows, :] = (o * _silu(gate_ref[0, rows, :].astype(F32))).astype(BF16)


def _na(q, k, v, gate, table):
    bsz = q.shape[0]
    pairs = NA_HEADS // 2
    seq_spec = pl.BlockSpec((1, S_ALL, 128), lambda b, p: (b, 0, p))
    return pl.pallas_call(
        _na_kernel,
        out_shape=jax.ShapeDtypeStruct((bsz, S_ALL, NA_HEADS * NA_DH), BF16),
        grid=(bsz, pairs),
        in_specs=[seq_spec, seq_spec, seq_spec, seq_spec,
                  pl.BlockSpec((2, 2 * NA_KH - 2, GRID_W, 128), lambda b, p: (p, 0, 0, 0))],
        out_specs=seq_spec,
        scratch_shapes=[pltpu.VMEM((NA_UNROLL, 2 * GRID_W, NA_KH * GRID_W), F32),
                        pltpu.VMEM((2, NA_UNROLL * 2 * GRID_W, CTX_LEN), F32)],
        compiler_params=_params("parallel", "arbitrary"),
        name="na_attn",
    )(q, k, v, gate, table)


def _swa_mask_table():
    span = SW_BLOCK + 2 * SW_WINDOW
    r = np.arange(SW_BLOCK)[:, None]
    c = np.arange(span)[None, :]
    cases = [np.abs(r + off - c) <= SW_WINDOW for off in (0, SW_WINDOW, 2 * SW_WINDOW)]
    cases.append(np.zeros((SW_BLOCK, span), bool))
    return jnp.asarray(np.where(np.stack(cases), 0.0, NEG), F32)


def _swa_kernel(sink_ref, q_ref, k_ref, v_ref, gate_ref, mask_ref, out_ref, s_scr):
    n_blocks = S_ALL // SW_BLOCK
    ctx_blocks = CTX_LEN // SW_BLOCK
    n_tiles = SW_HEADS // 2
    span = SW_BLOCK + 2 * SW_WINDOW
    kc = k_ref[0, 0:CTX_LEN, :]
    vc = v_ref[0, 0:CTX_LEN, :]
    first = lax.broadcasted_iota(jnp.int32, (2 * SW_BLOCK, 1), 0) < SW_BLOCK

    def rows_of(n):
        qstart = (n - ctx_blocks) * SW_BLOCK
        kstart = jnp.clip(qstart - SW_WINDOW, 0, SEQ - span)
        q_rows = pl.ds(pl.multiple_of(n * SW_BLOCK, SW_BLOCK), SW_BLOCK)
        k_rows = pl.ds(pl.multiple_of(CTX_LEN + kstart, SW_BLOCK), span)
        return q_rows, k_rows

    def scores(n, t, slot):
        q_rows, k_rows = rows_of(n)
        case = jnp.where(n < ctx_blocks, 3,
                         jnp.where(n == ctx_blocks, 0, jnp.where(n == n_blocks - 1, 2, 1)))
        qs = _stack_heads(q_ref[0, q_rows, t * 128:(t + 1) * 128])
        mb = mask_ref[case]
        s_scr[slot, :, 0:span] = _dot_nt(qs, k_ref[0, k_rows, :]) + jnp.concatenate([mb, mb], axis=0)
        s_scr[slot, :, span:span + CTX_LEN] = _dot_nt(qs, kc)

    def finish(n, t, slot):
        q_rows, k_rows = rows_of(n)
        cols = slice(t * 128, (t + 1) * 128)
        sink = jnp.where(first, sink_ref[t], sink_ref[t + n_tiles]) * LOG2E
        o = _merge_heads(_softmax_pv([s_scr[slot, :, 0:span], s_scr[slot, :, span:span + CTX_LEN]],
                                     [v_ref[0, k_rows, :], vc], extra_logit=sink))
        out_ref[0, q_rows, cols] = (o * _silu(gate_ref[0, q_rows, cols].astype(F32))).astype(BF16)

    for t in range(SW_LOOKAHEAD):
        scores(0, t, t)

    def body(n, carry):
        for t in range(n_tiles):
            ahead = t + SW_LOOKAHEAD
            if ahead < n_tiles:
                scores(n, ahead, ahead)
            else:
                scores(jnp.minimum(n + 1, n_blocks - 1), ahead - n_tiles, ahead - n_tiles)
            finish(n, t, t)
        return carry

    lax.fori_loop(0, n_blocks, body, 0)


def _swa(q, k, v, gate, sink):
    bsz = q.shape[0]
    width = SW_HEADS * SW_DH
    span = SW_BLOCK + 2 * SW_WINDOW
    seq_spec = pl.BlockSpec((1, S_ALL, width), lambda b: (b, 0, 0))
    kv_spec = pl.BlockSpec((1, S_ALL, 128), lambda b: (b, 0, 0))
    return pl.pallas_call(
        _swa_kernel,
        out_shape=jax.ShapeDtypeStruct((bsz, S_ALL, width), BF16),
        grid=(bsz,),
        in_specs=[pl.BlockSpec(memory_space=pltpu.SMEM), seq_spec, kv_spec, kv_spec, seq_spec,
                  pl.BlockSpec((4, SW_BLOCK, span), lambda b: (0, 0, 0))],
        out_specs=seq_spec,
        scratch_shapes=[pltpu.VMEM((SW_HEADS // 2, 2 * SW_BLOCK, span + CTX_LEN), F32)],
        compiler_params=_params("parallel"),
        name="swa_attn",
    )(sink, q, k, v, gate, _swa_mask_table())


def _pad_cols(w, width):
    return jnp.pad(w, ((0, 0), (0, width - w.shape[1])))


def _prep_even(w_in, gate_w, gate_b):
    o = np.cumsum((0, 256, 256, 512, 512, 32, 512, 512, 512, 512))
    gq, gk, gv, gg, lra, nq, nk, nv, ng = [w_in[:, o[i]:o[i + 1]] for i in range(9)]
    w = jnp.concatenate([gq * GLA_DK ** -0.5, gk, gv, gg, nq * (NA_DH ** -0.5 * LOG2E), nk, nv, ng,
                         _pad_cols(lra, 128)], axis=1).astype(BF16)
    gws, gbs = [], []
    for p in range(GLA_HEADS // 2):
        cols = slice(p * 128, (p + 1) * 128)
        zero = jnp.zeros((GLA_RANK, 128), F32)
        gws.append(jnp.concatenate([jnp.concatenate([gate_w[0][:, cols], zero], axis=1),
                                    jnp.concatenate([zero, gate_w[1][:, cols]], axis=1)], axis=0))
        gbs.append(jnp.concatenate([gate_b[0][cols], gate_b[1][cols]])[None])
    return w, jnp.stack(gws).astype(BF16), jnp.stack(gbs)


def _swa_head_perm():
    order = []
    for t in range(SW_HEADS // 2):
        order += list(range(t * SW_DH, (t + 1) * SW_DH))
        order += list(range((t + SW_HEADS // 2) * SW_DH, (t + SW_HEADS // 2 + 1) * SW_DH))
    return np.asarray(order)


def _rope_swap(w):
    nf = SW_DH // 4
    idx = np.arange(w.shape[1]).reshape(-1, 2, nf)[:, ::-1, :].reshape(-1)
    return w[:, idx]


def _prep_odd(w_in, w_out):
    o = np.cumsum((0, 256, 256, 512, 512, 16, 512, 128, 128, 512))
    mq, mk, mv, mg, gates, sq, sk, sv, sg = [w_in[:, o[i]:o[i + 1]] for i in range(9)]
    perm = _swa_head_perm()
    sq = sq[:, perm] * (SW_DH ** -0.5 * LOG2E)
    w = jnp.concatenate([mq, mk * ML_DQK ** -0.5, mv, mg, sq, _rope_swap(sq), sk, _rope_swap(sk),
                         sv, _pad_cols(gates, 128), sg[:, perm]], axis=1).astype(BF16)
    half = ML_HEADS * ML_DV
    w_o = jnp.concatenate([w_out[:half], w_out[half:][perm]], axis=0).astype(BF16)
    return w, w_o


def _rope_tables():
    nf = SW_DH // 4
    freqs = ROPE_BASE ** (-jnp.arange(nf, dtype=F32) / nf)
    pos = jnp.arange(SEQ)
    rows = (pos // GRID_W).astype(F32)
    cols = (pos % GRID_W).astype(F32)
    ar = rows[:, None] * freqs[None, :]
    ac = cols[:, None] * freqs[None, :]
    cos = jnp.concatenate([jnp.cos(ar), jnp.cos(ar), jnp.cos(ac), jnp.cos(ac)], axis=1)
    sin = jnp.concatenate([-jnp.sin(ar), jnp.sin(ar), -jnp.sin(ac), jnp.sin(ac)], axis=1)
    cos = jnp.concatenate([jnp.ones((CTX_LEN, SW_DH), F32), cos], axis=0)
    sin = jnp.concatenate([jnp.zeros((CTX_LEN, SW_DH), F32), sin], axis=0)
    return jnp.tile(cos, (1, 2)), jnp.tile(sin, (1, 2))


def _gate_layouts(gates):
    bsz = gates.shape[0]
    g = gates.reshape(bsz, S_ALL, 2, 2, 2, 2)
    gcol = g.transpose(0, 4, 1, 2, 3, 5).reshape(bsz, 2, S_ALL, 8)
    g = gates.reshape(bsz, N_CHUNKS, CHUNK, 2, 2, 2, 2)
    grow = g.transpose(0, 3, 4, 5, 1, 6, 2).reshape(bsz, 2, 2, 2, N_CHUNKS, 2 * CHUNK)
    return gcol, grow


def kernel(x, c, ctx, c_ctx, w_ada, b_ada, ln_g, ln_b, w_in_even, w_out_even, gla_gate_w, gla_gate_b,
           gla_norm_g, na_rpb, w_in_odd, w_out_odd, ml_gate_b, ml_norm_g, sw_sink):
    bsz = x.shape[0]
    mod_rows = 16
    cvec = jnp.zeros((mod_rows, D_MODEL), F32).at[:bsz].set(c).at[bsz].set(c_ctx)
    ada = _ada(cvec, w_ada, b_ada)
    xs = (ctx, x)
    cos_t, sin_t = _rope_tables()
    for l in range(DEPTH):
        i = l // 2
        last = l == DEPTH - 1
        lat = ada[l, :bsz].reshape(bsz, 1, 3, D_MODEL)
        cx = jnp.broadcast_to(ada[l, bsz].reshape(1, 1, 3, D_MODEL), (bsz, 1, 3, D_MODEL))
        mod = jnp.concatenate([lat, cx], axis=1)
        if l % 2 == 0:
            w, gw, gb = _prep_even(w_in_even[i], gla_gate_w[i], gla_gate_b[i])
            gq, gk, gv, gg, nq, nk, nv, ng, lr = _inproj_even(xs, mod, w)
            ya = _gla(gq, gk, gv, gg, lr, gw, gb, gla_norm_g[i].reshape(GLA_HEADS // 2, 1, 256))
            yb = _na(nq, nk, nv, ng, _rpb_table(na_rpb[i]))
            w_o = w_out_even[i].astype(BF16)
        else:
            w, w_o = _prep_odd(w_in_odd[i], w_out_odd[i])
            mq, mk, mv, mg, sq, sk, sv, sg, gates = _inproj_odd(
                xs[0], mod, w, cos_t, sin_t, ml_gate_b[i].reshape(1, 4 * ML_HEADS))
            gcol, grow = _gate_layouts(gates)
            ya = _mlstm(mq, mk, mv, mg, gcol, grow, ml_norm_g[i].reshape(ML_HEADS // 2, 1, 256))
            yb = _swa(sq, sk, sv, sg, sw_sink[i])
        xs = (_outproj(ya, yb, xs, mod, w_o, ln_g[l].reshape(1, D_MODEL), ln_b[l].reshape(1, D_MODEL), last),)
    return xs[0]
```

```python
import functools
import math

import numpy as np
import jax
import jax.numpy as jnp
from jax import lax
from jax.experimental import pallas as pl
from jax.experimental.pallas import tpu as pltpu

F32 = jnp.float32
BF16 = jnp.bfloat16

D_MODEL = 1024
SEQ = 4096
DEPTH = 4
GRID_W = 64
CTX_LEN = 256
S_ALL = CTX_LEN + SEQ
ALPHA = (2.0 * DEPTH) ** 0.25
LN_EPS = 1e-5
NORM_EPS = 1e-6
NEG = -1e30
LOG2E = 1.4426950408889634
GLA_HEADS, GLA_DK, GLA_DV, GLA_RANK, GLA_TAU = 4, 64, 128, 16, 16.0
NA_HEADS, NA_DH, NA_KH, NA_KW = 8, 64, 8, 16
ML_HEADS, ML_DQK, ML_DV = 4, 64, 128
SW_HEADS, SW_KV_HEADS, SW_DH, SW_WINDOW, SW_BLOCK = 8, 2, 64, 128, 128
ROPE_BASE = 10000.0
CHUNK = 64
N_CHUNKS = S_ALL // CHUNK
CTX_CHUNKS = CTX_LEN // CHUNK
ROW_TILE = 256
N_ROW_TILES = S_ALL // ROW_TILE
GRID_ROWS = SEQ // GRID_W
OUT_SUB = 4
SCAN_SLOTS = 4
SCAN_LOOKAHEAD = 2
NA_UNROLL = 4
NA_LOOKAHEAD = 2
SW_LOOKAHEAD = 1

VMEM_LIMIT = 56 * 1024 * 1024

NT_DIMS = (((1,), (1,)), ((), ()))
TN_DIMS = (((0,), (0,)), ((), ()))


def _dot(a, b):
    return jnp.dot(a, b, preferred_element_type=F32)


def _dot_nt(a, b):
    return lax.dot_general(a, b, NT_DIMS, preferred_element_type=F32)


def _dot_tn(a, b):
    return lax.dot_general(a, b, TN_DIMS, preferred_element_type=F32)


def _silu(x):
    h = 0.5 * x
    return h + h * jnp.tanh(h)


def _log_sigmoid(x):
    return jnp.minimum(x, 0.0) - jnp.log(1.0 + jnp.exp(-jnp.abs(x)))


def _cumsum_rows(x, reverse):
    n = x.shape[0]
    row = lax.broadcasted_iota(jnp.int32, x.shape, 0)
    k = 1
    while k < n:
        if reverse:
            x = x + jnp.where(row < n - k, pltpu.roll(x, n - k, axis=0), 0.0)
        else:
            x = x + jnp.where(row >= k, pltpu.roll(x, k, axis=0), 0.0)
        k *= 2
    return x


def _split_dot(mat_bf16, x):
    hi = x.astype(BF16)
    lo = (x - hi.astype(F32)).astype(BF16)
    return _dot(mat_bf16, hi) + _dot(mat_bf16, lo)


def _split_dot_r(x, mat_bf16):
    hi = x.astype(BF16)
    lo = (x - hi.astype(F32)).astype(BF16)
    return _dot(hi, mat_bf16) + _dot(lo, mat_bf16)


def _halves_to_tiles(x):
    half0 = lax.broadcasted_iota(jnp.int32, x.shape, 1) < 64
    sw = pltpu.roll(x, 64, axis=1)
    return jnp.concatenate([jnp.where(half0, x, sw), jnp.where(half0, sw, x)], axis=1)


def _params(*sem):
    return pltpu.CompilerParams(dimension_semantics=sem, vmem_limit_bytes=VMEM_LIMIT)


def _ada_kernel(c_ref, w_ref, b_ref, o_ref):
    cond = _silu(c_ref[...])
    o_ref[0] = jnp.dot(cond, w_ref[0], precision=lax.Precision.HIGHEST,
                       preferred_element_type=F32) + b_ref[0]


def _ada(cvec, w_ada, b_ada):
    rows = cvec.shape[0]
    nblk = 3
    return pl.pallas_call(
        _ada_kernel,
        out_shape=jax.ShapeDtypeStruct((DEPTH, rows, 3 * D_MODEL), F32),
        grid=(DEPTH, nblk),
        in_specs=[
            pl.BlockSpec((rows, D_MODEL), lambda l, n: (0, 0)),
            pl.BlockSpec((1, D_MODEL, D_MODEL), lambda l, n: (l, 0, n)),
            pl.BlockSpec((1, 1, D_MODEL), lambda l, n: (l, 0, n)),
        ],
        out_specs=pl.BlockSpec((1, rows, D_MODEL), lambda l, n: (l, 0, n)),
        compiler_params=_params("arbitrary", "arbitrary"),
        name="ada_mod",
    )(cvec, w_ada, b_ada.reshape(DEPTH, 1, 3 * D_MODEL))


def _row_tile(x_refs, is_ctx):
    if len(x_refs) == 1:
        return x_refs[0][0]
    return jnp.where(is_ctx, x_refs[0][0], x_refs[1][0])


def _x_specs(xs, tile_of):
    def combined(*ids):
        b, j = tile_of(*ids)
        return (b, j, 0)

    def context(*ids):
        return (tile_of(*ids)[0], 0, 0)

    def latent(*ids):
        b, j = tile_of(*ids)
        return (b, jnp.maximum(j - 1, 0), 0)

    if len(xs) == 1:
        return [pl.BlockSpec((1, ROW_TILE, D_MODEL), combined)]
    return [pl.BlockSpec((1, ROW_TILE, D_MODEL), context), pl.BlockSpec((1, ROW_TILE, D_MODEL), latent)]


def _modulated(x_refs, mod_ref):
    shift = mod_ref[0, 0, 0:1, :]
    scale = mod_ref[0, 0, 1:2, :]
    x = _row_tile(x_refs, pl.program_id(1) == 0)
    return (x * (1.0 + scale) + shift).astype(BF16)


EVEN_SLABS = (256, 256, 512, 512, 512, 512, 512, 512)


def _inproj_even_kernel(*refs, n_x):
    x_refs, (mod_ref, w_ref), outs = refs[:n_x], refs[n_x:n_x + 2], refs[n_x + 2:]
    h = _modulated(x_refs, mod_ref)
    off = 0
    for ref, width in zip(outs[:-1], EVEN_SLABS):
        ref[0] = _dot(h, w_ref[:, off:off + width]).astype(ref.dtype)
        off += width
    lr = _dot(h, w_ref[:, off:off + 128])
    outs[-1][0] = lr[:, :2 * GLA_RANK]


ODD_SLABS = (256, 256, 512, 512)


def _inproj_odd_kernel(x_ref, mod_ref, w_ref, cos_ref, sin_ref, gb_ref,
                       mq, mk, mv, mg, sq, sk, sv, sg, gcol, grow_t):
    h = _modulated((x_ref,), mod_ref)
    off = 0
    for ref, width in zip((mq, mk, mv, mg), ODD_SLABS):
        ref[0] = _dot(h, w_ref[:, off:off + width]).astype(ref.dtype)
        off += width
    cos = cos_ref[...]
    sin = sin_ref[...]
    a = _dot(h, w_ref[:, off:off + 512])
    b = _dot(h, w_ref[:, off + 512:off + 1024])
    for t in range(4):
        cols = slice(t * 128, (t + 1) * 128)
        sq[0, :, cols] = (a[:, cols] * cos + b[:, cols] * sin).astype(BF16)
    off += 1024
    kvg = _dot(h, w_ref[:, off:off + 512])
    sk[0] = (kvg[:, 0:128] * cos + kvg[:, 128:256] * sin).astype(BF16)
    sv[0] = kvg[:, 256:384].astype(BF16)
    g = kvg[:, 384:512] + gb_ref[...]
    for p in range(ML_HEADS // 2):
        gcol[0, p] = g[:, p * 8:(p + 1) * 8]
    grow_t[0] = g.T[0:4 * ML_HEADS, :]
    off += 512
    sg[0] = _dot(h, w_ref[:, off:off + 512]).astype(BF16)


def _mod_index(b, j):
    return (b, jnp.where(j == 0, 1, 0), 0, 0)


def _inproj_even(xs, mod, w):
    bsz = xs[0].shape[0]
    ntot = w.shape[1]
    out_shape = [jax.ShapeDtypeStruct((bsz, S_ALL, wd), BF16) for wd in EVEN_SLABS]
    out_shape.append(jax.ShapeDtypeStruct((bsz, S_ALL, 2 * GLA_RANK), F32))
    out_specs = [pl.BlockSpec((1, ROW_TILE, wd), lambda b, j: (b, j, 0)) for wd in EVEN_SLABS]
    out_specs.append(pl.BlockSpec((1, ROW_TILE, 2 * GLA_RANK), lambda b, j: (b, j, 0)))
    return pl.pallas_call(
        functools.partial(_inproj_even_kernel, n_x=len(xs)),
        out_shape=out_shape,
        grid=(bsz, N_ROW_TILES),
        in_specs=_x_specs(xs, lambda b, j: (b, j)) + [
            pl.BlockSpec((1, 1, 3, D_MODEL), _mod_index),
            pl.BlockSpec((D_MODEL, ntot), lambda b, j: (0, 0)),
        ],
        out_specs=out_specs,
        compiler_params=_params("parallel", "arbitrary"),
        name="inproj_even",
    )(*xs, mod, w)


def _inproj_odd(xc, mod, w, cos_t, sin_t, gate_b):
    bsz = xc.shape[0]
    ntot = w.shape[1]
    widths = (256, 256, 512, 512, 512, 128, 128, 512)
    out_shape = [jax.ShapeDtypeStruct((bsz, S_ALL, wd), BF16) for wd in widths]
    out_shape.append(jax.ShapeDtypeStruct((bsz, ML_HEADS // 2, S_ALL, 8), F32))
    out_shape.append(jax.ShapeDtypeStruct((bsz, 4 * ML_HEADS, S_ALL), F32))
    out_specs = [pl.BlockSpec((1, ROW_TILE, wd), lambda b, j: (b, j, 0)) for wd in widths]
    out_specs.append(pl.BlockSpec((1, ML_HEADS // 2, ROW_TILE, 8), lambda b, j: (b, 0, j, 0)))
    out_specs.append(pl.BlockSpec((1, 4 * ML_HEADS, ROW_TILE), lambda b, j: (b, 0, j)))
    return pl.pallas_call(
        _inproj_odd_kernel,
        out_shape=out_shape,
        grid=(bsz, N_ROW_TILES),
        in_specs=[
            pl.BlockSpec((1, ROW_TILE, D_MODEL), lambda b, j: (b, j, 0)),
            pl.BlockSpec((1, 1, 3, D_MODEL), _mod_index),
            pl.BlockSpec((D_MODEL, ntot), lambda b, j: (0, 0)),
            pl.BlockSpec((ROW_TILE, 128), lambda b, j: (j, 0)),
            pl.BlockSpec((ROW_TILE, 128), lambda b, j: (j, 0)),
            pl.BlockSpec((1, 128), lambda b, j: (0, 0)),
        ],
        out_specs=out_specs,
        compiler_params=_params("parallel", "arbitrary"),
        name="inproj_odd",
    )(xc, mod, w, cos_t, sin_t, gate_b)


def _outproj_kernel(*refs, n_x, n_tiles, n_sub):
    ya_refs, yb_refs = refs[:n_sub], refs[n_sub:2 * n_sub]
    x_refs = refs[2 * n_sub:2 * n_sub + n_sub * n_x]
    mod_refs = refs[2 * n_sub + n_sub * n_x:3 * n_sub + n_sub * n_x]
    w_ref, g_ref, b_ref = refs[3 * n_sub + n_sub * n_x:3 * n_sub + n_sub * n_x + 3]
    o_ref, r_scr = refs[-2:]
    t = pl.program_id(0)

    @pl.when(t == 0)
    def _():
        r_scr[...] = jnp.zeros(r_scr.shape, F32)

    for s in range(n_sub):
        r = r_scr[s]
        mu = jnp.mean(r, axis=-1, keepdims=True)
        d = r - mu
        var = jnp.mean(d * d, axis=-1, keepdims=True)
        o_ref[s * ROW_TILE:(s + 1) * ROW_TILE, :] = d * lax.rsqrt(var + LN_EPS) * g_ref[...] + b_ref[...]

    half = ya_refs[0].shape[2]
    ya = jnp.concatenate([ref[0] for ref in ya_refs], axis=0)
    yb = jnp.concatenate([ref[0] for ref in yb_refs], axis=0)
    y = _dot(ya, w_ref[0:half, :]) + _dot(yb, w_ref[half:2 * half, :])
    for s in range(n_sub):
        x = _row_tile(x_refs[s * n_x:(s + 1) * n_x], lax.rem(n_sub * t + s, n_tiles) == 0)
        r_scr[s] = ALPHA * x + mod_refs[s][0, 0, 2:3, :] * y[s * ROW_TILE:(s + 1) * ROW_TILE]


def _outproj(ya, yb, xs, mod, w, ln_g, ln_b, latent_only):
    bsz = xs[0].shape[0]
    skip = 1 if latent_only else 0
    n_tiles = N_ROW_TILES - skip
    total = bsz * n_tiles
    n_sub = math.gcd(total, OUT_SUB)
    half = ya.shape[2]

    def tile_of(s):
        def fn(t):
            tt = jnp.minimum(n_sub * t + s, total - 1)
            return tt // n_tiles, lax.rem(tt, n_tiles) + skip
        return fn

    def y_index(s):
        def fn(t):
            b, j = tile_of(s)(t)
            return (b, j, 0)
        return fn

    def mod_index(s):
        return lambda t: _mod_index(*tile_of(s)(t))

    subs = range(n_sub)
    const = lambda t: (0, 0)
    x_specs = [spec for s in subs for spec in _x_specs(xs, tile_of(s))]
    out = pl.pallas_call(
        functools.partial(_outproj_kernel, n_x=len(xs), n_tiles=n_tiles, n_sub=n_sub),
        out_shape=jax.ShapeDtypeStruct((total * ROW_TILE, D_MODEL), F32),
        grid=(total // n_sub + 1,),
        in_specs=[pl.BlockSpec((1, ROW_TILE, half), y_index(s)) for s in subs]
        + [pl.BlockSpec((1, ROW_TILE, half), y_index(s)) for s in subs]
        + x_specs
        + [pl.BlockSpec((1, 1, 3, D_MODEL), mod_index(s)) for s in subs]
        + [pl.BlockSpec((2 * half, D_MODEL), const),
           pl.BlockSpec((1, D_MODEL), const),
           pl.BlockSpec((1, D_MODEL), const)],
        out_specs=pl.BlockSpec((n_sub * ROW_TILE, D_MODEL), lambda t: (jnp.maximum(t - 1, 0), 0)),
        scratch_shapes=[pltpu.VMEM((n_sub, ROW_TILE, D_MODEL), F32)],
        compiler_params=_params("arbitrary"),
        name="outproj_ln",
    )(*([ya] * n_sub), *([yb] * n_sub), *(list(xs) * n_sub), *([mod] * n_sub), w, ln_g, ln_b)
    return out.reshape(bsz, n_tiles * ROW_TILE, D_MODEL)


def _scan_epilogue(of_scr, ob_scr, gate_ref, ng_ref, out_ref):
    def body(t, carry):
        rows = pl.ds(pl.multiple_of(t * ROW_TILE, ROW_TILE), ROW_TILE)
        o = of_scr[rows, :] + ob_scr[rows, :]
        parts = []
        for hh in range(2):
            oh = o[:, hh * 128:(hh + 1) * 128]
            ms = jnp.mean(oh * oh, axis=-1, keepdims=True)
            parts.append(oh * lax.rsqrt(ms + NORM_EPS))
        y = jnp.concatenate(parts, axis=1) * ng_ref[0]
        out_ref[0, rows, :] = (y * _silu(gate_ref[0, rows, :].astype(F32))).astype(BF16)
        return carry

    lax.fori_loop(0, N_ROW_TILES, body, 0)


def _bwd_chunk(i):
    return jnp.where(i < CTX_CHUNKS, CTX_CHUNKS - 1 - i, N_CHUNKS - 1 + CTX_CHUNKS - i)


def _scan_pipeline(local_fn, state_fn):
    for i in range(SCAN_LOOKAHEAD):
        local_fn(i, 0, i)
        local_fn(_bwd_chunk(i), 1, i)

    def body(j, carry):
        for u in range(SCAN_SLOTS):
            i = SCAN_SLOTS * j + u
            nxt = jnp.minimum(i + SCAN_LOOKAHEAD, N_CHUNKS - 1)
            local_fn(nxt, 0, (u + SCAN_LOOKAHEAD) % SCAN_SLOTS)
            local_fn(_bwd_chunk(nxt), 1, (u + SCAN_LOOKAHEAD) % SCAN_SLOTS)
            state_fn(i, 0, u)
            state_fn(_bwd_chunk(i), 1, u)
        return carry

    lax.fori_loop(0, N_CHUNKS // SCAN_SLOTS, body, 0)


def _gla_kernel(q_ref, k_ref, v_ref, gate_ref, lr_ref, gw_ref, gb_ref, ng_ref, out_ref,
                g_scr, of_scr, ob_scr, st_scr, a_scr, qin_scr, u_scr, dl_scr):
    def pre(t, carry):
        rows = pl.ds(pl.multiple_of(t * ROW_TILE, ROW_TILE), ROW_TILE)
        z = _dot(lr_ref[0, rows, :].astype(BF16), gw_ref[0]) + gb_ref[0]
        g_scr[rows, :] = _log_sigmoid(z) * (1.0 / GLA_TAU)
        return carry

    lax.fori_loop(0, N_ROW_TILES, pre, 0)
    st_scr[...] = jnp.zeros(st_scr.shape, F32)

    def local(cc, d, slot):
        rows = pl.ds(pl.multiple_of(cc * CHUNK, CHUNK), CHUNK)
        g = g_scr[rows, d * 128:(d + 1) * 128]
        b = _cumsum_rows(g, reverse=(d == 1))
        b_last = b[0:1, :] if d == 1 else b[CHUNK - 1:CHUNK, :]
        q = q_ref[0, rows, :].astype(F32)
        k = k_ref[0, rows, :].astype(F32)
        q_in = q * jnp.exp(b)
        k_in = (k * jnp.exp(-b)).astype(BF16)
        k_dec = (k * jnp.exp(b_last - b)).astype(BF16)

        head0 = lax.broadcasted_iota(jnp.int32, (CHUNK, 128), 1) < GLA_DK
        q_stack = jnp.concatenate([jnp.where(head0, q_in, 0.0), jnp.where(head0, 0.0, q_in)],
                                  axis=0).astype(BF16)
        a = _dot_nt(q_stack, k_in)
        t_i = lax.broadcasted_iota(jnp.int32, (2 * CHUNK, CHUNK), 0) & (CHUNK - 1)
        s_i = lax.broadcasted_iota(jnp.int32, (2 * CHUNK, CHUNK), 1)
        keep = (s_i >= t_i) if d == 1 else (s_i <= t_i)
        a_scr[slot, d] = jnp.where(keep, a, 0.0).astype(BF16)
        qin_scr[slot, d] = q_in.astype(BF16)

        u = _dot_tn(v_ref[0, rows, :], k_dec)
        r_i = lax.broadcasted_iota(jnp.int32, u.shape, 0) < GLA_DV
        c_i = lax.broadcasted_iota(jnp.int32, u.shape, 1) < GLA_DK
        u_scr[slot, d] = jnp.where(r_i == c_i, u, 0.0)
        dl_scr[slot, d] = jnp.exp(b_last)

    def state(cc, d, slot):
        rows = pl.ds(pl.multiple_of(cc * CHUNK, CHUNK), CHUNK)
        v = v_ref[0, rows, :]
        a = a_scr[slot, d]
        vhead0 = lax.broadcasted_iota(jnp.int32, (CHUNK, 2 * GLA_DV), 1) < GLA_DV
        zero = jnp.zeros_like(v)
        st = st_scr[d]
        o = (_dot(a[:CHUNK], jnp.where(vhead0, v, zero))
             + _dot(a[CHUNK:], jnp.where(vhead0, zero, v))
             + _dot_nt(qin_scr[slot, d], st.astype(BF16)))
        if d == 0:
            of_scr[rows, :] = o
        else:
            ob_scr[rows, :] = o
        st_scr[d] = st * dl_scr[slot, d] + u_scr[slot, d]

    _scan_pipeline(local, state)
    _scan_epilogue(of_scr, ob_scr, gate_ref, ng_ref, out_ref)


def _gla(q, k, v, gate, lr, gw, gb, ng):
    bsz = q.shape[0]
    pairs = GLA_HEADS // 2
    return pl.pallas_call(
        _gla_kernel,
        out_shape=jax.ShapeDtypeStruct((bsz, S_ALL, GLA_HEADS * GLA_DV), BF16),
        grid=(bsz, pairs),
        in_specs=[
            pl.BlockSpec((1, S_ALL, 128), lambda b, p: (b, 0, p)),
            pl.BlockSpec((1, S_ALL, 128), lambda b, p: (b, 0, p)),
            pl.BlockSpec((1, S_ALL, 256), lambda b, p: (b, 0, p)),
            pl.BlockSpec((1, S_ALL, 256), lambda b, p: (b, 0, p)),
            pl.BlockSpec((1, S_ALL, 2 * GLA_RANK), lambda b, p: (b, 0, 0)),
            pl.BlockSpec((1, 2 * GLA_RANK, 256), lambda b, p: (p, 0, 0)),
            pl.BlockSpec((1, 1, 256), lambda b, p: (p, 0, 0)),
            pl.BlockSpec((1, 1, 256), lambda b, p: (p, 0, 0)),
        ],
        out_specs=pl.BlockSpec((1, S_ALL, 256), lambda b, p: (b, 0, p)),
        scratch_shapes=[
            pltpu.VMEM((S_ALL, 256), F32),
            pltpu.VMEM((S_ALL, 256), F32),
            pltpu.VMEM((S_ALL, 256), F32),
            pltpu.VMEM((2, 2 * GLA_DV, 2 * GLA_DK), F32),
            pltpu.VMEM((SCAN_SLOTS, 2, 2 * CHUNK, CHUNK), BF16),
            pltpu.VMEM((SCAN_SLOTS, 2, CHUNK, 2 * GLA_DK), BF16),
            pltpu.VMEM((SCAN_SLOTS, 2, 2 * GLA_DV, 2 * GLA_DK), F32),
            pltpu.VMEM((SCAN_SLOTS, 2, 1, 2 * GLA_DK), F32),
        ],
        compiler_params=_params("parallel", "arbitrary"),
        name="gla_scan",
    )(q, k, v, gate, lr, gw, gb, ng)


def _cummax_rows_chunked(x, reverse):
    n = x.shape[0]
    pos = lax.broadcasted_iota(jnp.int32, x.shape, 0) & (CHUNK - 1)
    k = 1
    while k < CHUNK:
        if reverse:
            x = jnp.maximum(x, jnp.where(pos < CHUNK - k, pltpu.roll(x, n - k, axis=0), NEG))
        else:
            x = jnp.maximum(x, jnp.where(pos >= k, pltpu.roll(x, k, axis=0), NEG))
        k *= 2
    return x


def _mlstm_kernel(q_ref, k_ref, v_ref, gate_ref, gcol_ref, grow_ref, ng_ref, out_ref,
                  bcm_scr, rcm_scr, cmx_scr, rrow_scr, blast_scr, wsmax_scr, blast2_scr, wsmax2_scr,
                  of_scr, ob_scr, st_scr, m2_scr, nd_scr, u_scr):
    lane128 = lax.broadcasted_iota(jnp.int32, (1, 128), 1)
    half0_row = lane128 < CHUNK

    li = lax.broadcasted_iota(jnp.int32, (128, 128), 0)
    lj = lax.broadcasted_iota(jnp.int32, (128, 128), 1)
    same_half = (li < CHUNK) == (lj < CHUNK)
    ones_half = jnp.where(same_half, 1.0, 0.0).astype(BF16)
    for d in range(2):
        within = (li >= lj) if d == 1 else (li <= lj)
        tri = jnp.where(same_half & within, 1.0, 0.0).astype(BF16)
        lf = _log_sigmoid(grow_ref[0, d, 1, 0])
        b_row = _split_dot_r(lf, tri)
        total = _split_dot_r(lf, ones_half)
        r_row = grow_ref[0, d, 0, 0] - b_row
        rrow_scr[d] = r_row
        blast_scr[d] = total
        mx0 = jnp.max(jnp.where(half0_row, r_row, NEG), axis=-1, keepdims=True)
        mx1 = jnp.max(jnp.where(half0_row, NEG, r_row), axis=-1, keepdims=True)
        wsmax = total + jnp.where(half0_row, mx0, mx1)
        wsmax_scr[d] = wsmax
        blast2_scr[d] = _halves_to_tiles(total)
        wsmax2_scr[d] = _halves_to_tiles(wsmax)

    ti = lax.broadcasted_iota(jnp.int32, (ROW_TILE, ROW_TILE), 0)
    tj = lax.broadcasted_iota(jnp.int32, (ROW_TILE, ROW_TILE), 1)
    same_chunk = (ti >> 6) == (tj >> 6)
    tril = jnp.where(same_chunk & (tj <= ti), 1.0, 0.0).astype(BF16)
    triu = jnp.where(same_chunk & (tj >= ti), 1.0, 0.0).astype(BF16)
    half0_tile = lax.broadcasted_iota(jnp.int32, (ROW_TILE, 128), 1) < CHUNK

    def pre(t, carry):
        rows = pl.ds(pl.multiple_of(t * ROW_TILE, ROW_TILE), ROW_TILE)
        gc = gcol_ref[0, 0, rows, :]
        lf = _log_sigmoid(gc)
        for d in range(2):
            bcol = _split_dot(triu if d == 1 else tril, lf)
            b0 = jnp.broadcast_to(bcol[:, d * 4 + 2:d * 4 + 3], (ROW_TILE, 128))
            b1 = jnp.broadcast_to(bcol[:, d * 4 + 3:d * 4 + 4], (ROW_TILE, 128))
            i0 = jnp.broadcast_to(gc[:, d * 4:d * 4 + 1], (ROW_TILE, 128))
            i1 = jnp.broadcast_to(gc[:, d * 4 + 1:d * 4 + 2], (ROW_TILE, 128))
            bc = jnp.where(half0_tile, b0, b1)
            rc = jnp.where(half0_tile, i0, i1) - bc
            bcm_scr[d, rows, :] = bc
            rcm_scr[d, rows, :] = rc
            cmx_scr[d, rows, :] = _cummax_rows_chunked(rc, reverse=(d == 1))
        return carry

    lax.fori_loop(0, N_ROW_TILES, pre, 0)
    st_scr[...] = jnp.zeros(st_scr.shape, F32)
    m2_scr[...] = jnp.zeros(m2_scr.shape, F32)

    def local(cc, d, slot):
        rows = pl.ds(pl.multiple_of(cc * CHUNK, CHUNK), CHUNK)
        rc = rcm_scr[d, rows, :]
        cmx = cmx_scr[d, rows, :]
        rr = rrow_scr[d, pl.ds(cc, 1), :]
        bl = blast_scr[d, pl.ds(cc, 1), :]
        wm = wsmax_scr[d, pl.ds(cc, 1), :]
        q = q_ref[0, rows, :]
        k = k_ref[0, rows, :]
        v = v_ref[0, rows, :]

        half0 = lax.broadcasted_iota(jnp.int32, (CHUNK, 128), 1) < CHUNK
        t_i = lax.broadcasted_iota(jnp.int32, (CHUNK, 128), 0)
        s_i = lax.broadcasted_iota(jnp.int32, (CHUNK, 128), 1) & (CHUNK - 1)
        keep = (s_i >= t_i) if d == 1 else (s_i <= t_i)
        w = jnp.exp(jnp.where(keep, rr - cmx, NEG))

        zero_k = jnp.zeros_like(k)
        k_stack = jnp.concatenate([jnp.where(half0, k, zero_k), jnp.where(half0, zero_k, k)], axis=0)
        s_w = (_dot_nt(q, k_stack) * w).astype(BF16)

        vhead0 = lax.broadcasted_iota(jnp.int32, (CHUNK, 2 * ML_DV), 1) < ML_DV
        zero_v = jnp.zeros_like(v)
        one_v = jnp.ones_like(v)
        ones0 = jnp.where(vhead0, 1.0, 0.0).astype(BF16)
        ones1 = jnp.where(vhead0, 0.0, 1.0).astype(BF16)
        vo_stack = jnp.concatenate(
            [jnp.concatenate([jnp.where(vhead0, v, zero_v), ones0], axis=1),
             jnp.concatenate([jnp.where(vhead0, zero_v, v), ones1], axis=1)],
            axis=0)
        nd_scr[slot, d] = _dot(s_w, vo_stack)

        kws = (k.astype(F32) * jnp.exp(rc + (bl - wm))).astype(BF16)
        u = _dot_tn(kws, jnp.concatenate([v, one_v], axis=1))
        for blk in range(4):
            r0 = (blk // 2) * ML_DQK
            c0 = (blk % 2) * 2 * ML_DV + (blk // 2) * ML_DV
            u_scr[slot, d, blk] = u[r0:r0 + ML_DQK, c0:c0 + ML_DV]

    def state(cc, d, slot):
        rows = pl.ds(pl.multiple_of(cc * CHUNK, CHUNK), CHUNK)
        cmx2 = _halves_to_tiles(cmx_scr[d, rows, :])
        bc2 = _halves_to_tiles(bcm_scr[d, rows, :])
        m_old = m2_scr[d]
        g = jnp.maximum(cmx2, m_old)
        w_loc = jnp.exp(cmx2 - g)
        w_int = jnp.exp(m_old - g)
        floor = jnp.exp(-(bc2 + g))

        blocks = [st_scr[d, blk] for blk in range(4)]
        zero = jnp.zeros((ML_DQK, ML_DV), BF16)
        cm0, n0, cm1, n1 = [x.astype(BF16) for x in blocks]
        st = jnp.concatenate([jnp.concatenate([cm0, zero, n0, zero], axis=1),
                              jnp.concatenate([zero, cm1, zero, n1], axis=1)], axis=0)
        sd = _dot(q_ref[0, rows, :], st)
        nd = nd_scr[slot, d]
        num = w_loc * nd[:, :2 * ML_DV] + w_int * sd[:, :2 * ML_DV]
        den = w_loc * nd[:, 2 * ML_DV:] + w_int * sd[:, 2 * ML_DV:]
        h_out = num / jnp.maximum(jnp.abs(den), floor)
        if d == 0:
            of_scr[rows, :] = h_out
        else:
            ob_scr[rows, :] = h_out

        bl = blast2_scr[d, pl.ds(cc, 1), :]
        wm = wsmax2_scr[d, pl.ds(cc, 1), :]
        m_new = jnp.maximum(bl + m_old, wm)
        dec = jnp.exp(bl + m_old - m_new)
        ws_scale = jnp.exp(wm - m_new)
        for blk in range(4):
            head = slice((blk // 2) * ML_DV, (blk // 2 + 1) * ML_DV)
            st_scr[d, blk] = blocks[blk] * dec[:, head] + u_scr[slot, d, blk] * ws_scale[:, head]
        m2_scr[d] = m_new

    _scan_pipeline(local, state)
    _scan_epilogue(of_scr, ob_scr, gate_ref, ng_ref, out_ref)


def _mlstm(q, k, v, gate, gcol, grow, ng):
    bsz = q.shape[0]
    pairs = ML_HEADS // 2
    return pl.pallas_call(
        _mlstm_kernel,
        out_shape=jax.ShapeDtypeStruct((bsz, S_ALL, ML_HEADS * ML_DV), BF16),
        grid=(bsz, pairs),
        in_specs=[
            pl.BlockSpec((1, S_ALL, 128), lambda b, p: (b, 0, p)),
            pl.BlockSpec((1, S_ALL, 128), lambda b, p: (b, 0, p)),
            pl.BlockSpec((1, S_ALL, 256), lambda b, p: (b, 0, p)),
            pl.BlockSpec((1, S_ALL, 256), lambda b, p: (b, 0, p)),
            pl.BlockSpec((1, 1, S_ALL, 8), lambda b, p: (b, p, 0, 0)),
            pl.BlockSpec((1, 2, 2, 1, N_CHUNKS, 128), lambda b, p: (b, 0, 0, p, 0, 0)),
            pl.BlockSpec((1, 1, 256), lambda b, p: (p, 0, 0)),
        ],
        out_specs=pl.BlockSpec((1, S_ALL, 256), lambda b, p: (b, 0, p)),
        scratch_shapes=[
            pltpu.VMEM((2, S_ALL, 128), F32),
            pltpu.VMEM((2, S_ALL, 128), F32),
            pltpu.VMEM((2, S_ALL, 128), F32),
            pltpu.VMEM((2, N_CHUNKS, 128), F32),
            pltpu.VMEM((2, N_CHUNKS, 128), F32),
            pltpu.VMEM((2, N_CHUNKS, 128), F32),
            pltpu.VMEM((2, N_CHUNKS, 2 * ML_DV), F32),
            pltpu.VMEM((2, N_CHUNKS, 2 * ML_DV), F32),
            pltpu.VMEM((S_ALL, 2 * ML_DV), F32),
            pltpu.VMEM((S_ALL, 2 * ML_DV), F32),
            pltpu.VMEM((2, 4, ML_DQK, ML_DV), F32),
            pltpu.VMEM((2, 1, 2 * ML_DV), F32),
            pltpu.VMEM((SCAN_SLOTS, 2, CHUNK, 4 * ML_DV), F32),
            pltpu.VMEM((SCAN_SLOTS, 2, 4, ML_DQK, ML_DV), F32),
        ],
        compiler_params=_params("parallel", "arbitrary"),
        name="mlstm_scan",
    )(q, k, v, gate, gcol, grow, ng)


def _rpb_table_kernel(rpb_ref, o_ref):
    h = pl.program_id(0)
    n_dc = 2 * NA_KW - 1
    qi = lax.broadcasted_iota(jnp.int32, (GRID_W, 128), 0)
    lane = lax.broadcasted_iota(jnp.int32, (GRID_W, 128), 1)
    wi = lane & (GRID_W - 1)
    second = lane >= GRID_W
    dc = jnp.clip(wi - qi + (NA_KW - 1), 0, n_dc - 1)
    cs = jnp.clip(qi - NA_KW // 2, 0, GRID_W - NA_KW)
    col_ok = (wi >= cs) & (wi < cs + NA_KW)
    for dr in range(2 * NA_KH - 2):
        base = h * ((2 * NA_KH - 1) * n_dc) + dr * n_dc
        acc = jnp.zeros((GRID_W, 128), F32)
        for c in range(n_dc):
            val = jnp.where(second, rpb_ref[base + n_dc + c], rpb_ref[base + c])
            acc = jnp.where(dc == c, val, acc)
        o_ref[0, dr] = jnp.where(col_ok, acc * LOG2E, NEG)


def _rpb_table(rpb):
    flat = rpb.reshape(-1)
    return pl.pallas_call(
        _rpb_table_kernel,
        out_shape=jax.ShapeDtypeStruct((NA_HEADS, 2 * NA_KH - 2, GRID_W, 128), F32),
        grid=(NA_HEADS,),
        in_specs=[pl.BlockSpec(memory_space=pltpu.SMEM)],
        out_specs=pl.BlockSpec((1, 2 * NA_KH - 2, GRID_W, 128), lambda h: (h, 0, 0, 0)),
        compiler_params=_params("arbitrary"),
        name="na_rpb_table",
    )(flat)


def _softmax_weights(s_parts, extra_logit=None):
    def lane_tiles(a):
        return [a[:, i:i + 128] for i in range(0, a.shape[1], 128)]

    m = functools.reduce(jnp.maximum, [t for s in s_parts for t in lane_tiles(s)]).max(axis=-1, keepdims=True)
    if extra_logit is not None:
        m = jnp.maximum(m, extra_logit)
    p_parts, p_tiles = [], []
    for s in s_parts:
        p = jnp.exp2(s - m)
        p_tiles += lane_tiles(p)
        p_parts.append(p.astype(BF16))
    l = functools.reduce(jnp.add, p_tiles).sum(axis=-1, keepdims=True)
    if extra_logit is not None:
        l = l + jnp.exp2(extra_logit - m)
    return p_parts, l


def _softmax_pv(s_parts, v_parts, extra_logit=None):
    p_parts, l = _softmax_weights(s_parts, extra_logit)
    o = None
    for p, v in zip(p_parts, v_parts):
        pv = _dot(p, v)
        o = pv if o is None else o + pv
    return o / l


def _stack_heads(x):
    head0 = lax.broadcasted_iota(jnp.int32, x.shape, 1) < 64
    zero = jnp.zeros_like(x)
    return jnp.concatenate([jnp.where(head0, x, zero), jnp.where(head0, zero, x)], axis=0)


def _merge_heads(o):
    n = o.shape[0] // 2
    head0 = lax.broadcasted_iota(jnp.int32, (n, o.shape[1]), 1) < 64
    return jnp.where(head0, o[:n], o[n:])


def _na_kernel(q_ref, k_ref, v_ref, gate_ref, tab_ref, out_ref, s_scr, sc_scr):
    kc = k_ref[0, 0:CTX_LEN, :]
    vc = v_ref[0, 0:CTX_LEN, :]
    n_lat = NA_KH * GRID_W
    rows_q = 2 * GRID_W
    n_groups = GRID_ROWS // NA_UNROLL

    def rows_of(r):
        rs = jnp.clip(r - NA_KH // 2, 0, GRID_ROWS - NA_KH)
        q_rows = pl.ds(pl.multiple_of(CTX_LEN + r * GRID_W, GRID_W), GRID_W)
        k_rows = pl.ds(pl.multiple_of(CTX_LEN + rs * GRID_W, GRID_W), n_lat)
        return rs, q_rows, k_rows

    def window_scores(r, slot):
        rs, q_rows, k_rows = rows_of(r)
        dr0 = rs - r + NA_KH - 1
        qs = _stack_heads(q_ref[0, q_rows, :])
        bias = jnp.concatenate(
            [jnp.concatenate([tab_ref[hh, dr0 + 2 * j] for j in range(NA_KH // 2)], axis=1)
             for hh in range(2)], axis=0)
        s_scr[slot] = _dot_nt(qs, k_ref[0, k_rows, :]) + bias

    def context_scores(g, slot):
        qs = jnp.concatenate([_stack_heads(q_ref[0, rows_of(g * NA_UNROLL + u)[1], :])
                              for u in range(NA_UNROLL)], axis=0)
        sc_scr[slot] = _dot_nt(qs, kc)

    context_scores(0, 0)
    for r in range(NA_LOOKAHEAD):
        window_scores(r, r)

    def body(i, carry):
        for g_slot in range(2):
            g = 2 * i + g_slot
            context_scores(jnp.minimum(g + 1, n_groups - 1), 1 - g_slot)
            for u in range(NA_UNROLL):
                r = g * NA_UNROLL + u
                window_scores(jnp.minimum(r + NA_LOOKAHEAD, GRID_ROWS - 1), (u + NA_LOOKAHEAD) % NA_UNROLL)
                _, q_rows, k_rows = rows_of(r)
                s_ctx = sc_scr[g_slot, u * rows_q:(u + 1) * rows_q, :]
                o = _merge_heads(_softmax_pv([s_scr[u], s_ctx], [v_ref[0, k_rows, :], vc]))
                out_ref[0, q_rows, :] = (o * _silu(gate_ref[0, q_rows, :].astype(F32))).astype(BF16)
        return carry

    lax.fori_loop(0, n_groups // 2, body, 0)

    for t in range(CTX_LEN // 128):
        rows = slice(t * 128, (t + 1) * 128)
        qs = _stack_heads(q_ref[0, rows, :])
        o = _merge_heads(_softmax_pv([_dot_nt(qs, kc)], [vc]))
        out_ref[0, rows, :] = (o * _silu(gate_ref[0, rows, :].astype(F32))).astype(BF16)


def _na(q, k, v, gate, table):
    bsz = q.shape[0]
    pairs = NA_HEADS // 2
    seq_spec = pl.BlockSpec((1, S_ALL, 128), lambda b, p: (b, 0, p))
    return pl.pallas_call(
        _na_kernel,
        out_shape=jax.ShapeDtypeStruct((bsz, S_ALL, NA_HEADS * NA_DH), BF16),
        grid=(bsz, pairs),
        in_specs=[seq_spec, seq_spec, seq_spec, seq_spec,
                  pl.BlockSpec((2, 2 * NA_KH - 2, GRID_W, 128), lambda b, p: (p, 0, 0, 0))],
        out_specs=seq_spec,
        scratch_shapes=[pltpu.VMEM((NA_UNROLL, 2 * GRID_W, NA_KH * GRID_W), F32),
                        pltpu.VMEM((2, NA_UNROLL * 2 * GRID_W, CTX_LEN), F32)],
        compiler_params=_params("parallel", "arbitrary"),
        name="na_attn",
    )(q, k, v, gate, table)


def _swa_mask_table():
    span = SW_BLOCK + 2 * SW_WINDOW
    r = np.arange(SW_BLOCK)[:, None]
    c = np.arange(span)[None, :]
    cases = [np.abs(r + off - c) <= SW_WINDOW for off in (0, SW_WINDOW, 2 * SW_WINDOW)]
    cases.append(np.zeros((SW_BLOCK, span), bool))
    return jnp.asarray(np.where(np.stack(cases), 0.0, NEG), F32)


def _swa_kernel(sink_ref, q_ref, k_ref, v_ref, gate_ref, mask_ref, out_ref, s_scr):
    n_blocks = S_ALL // SW_BLOCK
    ctx_blocks = CTX_LEN // SW_BLOCK
    n_tiles = SW_HEADS // 2
    span = SW_BLOCK + 2 * SW_WINDOW
    kc = k_ref[0, 0:CTX_LEN, :]
    vc = v_ref[0, 0:CTX_LEN, :]
    first = lax.broadcasted_iota(jnp.int32, (2 * SW_BLOCK, 1), 0) < SW_BLOCK

    def rows_of(n):
        qstart = (n - ctx_blocks) * SW_BLOCK
        kstart = jnp.clip(qstart - SW_WINDOW, 0, SEQ - span)
        q_rows = pl.ds(pl.multiple_of(n * SW_BLOCK, SW_BLOCK), SW_BLOCK)
        k_rows = pl.ds(pl.multiple_of(CTX_LEN + kstart, SW_BLOCK), span)
        return q_rows, k_rows

    def scores(n, t, slot):
        q_rows, k_rows = rows_of(n)
        case = jnp.where(n < ctx_blocks, 3,
                         jnp.where(n == ctx_blocks, 0, jnp.where(n == n_blocks - 1, 2, 1)))
        qs = _stack_heads(q_ref[0, q_rows, t * 128:(t + 1) * 128])
        mb = mask_ref[case]
        s_scr[slot, :, 0:span] = _dot_nt(qs, k_ref[0, k_rows, :]) + jnp.concatenate([mb, mb], axis=0)
        s_scr[slot, :, span:span + CTX_LEN] = _dot_nt(qs, kc)

    def finish(n, t, slot):
        q_rows, k_rows = rows_of(n)
        cols = slice(t * 128, (t + 1) * 128)
        sink = jnp.where(first, sink_ref[t], sink_ref[t + n_tiles]) * LOG2E
        o = _merge_heads(_softmax_pv([s_scr[slot, :, 0:span], s_scr[slot, :, span:span + CTX_LEN]],
                                     [v_ref[0, k_rows, :], vc], extra_logit=sink))
        out_ref[0, q_rows, cols] = (o * _silu(gate_ref[0, q_rows, cols].astype(F32))).astype(BF16)

    for t in range(SW_LOOKAHEAD):
        scores(0, t, t)

    def body(n, carry):
        for t in range(n_tiles):
            ahead = t + SW_LOOKAHEAD
            if ahead < n_tiles:
                scores(n, ahead, ahead)
            else:
                scores(jnp.minimum(n + 1, n_blocks - 1), ahead - n_tiles, ahead - n_tiles)
            finish(n, t, t)
        return carry

    lax.fori_loop(0, n_blocks, body, 0)


def _swa(q, k, v, gate, sink):
    bsz = q.shape[0]
    width = SW_HEADS * SW_DH
    span = SW_BLOCK + 2 * SW_WINDOW
    seq_spec = pl.BlockSpec((1, S_ALL, width), lambda b: (b, 0, 0))
    kv_spec = pl.BlockSpec((1, S_ALL, 128), lambda b: (b, 0, 0))
    return pl.pallas_call(
        _swa_kernel,
        out_shape=jax.ShapeDtypeStruct((bsz, S_ALL, width), BF16),
        grid=(bsz,),
        in_specs=[pl.BlockSpec(memory_space=pltpu.SMEM), seq_spec, kv_spec, kv_spec, seq_spec,
                  pl.BlockSpec((4, SW_BLOCK, span), lambda b: (0, 0, 0))],
        out_specs=seq_spec,
        scratch_shapes=[pltpu.VMEM((SW_HEADS // 2, 2 * SW_BLOCK, span + CTX_LEN), F32)],
        compiler_params=_params("parallel"),
        name="swa_attn",
    )(sink, q, k, v, gate, _swa_mask_table())


def _pad_cols(w, width):
    return jnp.pad(w, ((0, 0), (0, width - w.shape[1])))


def _prep_even(w_in, gate_w, gate_b):
    o = np.cumsum((0, 256, 256, 512, 512, 32, 512, 512, 512, 512))
    gq, gk, gv, gg, lra, nq, nk, nv, ng = [w_in[:, o[i]:o[i + 1]] for i in range(9)]
    w = jnp.concatenate([gq * GLA_DK ** -0.5, gk, gv, gg, nq * (NA_DH ** -0.5 * LOG2E), nk, nv, ng,
                         _pad_cols(lra, 128)], axis=1).astype(BF16)
    gws, gbs = [], []
    for p in range(GLA_HEADS // 2):
        cols = slice(p * 128, (p + 1) * 128)
        zero = jnp.zeros((GLA_RANK, 128), F32)
        gws.append(jnp.concatenate([jnp.concatenate([gate_w[0][:, cols], zero], axis=1),
                                    jnp.concatenate([zero, gate_w[1][:, cols]], axis=1)], axis=0))
        gbs.append(jnp.concatenate([gate_b[0][cols], gate_b[1][cols]])[None])
    return w, jnp.stack(gws).astype(BF16), jnp.stack(gbs)


def _swa_head_perm():
    order = []
    for t in range(SW_HEADS // 2):
        order += list(range(t * SW_DH, (t + 1) * SW_DH))
        order += list(range((t + SW_HEADS // 2) * SW_DH, (t + SW_HEADS // 2 + 1) * SW_DH))
    return np.asarray(order)


def _rope_swap(w):
    nf = SW_DH // 4
    idx = np.arange(w.shape[1]).reshape(-1, 2, nf)[:, ::-1, :].reshape(-1)
    return w[:, idx]


def _prep_odd(w_in, w_out):
    o = np.cumsum((0, 256, 256, 512, 512, 16, 512, 128, 128, 512))
    mq, mk, mv, mg, gates, sq, sk, sv, sg = [w_in[:, o[i]:o[i + 1]] for i in range(9)]
    perm = _swa_head_perm()
    sq = sq[:, perm] * (SW_DH ** -0.5 * LOG2E)
    w = jnp.concatenate([mq, mk * ML_DQK ** -0.5, mv, mg, sq, _rope_swap(sq), sk, _rope_swap(sk),
                         sv, _pad_cols(gates[:, _gate_perm()], 128), sg[:, perm]], axis=1).astype(BF16)
    half = ML_HEADS * ML_DV
    w_o = jnp.concatenate([w_out[:half], w_out[half:][perm]], axis=0).astype(BF16)
    return w, w_o


def _rope_tables():
    nf = SW_DH // 4
    freqs = ROPE_BASE ** (-jnp.arange(nf, dtype=F32) / nf)
    pos = jnp.arange(SEQ)
    rows = (pos // GRID_W).astype(F32)
    cols = (pos % GRID_W).astype(F32)
    ar = rows[:, None] * freqs[None, :]
    ac = cols[:, None] * freqs[None, :]
    cos = jnp.concatenate([jnp.cos(ar), jnp.cos(ar), jnp.cos(ac), jnp.cos(ac)], axis=1)
    sin = jnp.concatenate([-jnp.sin(ar), jnp.sin(ar), -jnp.sin(ac), jnp.sin(ac)], axis=1)
    cos = jnp.concatenate([jnp.ones((CTX_LEN, SW_DH), F32), cos], axis=0)
    sin = jnp.concatenate([jnp.zeros((CTX_LEN, SW_DH), F32), sin], axis=0)
    return jnp.tile(cos, (1, 2)), jnp.tile(sin, (1, 2))


def _gate_perm():
    return np.arange(4 * ML_HEADS).reshape(2, 2, 2, 2).transpose(2, 0, 1, 3).reshape(-1)


def _gate_rows(grow_t):
    bsz = grow_t.shape[0]
    g = grow_t.reshape(bsz, 2, 2, 2, 2, N_CHUNKS, CHUNK)
    return g.transpose(0, 2, 3, 1, 5, 4, 6).reshape(bsz, 2, 2, 2, N_CHUNKS, 2 * CHUNK)


def kernel(x, c, ctx, c_ctx, w_ada, b_ada, ln_g, ln_b, w_in_even, w_out_even, gla_gate_w, gla_gate_b,
           gla_norm_g, na_rpb, w_in_odd, w_out_odd, ml_gate_b, ml_norm_g, sw_sink):
    bsz = x.shape[0]
    mod_rows = 16
    cvec = jnp.zeros((mod_rows, D_MODEL), F32).at[:bsz].set(c).at[bsz].set(c_ctx)
    ada = _ada(cvec, w_ada, b_ada)
    xs = (ctx, x)
    cos_t, sin_t = _rope_tables()
    for l in range(DEPTH):
        i = l // 2
        last = l == DEPTH - 1
        lat = ada[l, :bsz].reshape(bsz, 1, 3, D_MODEL)
        cx = jnp.broadcast_to(ada[l, bsz].reshape(1, 1, 3, D_MODEL), (bsz, 1, 3, D_MODEL))
        mod = jnp.concatenate([lat, cx], axis=1)
        if l % 2 == 0:
            w, gw, gb = _prep_even(w_in_even[i], gla_gate_w[i], gla_gate_b[i])
            gq, gk, gv, gg, nq, nk, nv, ng, lr = _inproj_even(xs, mod, w)
            ya = _gla(gq, gk, gv, gg, lr, gw, gb, gla_norm_g[i].reshape(GLA_HEADS // 2, 1, 256))
            yb = _na(nq, nk, nv, ng, _rpb_table(na_rpb[i]))
            w_o = w_out_even[i].astype(BF16)
        else:
            w, w_o = _prep_odd(w_in_odd[i], w_out_odd[i])
            gate_b = _pad_cols(ml_gate_b[i].reshape(1, 4 * ML_HEADS)[:, _gate_perm()], 128)
            mq, mk, mv, mg, sq, sk, sv, sg, gcol, grow_t = _inproj_odd(xs[0], mod, w, cos_t, sin_t, gate_b)
            grow = _gate_rows(grow_t)
            ya = _mlstm(mq, mk, mv, mg, gcol, grow, ml_norm_g[i].reshape(ML_HEADS // 2, 1, 256))
            yb = _swa(sq, sk, sv, sg, sw_sink[i])
        xs = (_outproj(ya, yb, xs, mod, w_o, ln_g[l].reshape(1, D_MODEL), ln_b[l].reshape(1, D_MODEL), last),)
    return xs[0]
```

```python
import functools
import math

import numpy as np
import jax
import jax.numpy as jnp
from jax import lax
from jax.experimental import pallas as pl
from jax.experimental.pallas import tpu as pltpu

F32 = jnp.float32
BF16 = jnp.bfloat16

D_MODEL = 1024
SEQ = 4096
DEPTH = 4
GRID_W = 64
CTX_LEN = 256
S_ALL = CTX_LEN + SEQ
ALPHA = (2.0 * DEPTH) ** 0.25
LN_EPS = 1e-5
NORM_EPS = 1e-6
NEG = -1e30
LOG2E = 1.4426950408889634
GLA_HEADS, GLA_DK, GLA_DV, GLA_RANK, GLA_TAU = 4, 64, 128, 16, 16.0
NA_HEADS, NA_DH, NA_KH, NA_KW = 8, 64, 8, 16
ML_HEADS, ML_DQK, ML_DV = 4, 64, 128
SW_HEADS, SW_KV_HEADS, SW_DH, SW_WINDOW, SW_BLOCK = 8, 2, 64, 128, 128
ROPE_BASE = 10000.0
CHUNK = 64
N_CHUNKS = S_ALL // CHUNK
CTX_CHUNKS = CTX_LEN // CHUNK
ROW_TILE = 256
N_ROW_TILES = S_ALL // ROW_TILE
GRID_ROWS = SEQ // GRID_W
OUT_SUB = 4
SCAN_SLOTS = 4
SCAN_LOOKAHEAD = 2
NA_UNROLL = 4
NA_LOOKAHEAD = 2
SW_LOOKAHEAD = 1

VMEM_LIMIT = 56 * 1024 * 1024

NT_DIMS = (((1,), (1,)), ((), ()))
TN_DIMS = (((0,), (0,)), ((), ()))


def _dot(a, b):
    return jnp.dot(a, b, preferred_element_type=F32)


def _dot_nt(a, b):
    return lax.dot_general(a, b, NT_DIMS, preferred_element_type=F32)


def _dot_tn(a, b):
    return lax.dot_general(a, b, TN_DIMS, preferred_element_type=F32)


def _silu(x):
    h = 0.5 * x
    return h + h * jnp.tanh(h)


def _log_sigmoid(x):
    return jnp.minimum(x, 0.0) - jnp.log(1.0 + jnp.exp(-jnp.abs(x)))


def _cumsum_rows(x, reverse):
    n = x.shape[0]
    row = lax.broadcasted_iota(jnp.int32, x.shape, 0)
    k = 1
    while k < n:
        if reverse:
            x = x + jnp.where(row < n - k, pltpu.roll(x, n - k, axis=0), 0.0)
        else:
            x = x + jnp.where(row >= k, pltpu.roll(x, k, axis=0), 0.0)
        k *= 2
    return x


def _split_dot(mat_bf16, x):
    hi = x.astype(BF16)
    lo = (x - hi.astype(F32)).astype(BF16)
    return _dot(mat_bf16, hi) + _dot(mat_bf16, lo)


def _split_dot_r(x, mat_bf16):
    hi = x.astype(BF16)
    lo = (x - hi.astype(F32)).astype(BF16)
    return _dot(hi, mat_bf16) + _dot(lo, mat_bf16)


def _halves_to_tiles(x):
    half0 = lax.broadcasted_iota(jnp.int32, x.shape, 1) < 64
    sw = pltpu.roll(x, 64, axis=1)
    return jnp.concatenate([jnp.where(half0, x, sw), jnp.where(half0, sw, x)], axis=1)


def _params(*sem):
    return pltpu.CompilerParams(dimension_semantics=sem, vmem_limit_bytes=VMEM_LIMIT)


def _ada_kernel(c_ref, w_ref, b_ref, o_ref):
    cond = _silu(c_ref[...])
    w = w_ref[0]
    c_hi = cond.astype(BF16)
    c_lo = (cond - c_hi.astype(F32)).astype(BF16)
    w_hi = w.astype(BF16)
    w_lo = (w - w_hi.astype(F32)).astype(BF16)
    o_ref[0] = _dot(c_hi, w_hi) + _dot(c_hi, w_lo) + _dot(c_lo, w_hi) + b_ref[0]


def _ada(cvec, w_ada, b_ada):
    rows = cvec.shape[0]
    nblk = 3
    return pl.pallas_call(
        _ada_kernel,
        out_shape=jax.ShapeDtypeStruct((DEPTH, rows, 3 * D_MODEL), F32),
        grid=(DEPTH, nblk),
        in_specs=[
            pl.BlockSpec((rows, D_MODEL), lambda l, n: (0, 0)),
            pl.BlockSpec((1, D_MODEL, D_MODEL), lambda l, n: (l, 0, n)),
            pl.BlockSpec((1, 1, D_MODEL), lambda l, n: (l, 0, n)),
        ],
        out_specs=pl.BlockSpec((1, rows, D_MODEL), lambda l, n: (l, 0, n)),
        compiler_params=_params("arbitrary", "arbitrary"),
        name="ada_mod",
    )(cvec, w_ada, b_ada.reshape(DEPTH, 1, 3 * D_MODEL))


def _row_tile(x_refs, is_ctx):
    if len(x_refs) == 1:
        return x_refs[0][0]
    return jnp.where(is_ctx, x_refs[0][0], x_refs[1][0])


def _x_specs(xs, tile_of):
    def combined(*ids):
        b, j = tile_of(*ids)
        return (b, j, 0)

    def context(*ids):
        return (tile_of(*ids)[0], 0, 0)

    def latent(*ids):
        b, j = tile_of(*ids)
        return (b, jnp.maximum(j - 1, 0), 0)

    if len(xs) == 1:
        return [pl.BlockSpec((1, ROW_TILE, D_MODEL), combined)]
    return [pl.BlockSpec((1, ROW_TILE, D_MODEL), context), pl.BlockSpec((1, ROW_TILE, D_MODEL), latent)]


def _modulated(x_refs, mod_ref):
    shift = mod_ref[0, 0, 0:1, :]
    scale = mod_ref[0, 0, 1:2, :]
    x = _row_tile(x_refs, pl.program_id(1) == 0)
    return (x * (1.0 + scale) + shift).astype(BF16)


EVEN_SLABS = (256, 256, 512, 512, 512, 512, 512, 512)


def _inproj_even_kernel(*refs, n_x):
    x_refs, (mod_ref, w_ref), outs = refs[:n_x], refs[n_x:n_x + 2], refs[n_x + 2:]
    h = _modulated(x_refs, mod_ref)
    off = 0
    for ref, width in zip(outs[:-1], EVEN_SLABS):
        ref[0] = _dot(h, w_ref[:, off:off + width]).astype(ref.dtype)
        off += width
    lr = _dot(h, w_ref[:, off:off + 128])
    outs[-1][0] = lr[:, :2 * GLA_RANK]


ODD_SLABS = (256, 256, 512, 512)


def _inproj_odd_kernel(x_ref, mod_ref, w_ref, cos_ref, sin_ref, gb_ref,
                       mq, mk, mv, mg, sq, sk, sv, sg, gcol, grow_t):
    h = _modulated((x_ref,), mod_ref)
    off = 0
    for ref, width in zip((mq, mk, mv, mg), ODD_SLABS):
        ref[0] = _dot(h, w_ref[:, off:off + width]).astype(ref.dtype)
        off += width
    cos = cos_ref[...]
    sin = sin_ref[...]
    a = _dot(h, w_ref[:, off:off + 512])
    b = _dot(h, w_ref[:, off + 512:off + 1024])
    for t in range(4):
        cols = slice(t * 128, (t + 1) * 128)
        sq[0, :, cols] = (a[:, cols] * cos + b[:, cols] * sin).astype(BF16)
    off += 1024
    kvg = _dot(h, w_ref[:, off:off + 512])
    sk[0] = (kvg[:, 0:128] * cos + kvg[:, 128:256] * sin).astype(BF16)
    sv[0] = kvg[:, 256:384].astype(BF16)
    g = kvg[:, 384:512] + gb_ref[...]
    for p in range(ML_HEADS // 2):
        gcol[0, p] = g[:, p * 8:(p + 1) * 8]
    grow_t[0] = g.T[0:4 * ML_HEADS, :]
    off += 512
    sg[0] = _dot(h, w_ref[:, off:off + 512]).astype(BF16)


def _mod_index(b, j):
    return (b, jnp.where(j == 0, 1, 0), 0, 0)


def _inproj_even(xs, mod, w):
    bsz = xs[0].shape[0]
    ntot = w.shape[1]
    out_shape = [jax.ShapeDtypeStruct((bsz, S_ALL, wd), BF16) for wd in EVEN_SLABS]
    out_shape.append(jax.ShapeDtypeStruct((bsz, S_ALL, 2 * GLA_RANK), F32))
    out_specs = [pl.BlockSpec((1, ROW_TILE, wd), lambda b, j: (b, j, 0)) for wd in EVEN_SLABS]
    out_specs.append(pl.BlockSpec((1, ROW_TILE, 2 * GLA_RANK), lambda b, j: (b, j, 0)))
    return pl.pallas_call(
        functools.partial(_inproj_even_kernel, n_x=len(xs)),
        out_shape=out_shape,
        grid=(bsz, N_ROW_TILES),
        in_specs=_x_specs(xs, lambda b, j: (b, j)) + [
            pl.BlockSpec((1, 1, 3, D_MODEL), _mod_index),
            pl.BlockSpec((D_MODEL, ntot), lambda b, j: (0, 0)),
        ],
        out_specs=out_specs,
        compiler_params=_params("parallel", "arbitrary"),
        name="inproj_even",
    )(*xs, mod, w)


def _inproj_odd(xc, mod, w, cos_t, sin_t, gate_b):
    bsz = xc.shape[0]
    ntot = w.shape[1]
    widths = (256, 256, 512, 512, 512, 128, 128, 512)
    out_shape = [jax.ShapeDtypeStruct((bsz, S_ALL, wd), BF16) for wd in widths]
    out_shape.append(jax.ShapeDtypeStruct((bsz, ML_HEADS // 2, S_ALL, 8), F32))
    out_shape.append(jax.ShapeDtypeStruct((bsz, 4 * ML_HEADS, S_ALL), F32))
    out_specs = [pl.BlockSpec((1, ROW_TILE, wd), lambda b, j: (b, j, 0)) for wd in widths]
    out_specs.append(pl.BlockSpec((1, ML_HEADS // 2, ROW_TILE, 8), lambda b, j: (b, 0, j, 0)))
    out_specs.append(pl.BlockSpec((1, 4 * ML_HEADS, ROW_TILE), lambda b, j: (b, 0, j)))
    return pl.pallas_call(
        _inproj_odd_kernel,
        out_shape=out_shape,
        grid=(bsz, N_ROW_TILES),
        in_specs=[
            pl.BlockSpec((1, ROW_TILE, D_MODEL), lambda b, j: (b, j, 0)),
            pl.BlockSpec((1, 1, 3, D_MODEL), _mod_index),
            pl.BlockSpec((D_MODEL, ntot), lambda b, j: (0, 0)),
            pl.BlockSpec((ROW_TILE, 128), lambda b, j: (j, 0)),
            pl.BlockSpec((ROW_TILE, 128), lambda b, j: (j, 0)),
            pl.BlockSpec((1, 128), lambda b, j: (0, 0)),
        ],
        out_specs=out_specs,
        compiler_params=_params("parallel", "arbitrary"),
        name="inproj_odd",
    )(xc, mod, w, cos_t, sin_t, gate_b)


def _outproj_kernel(*refs, n_x, n_tiles, n_sub):
    ya_refs, yb_refs = refs[:n_sub], refs[n_sub:2 * n_sub]
    x_refs = refs[2 * n_sub:2 * n_sub + n_sub * n_x]
    mod_refs = refs[2 * n_sub + n_sub * n_x:3 * n_sub + n_sub * n_x]
    w_ref, g_ref, b_ref = refs[3 * n_sub + n_sub * n_x:3 * n_sub + n_sub * n_x + 3]
    o_ref, r_scr = refs[-2:]
    t = pl.program_id(0)

    @pl.when(t == 0)
    def _():
        r_scr[...] = jnp.zeros(r_scr.shape, F32)

    for s in range(n_sub):
        r = r_scr[s]
        mu = jnp.mean(r, axis=-1, keepdims=True)
        d = r - mu
        var = jnp.mean(d * d, axis=-1, keepdims=True)
        o_ref[s * ROW_TILE:(s + 1) * ROW_TILE, :] = d * lax.rsqrt(var + LN_EPS) * g_ref[...] + b_ref[...]

    half = ya_refs[0].shape[2]
    ya = jnp.concatenate([ref[0] for ref in ya_refs], axis=0)
    yb = jnp.concatenate([ref[0] for ref in yb_refs], axis=0)
    y = _dot(ya, w_ref[0:half, :]) + _dot(yb, w_ref[half:2 * half, :])
    for s in range(n_sub):
        x = _row_tile(x_refs[s * n_x:(s + 1) * n_x], lax.rem(n_sub * t + s, n_tiles) == 0)
        r_scr[s] = ALPHA * x + mod_refs[s][0, 0, 2:3, :] * y[s * ROW_TILE:(s + 1) * ROW_TILE]


def _outproj(ya, yb, xs, mod, w, ln_g, ln_b, latent_only):
    bsz = xs[0].shape[0]
    skip = 1 if latent_only else 0
    n_tiles = N_ROW_TILES - skip
    total = bsz * n_tiles
    n_sub = math.gcd(total, OUT_SUB)
    half = ya.shape[2]

    def tile_of(s):
        def fn(t):
            tt = jnp.minimum(n_sub * t + s, total - 1)
            return tt // n_tiles, lax.rem(tt, n_tiles) + skip
        return fn

    def y_index(s):
        def fn(t):
            b, j = tile_of(s)(t)
            return (b, j, 0)
        return fn

    def mod_index(s):
        return lambda t: _mod_index(*tile_of(s)(t))

    subs = range(n_sub)
    const = lambda t: (0, 0)
    x_specs = [spec for s in subs for spec in _x_specs(xs, tile_of(s))]
    out = pl.pallas_call(
        functools.partial(_outproj_kernel, n_x=len(xs), n_tiles=n_tiles, n_sub=n_sub),
        out_shape=jax.ShapeDtypeStruct((total * ROW_TILE, D_MODEL), F32),
        grid=(total // n_sub + 1,),
        in_specs=[pl.BlockSpec((1, ROW_TILE, half), y_index(s)) for s in subs]
        + [pl.BlockSpec((1, ROW_TILE, half), y_index(s)) for s in subs]
        + x_specs
        + [pl.BlockSpec((1, 1, 3, D_MODEL), mod_index(s)) for s in subs]
        + [pl.BlockSpec((2 * half, D_MODEL), const),
           pl.BlockSpec((1, D_MODEL), const),
           pl.BlockSpec((1, D_MODEL), const)],
        out_specs=pl.BlockSpec((n_sub * ROW_TILE, D_MODEL), lambda t: (jnp.maximum(t - 1, 0), 0)),
        scratch_shapes=[pltpu.VMEM((n_sub, ROW_TILE, D_MODEL), F32)],
        compiler_params=_params("arbitrary"),
        name="outproj_ln",
    )(*([ya] * n_sub), *([yb] * n_sub), *(list(xs) * n_sub), *([mod] * n_sub), w, ln_g, ln_b)
    return out.reshape(bsz, n_tiles * ROW_TILE, D_MODEL)


def _scan_epilogue(of_scr, ob_scr, gate_ref, ng_ref, out_ref):
    def body(t, carry):
        rows = pl.ds(pl.multiple_of(t * ROW_TILE, ROW_TILE), ROW_TILE)
        o = of_scr[rows, :] + ob_scr[rows, :]
        parts = []
        for hh in range(2):
            oh = o[:, hh * 128:(hh + 1) * 128]
            ms = jnp.mean(oh * oh, axis=-1, keepdims=True)
            parts.append(oh * lax.rsqrt(ms + NORM_EPS))
        y = jnp.concatenate(parts, axis=1) * ng_ref[0]
        out_ref[0, rows, :] = (y * _silu(gate_ref[0, rows, :].astype(F32))).astype(BF16)
        return carry

    lax.fori_loop(0, N_ROW_TILES, body, 0)


def _bwd_chunk(i):
    return jnp.where(i < CTX_CHUNKS, CTX_CHUNKS - 1 - i, N_CHUNKS - 1 + CTX_CHUNKS - i)


def _scan_pipeline(local_fn, state_fn):
    for i in range(SCAN_LOOKAHEAD):
        local_fn(i, 0, i)
        local_fn(_bwd_chunk(i), 1, i)

    def body(j, carry):
        for u in range(SCAN_SLOTS):
            i = SCAN_SLOTS * j + u
            nxt = jnp.minimum(i + SCAN_LOOKAHEAD, N_CHUNKS - 1)
            local_fn(nxt, 0, (u + SCAN_LOOKAHEAD) % SCAN_SLOTS)
            local_fn(_bwd_chunk(nxt), 1, (u + SCAN_LOOKAHEAD) % SCAN_SLOTS)
            state_fn(i, 0, u)
            state_fn(_bwd_chunk(i), 1, u)
        return carry

    lax.fori_loop(0, N_CHUNKS // SCAN_SLOTS, body, 0)


def _gla_kernel(q_ref, k_ref, v_ref, gate_ref, lr_ref, gw_ref, gb_ref, ng_ref, out_ref,
                g_scr, of_scr, ob_scr, st_scr, a_scr, qin_scr, u_scr, dl_scr):
    def pre(t, carry):
        rows = pl.ds(pl.multiple_of(t * ROW_TILE, ROW_TILE), ROW_TILE)
        z = _dot(lr_ref[0, rows, :].astype(BF16), gw_ref[0]) + gb_ref[0]
        g_scr[rows, :] = _log_sigmoid(z) * (LOG2E / GLA_TAU)
        return carry

    lax.fori_loop(0, N_ROW_TILES, pre, 0)
    st_scr[...] = jnp.zeros(st_scr.shape, F32)

    def local(cc, d, slot):
        rows = pl.ds(pl.multiple_of(cc * CHUNK, CHUNK), CHUNK)
        g = g_scr[rows, d * 128:(d + 1) * 128]
        b = _cumsum_rows(g, reverse=(d == 1))
        b_last = b[0:1, :] if d == 1 else b[CHUNK - 1:CHUNK, :]
        q = q_ref[0, rows, :].astype(F32)
        k = k_ref[0, rows, :].astype(F32)
        q_in = q * jnp.exp2(b)
        k_in = (k * jnp.exp2(-b)).astype(BF16)
        k_dec = (k * jnp.exp2(b_last - b)).astype(BF16)

        head0 = lax.broadcasted_iota(jnp.int32, (CHUNK, 128), 1) < GLA_DK
        q_stack = jnp.concatenate([jnp.where(head0, q_in, 0.0), jnp.where(head0, 0.0, q_in)],
                                  axis=0).astype(BF16)
        a = _dot_nt(q_stack, k_in)
        t_i = lax.broadcasted_iota(jnp.int32, (2 * CHUNK, CHUNK), 0) & (CHUNK - 1)
        s_i = lax.broadcasted_iota(jnp.int32, (2 * CHUNK, CHUNK), 1)
        keep = (s_i >= t_i) if d == 1 else (s_i <= t_i)
        a_scr[slot, d] = jnp.where(keep, a, 0.0).astype(BF16)
        qin_scr[slot, d] = q_in.astype(BF16)

        u = _dot_tn(v_ref[0, rows, :], k_dec)
        r_i = lax.broadcasted_iota(jnp.int32, u.shape, 0) < GLA_DV
        c_i = lax.broadcasted_iota(jnp.int32, u.shape, 1) < GLA_DK
        u_scr[slot, d] = jnp.where(r_i == c_i, u, 0.0)
        dl_scr[slot, d] = jnp.exp2(b_last)

    def state(cc, d, slot):
        rows = pl.ds(pl.multiple_of(cc * CHUNK, CHUNK), CHUNK)
        v = v_ref[0, rows, :]
        a = a_scr[slot, d]
        vhead0 = lax.broadcasted_iota(jnp.int32, (CHUNK, 2 * GLA_DV), 1) < GLA_DV
        zero = jnp.zeros_like(v)
        st = st_scr[d]
        o = (_dot(a[:CHUNK], jnp.where(vhead0, v, zero))
             + _dot(a[CHUNK:], jnp.where(vhead0, zero, v))
             + _dot_nt(qin_scr[slot, d], st.astype(BF16)))
        if d == 0:
            of_scr[rows, :] = o
        else:
            ob_scr[rows, :] = o
        st_scr[d] = st * dl_scr[slot, d] + u_scr[slot, d]

    _scan_pipeline(local, state)
    _scan_epilogue(of_scr, ob_scr, gate_ref, ng_ref, out_ref)


def _gla(q, k, v, gate, lr, gw, gb, ng):
    bsz = q.shape[0]
    pairs = GLA_HEADS // 2
    return pl.pallas_call(
        _gla_kernel,
        out_shape=jax.ShapeDtypeStruct((bsz, S_ALL, GLA_HEADS * GLA_DV), BF16),
        grid=(bsz, pairs),
        in_specs=[
            pl.BlockSpec((1, S_ALL, 128), lambda b, p: (b, 0, p)),
            pl.BlockSpec((1, S_ALL, 128), lambda b, p: (b, 0, p)),
            pl.BlockSpec((1, S_ALL, 256), lambda b, p: (b, 0, p)),
            pl.BlockSpec((1, S_ALL, 256), lambda b, p: (b, 0, p)),
            pl.BlockSpec((1, S_ALL, 2 * GLA_RANK), lambda b, p: (b, 0, 0)),
            pl.BlockSpec((1, 2 * GLA_RANK, 256), lambda b, p: (p, 0, 0)),
            pl.BlockSpec((1, 1, 256), lambda b, p: (p, 0, 0)),
            pl.BlockSpec((1, 1, 256), lambda b, p: (p, 0, 0)),
        ],
        out_specs=pl.BlockSpec((1, S_ALL, 256), lambda b, p: (b, 0, p)),
        scratch_shapes=[
            pltpu.VMEM((S_ALL, 256), F32),
            pltpu.VMEM((S_ALL, 256), F32),
            pltpu.VMEM((S_ALL, 256), F32),
            pltpu.VMEM((2, 2 * GLA_DV, 2 * GLA_DK), F32),
            pltpu.VMEM((SCAN_SLOTS, 2, 2 * CHUNK, CHUNK), BF16),
            pltpu.VMEM((SCAN_SLOTS, 2, CHUNK, 2 * GLA_DK), BF16),
            pltpu.VMEM((SCAN_SLOTS, 2, 2 * GLA_DV, 2 * GLA_DK), F32),
            pltpu.VMEM((SCAN_SLOTS, 2, 1, 2 * GLA_DK), F32),
        ],
        compiler_params=_params("parallel", "arbitrary"),
        name="gla_scan",
    )(q, k, v, gate, lr, gw, gb, ng)


def _cummax_rows_chunked(x, reverse):
    n = x.shape[0]
    pos = lax.broadcasted_iota(jnp.int32, x.shape, 0) & (CHUNK - 1)
    k = 1
    while k < CHUNK:
        if reverse:
            x = jnp.maximum(x, jnp.where(pos < CHUNK - k, pltpu.roll(x, n - k, axis=0), NEG))
        else:
            x = jnp.maximum(x, jnp.where(pos >= k, pltpu.roll(x, k, axis=0), NEG))
        k *= 2
    return x


def _mlstm_kernel(q_ref, k_ref, v_ref, gate_ref, gcol_ref, grow_ref, ng_ref, out_ref,
                  bcm_scr, rcm_scr, cmx_scr, rrow_scr, blast_scr, wsmax_scr, blast2_scr, wsmax2_scr,
                  of_scr, ob_scr, st_scr, m2_scr, nd_scr, u_scr):
    lane128 = lax.broadcasted_iota(jnp.int32, (1, 128), 1)
    half0_row = lane128 < CHUNK

    li = lax.broadcasted_iota(jnp.int32, (128, 128), 0)
    lj = lax.broadcasted_iota(jnp.int32, (128, 128), 1)
    same_half = (li < CHUNK) == (lj < CHUNK)
    ones_half = jnp.where(same_half, 1.0, 0.0).astype(BF16)
    for d in range(2):
        within = (li >= lj) if d == 1 else (li <= lj)
        tri = jnp.where(same_half & within, 1.0, 0.0).astype(BF16)
        lf = _log_sigmoid(grow_ref[0, d, 1, 0]) * LOG2E
        b_row = _split_dot_r(lf, tri)
        total = _split_dot_r(lf, ones_half)
        r_row = grow_ref[0, d, 0, 0] * LOG2E - b_row
        rrow_scr[d] = r_row
        blast_scr[d] = total
        mx0 = jnp.max(jnp.where(half0_row, r_row, NEG), axis=-1, keepdims=True)
        mx1 = jnp.max(jnp.where(half0_row, NEG, r_row), axis=-1, keepdims=True)
        wsmax = total + jnp.where(half0_row, mx0, mx1)
        wsmax_scr[d] = wsmax
        blast2_scr[d] = _halves_to_tiles(total)
        wsmax2_scr[d] = _halves_to_tiles(wsmax)

    ti = lax.broadcasted_iota(jnp.int32, (ROW_TILE, ROW_TILE), 0)
    tj = lax.broadcasted_iota(jnp.int32, (ROW_TILE, ROW_TILE), 1)
    same_chunk = (ti >> 6) == (tj >> 6)
    tril = jnp.where(same_chunk & (tj <= ti), 1.0, 0.0).astype(BF16)
    triu = jnp.where(same_chunk & (tj >= ti), 1.0, 0.0).astype(BF16)
    half0_tile = lax.broadcasted_iota(jnp.int32, (ROW_TILE, 128), 1) < CHUNK

    def pre(t, carry):
        rows = pl.ds(pl.multiple_of(t * ROW_TILE, ROW_TILE), ROW_TILE)
        gc = gcol_ref[0, 0, rows, :]
        lf = _log_sigmoid(gc) * LOG2E
        gi = gc * LOG2E
        for d in range(2):
            bcol = _split_dot(triu if d == 1 else tril, lf)
            b0 = jnp.broadcast_to(bcol[:, d * 4 + 2:d * 4 + 3], (ROW_TILE, 128))
            b1 = jnp.broadcast_to(bcol[:, d * 4 + 3:d * 4 + 4], (ROW_TILE, 128))
            i0 = jnp.broadcast_to(gi[:, d * 4:d * 4 + 1], (ROW_TILE, 128))
            i1 = jnp.broadcast_to(gi[:, d * 4 + 1:d * 4 + 2], (ROW_TILE, 128))
            bc = jnp.where(half0_tile, b0, b1)
            rc = jnp.where(half0_tile, i0, i1) - bc
            bcm_scr[d, rows, :] = bc
            rcm_scr[d, rows, :] = rc
            cmx_scr[d, rows, :] = _cummax_rows_chunked(rc, reverse=(d == 1))
        return carry

    lax.fori_loop(0, N_ROW_TILES, pre, 0)
    st_scr[...] = jnp.zeros(st_scr.shape, F32)
    m2_scr[...] = jnp.zeros(m2_scr.shape, F32)

    def local(cc, d, slot):
        rows = pl.ds(pl.multiple_of(cc * CHUNK, CHUNK), CHUNK)
        rc = rcm_scr[d, rows, :]
        cmx = cmx_scr[d, rows, :]
        rr = rrow_scr[d, pl.ds(cc, 1), :]
        bl = blast_scr[d, pl.ds(cc, 1), :]
        wm = wsmax_scr[d, pl.ds(cc, 1), :]
        q = q_ref[0, rows, :]
        k = k_ref[0, rows, :]
        v = v_ref[0, rows, :]

        half0 = lax.broadcasted_iota(jnp.int32, (CHUNK, 128), 1) < CHUNK
        t_i = lax.broadcasted_iota(jnp.int32, (CHUNK, 128), 0)
        s_i = lax.broadcasted_iota(jnp.int32, (CHUNK, 128), 1) & (CHUNK - 1)
        keep = (s_i >= t_i) if d == 1 else (s_i <= t_i)
        w = jnp.exp2(jnp.where(keep, rr - cmx, NEG))

        zero_k = jnp.zeros_like(k)
        k_stack = jnp.concatenate([jnp.where(half0, k, zero_k), jnp.where(half0, zero_k, k)], axis=0)
        s_w = (_dot_nt(q, k_stack) * w).astype(BF16)

        vhead0 = lax.broadcasted_iota(jnp.int32, (CHUNK, 2 * ML_DV), 1) < ML_DV
        zero_v = jnp.zeros_like(v)
        one_v = jnp.ones_like(v)
        ones0 = jnp.where(vhead0, 1.0, 0.0).astype(BF16)
        ones1 = jnp.where(vhead0, 0.0, 1.0).astype(BF16)
        vo_stack = jnp.concatenate(
            [jnp.concatenate([jnp.where(vhead0, v, zero_v), ones0], axis=1),
             jnp.concatenate([jnp.where(vhead0, zero_v, v), ones1], axis=1)],
            axis=0)
        nd_scr[slot, d] = _dot(s_w, vo_stack)

        kws = (k.astype(F32) * jnp.exp2(rc + (bl - wm))).astype(BF16)
        u = _dot_tn(kws, jnp.concatenate([v, one_v], axis=1))
        for blk in range(4):
            r0 = (blk // 2) * ML_DQK
            c0 = (blk % 2) * 2 * ML_DV + (blk // 2) * ML_DV
            u_scr[slot, d, blk] = u[r0:r0 + ML_DQK, c0:c0 + ML_DV]

    def state(cc, d, slot):
        rows = pl.ds(pl.multiple_of(cc * CHUNK, CHUNK), CHUNK)
        cmx2 = _halves_to_tiles(cmx_scr[d, rows, :])
        bc2 = _halves_to_tiles(bcm_scr[d, rows, :])
        m_old = m2_scr[d]
        g = jnp.maximum(cmx2, m_old)
        w_loc = jnp.exp2(cmx2 - g)
        w_int = jnp.exp2(m_old - g)
        floor = jnp.exp2(-(bc2 + g))

        blocks = [st_scr[d, blk] for blk in range(4)]
        zero = jnp.zeros((ML_DQK, ML_DV), BF16)
        cm0, n0, cm1, n1 = [x.astype(BF16) for x in blocks]
        st = jnp.concatenate([jnp.concatenate([cm0, zero, n0, zero], axis=1),
                              jnp.concatenate([zero, cm1, zero, n1], axis=1)], axis=0)
        sd = _dot(q_ref[0, rows, :], st)
        nd = nd_scr[slot, d]
        num = w_loc * nd[:, :2 * ML_DV] + w_int * sd[:, :2 * ML_DV]
        den = w_loc * nd[:, 2 * ML_DV:] + w_int * sd[:, 2 * ML_DV:]
        h_out = num / jnp.maximum(jnp.abs(den), floor)
        if d == 0:
            of_scr[rows, :] = h_out
        else:
            ob_scr[rows, :] = h_out

        bl = blast2_scr[d, pl.ds(cc, 1), :]
        wm = wsmax2_scr[d, pl.ds(cc, 1), :]
        m_new = jnp.maximum(bl + m_old, wm)
        dec = jnp.exp2(bl + m_old - m_new)
        ws_scale = jnp.exp2(wm - m_new)
        for blk in range(4):
            head = slice((blk // 2) * ML_DV, (blk // 2 + 1) * ML_DV)
            st_scr[d, blk] = blocks[blk] * dec[:, head] + u_scr[slot, d, blk] * ws_scale[:, head]
        m2_scr[d] = m_new

    _scan_pipeline(local, state)
    _scan_epilogue(of_scr, ob_scr, gate_ref, ng_ref, out_ref)


def _mlstm(q, k, v, gate, gcol, grow, ng):
    bsz = q.shape[0]
    pairs = ML_HEADS // 2
    return pl.pallas_call(
        _mlstm_kernel,
        out_shape=jax.ShapeDtypeStruct((bsz, S_ALL, ML_HEADS * ML_DV), BF16),
        grid=(bsz, pairs),
        in_specs=[
            pl.BlockSpec((1, S_ALL, 128), lambda b, p: (b, 0, p)),
            pl.BlockSpec((1, S_ALL, 128), lambda b, p: (b, 0, p)),
            pl.BlockSpec((1, S_ALL, 256), lambda b, p: (b, 0, p)),
            pl.BlockSpec((1, S_ALL, 256), lambda b, p: (b, 0, p)),
            pl.BlockSpec((1, 1, S_ALL, 8), lambda b, p: (b, p, 0, 0)),
            pl.BlockSpec((1, 2, 2, 1, N_CHUNKS, 128), lambda b, p: (b, 0, 0, p, 0, 0)),
            pl.BlockSpec((1, 1, 256), lambda b, p: (p, 0, 0)),
        ],
        out_specs=pl.BlockSpec((1, S_ALL, 256), lambda b, p: (b, 0, p)),
        scratch_shapes=[
            pltpu.VMEM((2, S_ALL, 128), F32),
            pltpu.VMEM((2, S_ALL, 128), F32),
            pltpu.VMEM((2, S_ALL, 128), F32),
            pltpu.VMEM((2, N_CHUNKS, 128), F32),
            pltpu.VMEM((2, N_CHUNKS, 128), F32),
            pltpu.VMEM((2, N_CHUNKS, 128), F32),
            pltpu.VMEM((2, N_CHUNKS, 2 * ML_DV), F32),
            pltpu.VMEM((2, N_CHUNKS, 2 * ML_DV), F32),
            pltpu.VMEM((S_ALL, 2 * ML_DV), F32),
            pltpu.VMEM((S_ALL, 2 * ML_DV), F32),
            pltpu.VMEM((2, 4, ML_DQK, ML_DV), F32),
            pltpu.VMEM((2, 1, 2 * ML_DV), F32),
            pltpu.VMEM((SCAN_SLOTS, 2, CHUNK, 4 * ML_DV), F32),
            pltpu.VMEM((SCAN_SLOTS, 2, 4, ML_DQK, ML_DV), F32),
        ],
        compiler_params=_params("parallel", "arbitrary"),
        name="mlstm_scan",
    )(q, k, v, gate, gcol, grow, ng)


def _rpb_table_kernel(rpb_ref, o_ref):
    h = pl.program_id(0)
    n_dc = 2 * NA_KW - 1
    qi = lax.broadcasted_iota(jnp.int32, (GRID_W, 128), 0)
    lane = lax.broadcasted_iota(jnp.int32, (GRID_W, 128), 1)
    wi = lane & (GRID_W - 1)
    second = lane >= GRID_W
    dc = jnp.clip(wi - qi + (NA_KW - 1), 0, n_dc - 1)
    cs = jnp.clip(qi - NA_KW // 2, 0, GRID_W - NA_KW)
    col_ok = (wi >= cs) & (wi < cs + NA_KW)
    for dr in range(2 * NA_KH - 2):
        base = h * ((2 * NA_KH - 1) * n_dc) + dr * n_dc
        acc = jnp.zeros((GRID_W, 128), F32)
        for c in range(n_dc):
            val = jnp.where(second, rpb_ref[base + n_dc + c], rpb_ref[base + c])
            acc = jnp.where(dc == c, val, acc)
        o_ref[0, dr] = jnp.where(col_ok, acc * LOG2E, NEG)


def _rpb_table(rpb):
    flat = rpb.reshape(-1)
    return pl.pallas_call(
        _rpb_table_kernel,
        out_shape=jax.ShapeDtypeStruct((NA_HEADS, 2 * NA_KH - 2, GRID_W, 128), F32),
        grid=(NA_HEADS,),
        in_specs=[pl.BlockSpec(memory_space=pltpu.SMEM)],
        out_specs=pl.BlockSpec((1, 2 * NA_KH - 2, GRID_W, 128), lambda h: (h, 0, 0, 0)),
        compiler_params=_params("arbitrary"),
        name="na_rpb_table",
    )(flat)


def _softmax_weights(s_parts, extra_logit=None):
    def lane_tiles(a):
        return [a[:, i:i + 128] for i in range(0, a.shape[1], 128)]

    m = functools.reduce(jnp.maximum, [t for s in s_parts for t in lane_tiles(s)]).max(axis=-1, keepdims=True)
    if extra_logit is not None:
        m = jnp.maximum(m, extra_logit)
    p_parts, p_tiles = [], []
    for s in s_parts:
        p = jnp.exp2(s - m)
        p_tiles += lane_tiles(p)
        p_parts.append(p.astype(BF16))
    l = functools.reduce(jnp.add, p_tiles).sum(axis=-1, keepdims=True)
    if extra_logit is not None:
        l = l + jnp.exp2(extra_logit - m)
    return p_parts, l


def _softmax_pv(s_parts, v_parts, extra_logit=None):
    p_parts, l = _softmax_weights(s_parts, extra_logit)
    o = None
    for p, v in zip(p_parts, v_parts):
        pv = _dot(p, v)
        o = pv if o is None else o + pv
    return o / l


def _stack_heads(x):
    head0 = lax.broadcasted_iota(jnp.int32, x.shape, 1) < 64
    zero = jnp.zeros_like(x)
    return jnp.concatenate([jnp.where(head0, x, zero), jnp.where(head0, zero, x)], axis=0)


def _merge_heads(o):
    n = o.shape[0] // 2
    head0 = lax.broadcasted_iota(jnp.int32, (n, o.shape[1]), 1) < 64
    return jnp.where(head0, o[:n], o[n:])


def _na_kernel(q_ref, k_ref, v_ref, gate_ref, tab_ref, out_ref, s_scr, sc_scr):
    kc = k_ref[0, 0:CTX_LEN, :]
    vc = v_ref[0, 0:CTX_LEN, :]
    n_lat = NA_KH * GRID_W
    rows_q = 2 * GRID_W
    n_groups = GRID_ROWS // NA_UNROLL

    def rows_of(r):
        rs = jnp.clip(r - NA_KH // 2, 0, GRID_ROWS - NA_KH)
        q_rows = pl.ds(pl.multiple_of(CTX_LEN + r * GRID_W, GRID_W), GRID_W)
        k_rows = pl.ds(pl.multiple_of(CTX_LEN + rs * GRID_W, GRID_W), n_lat)
        return rs, q_rows, k_rows

    def window_scores(r, slot):
        rs, q_rows, k_rows = rows_of(r)
        dr0 = rs - r + NA_KH - 1
        qs = _stack_heads(q_ref[0, q_rows, :])
        bias = jnp.concatenate(
            [jnp.concatenate([tab_ref[hh, dr0 + 2 * j] for j in range(NA_KH // 2)], axis=1)
             for hh in range(2)], axis=0)
        s_scr[slot] = _dot_nt(qs, k_ref[0, k_rows, :]) + bias

    def context_scores(g, slot):
        qs = jnp.concatenate([_stack_heads(q_ref[0, rows_of(g * NA_UNROLL + u)[1], :])
                              for u in range(NA_UNROLL)], axis=0)
        sc_scr[slot] = _dot_nt(qs, kc)

    context_scores(0, 0)
    for r in range(NA_LOOKAHEAD):
        window_scores(r, r)

    def body(i, carry):
        for g_slot in range(2):
            g = 2 * i + g_slot
            context_scores(jnp.minimum(g + 1, n_groups - 1), 1 - g_slot)
            for u in range(NA_UNROLL):
                r = g * NA_UNROLL + u
                window_scores(jnp.minimum(r + NA_LOOKAHEAD, GRID_ROWS - 1), (u + NA_LOOKAHEAD) % NA_UNROLL)
                _, q_rows, k_rows = rows_of(r)
                s_ctx = sc_scr[g_slot, u * rows_q:(u + 1) * rows_q, :]
                o = _merge_heads(_softmax_pv([s_scr[u], s_ctx], [v_ref[0, k_rows, :], vc]))
                out_ref[0, q_rows, :] = (o * _silu(gate_ref[0, q_rows, :].astype(F32))).astype(BF16)
        return carry

    lax.fori_loop(0, n_groups // 2, body, 0)

    for t in range(CTX_LEN // 128):
        rows = slice(t * 128, (t + 1) * 128)
        qs = _stack_heads(q_ref[0, rows, :])
        o = _merge_heads(_softmax_pv([_dot_nt(qs, kc)], [vc]))
        out_ref[0, rows, :] = (o * _silu(gate_ref[0, rows, :].astype(F32))).astype(BF16)


def _na(q, k, v, gate, table):
    bsz = q.shape[0]
    pairs = NA_HEADS // 2
    seq_spec = pl.BlockSpec((1, S_ALL, 128), lambda b, p: (b, 0, p))
    return pl.pallas_call(
        _na_kernel,
        out_shape=jax.ShapeDtypeStruct((bsz, S_ALL, NA_HEADS * NA_DH), BF16),
        grid=(bsz, pairs),
        in_specs=[seq_spec, seq_spec, seq_spec, seq_spec,
                  pl.BlockSpec((2, 2 * NA_KH - 2, GRID_W, 128), lambda b, p: (p, 0, 0, 0))],
        out_specs=seq_spec,
        scratch_shapes=[pltpu.VMEM((NA_UNROLL, 2 * GRID_W, NA_KH * GRID_W), F32),
                        pltpu.VMEM((2, NA_UNROLL * 2 * GRID_W, CTX_LEN), F32)],
        compiler_params=_params("parallel", "arbitrary"),
        name="na_attn",
    )(q, k, v, gate, table)


def _swa_mask_table():
    span = SW_BLOCK + 2 * SW_WINDOW
    r = np.arange(SW_BLOCK)[:, None]
    c = np.arange(span)[None, :]
    cases = [np.abs(r + off - c) <= SW_WINDOW for off in (0, SW_WINDOW, 2 * SW_WINDOW)]
    cases.append(np.zeros((SW_BLOCK, span), bool))
    return jnp.asarray(np.where(np.stack(cases), 0.0, NEG), F32)


def _swa_kernel(sink_ref, q_ref, k_ref, v_ref, gate_ref, mask_ref, out_ref, s_scr):
    n_blocks = S_ALL // SW_BLOCK
    ctx_blocks = CTX_LEN // SW_BLOCK
    n_tiles = SW_HEADS // 2
    span = SW_BLOCK + 2 * SW_WINDOW
    kc = k_ref[0, 0:CTX_LEN, :]
    vc = v_ref[0, 0:CTX_LEN, :]
    first = lax.broadcasted_iota(jnp.int32, (2 * SW_BLOCK, 1), 0) < SW_BLOCK

    def rows_of(n):
        qstart = (n - ctx_blocks) * SW_BLOCK
        kstart = jnp.clip(qstart - SW_WINDOW, 0, SEQ - span)
        q_rows = pl.ds(pl.multiple_of(n * SW_BLOCK, SW_BLOCK), SW_BLOCK)
        k_rows = pl.ds(pl.multiple_of(CTX_LEN + kstart, SW_BLOCK), span)
        return q_rows, k_rows

    def scores(n, t, slot):
        q_rows, k_rows = rows_of(n)
        case = jnp.where(n < ctx_blocks, 3,
                         jnp.where(n == ctx_blocks, 0, jnp.where(n == n_blocks - 1, 2, 1)))
        qs = _stack_heads(q_ref[0, q_rows, t * 128:(t + 1) * 128])
        mb = mask_ref[case]
        s_scr[slot, :, 0:span] = _dot_nt(qs, k_ref[0, k_rows, :]) + jnp.concatenate([mb, mb], axis=0)
        s_scr[slot, :, span:span + CTX_LEN] = _dot_nt(qs, kc)

    def finish(n, t, slot):
        q_rows, k_rows = rows_of(n)
        cols = slice(t * 128, (t + 1) * 128)
        sink = jnp.where(first, sink_ref[t], sink_ref[t + n_tiles]) * LOG2E
        o = _merge_heads(_softmax_pv([s_scr[slot, :, 0:span], s_scr[slot, :, span:span + CTX_LEN]],
                                     [v_ref[0, k_rows, :], vc], extra_logit=sink))
        out_ref[0, q_rows, cols] = (o * _silu(gate_ref[0, q_rows, cols].astype(F32))).astype(BF16)

    for t in range(SW_LOOKAHEAD):
        scores(0, t, t)

    def body(n, carry):
        for t in range(n_tiles):
            ahead = t + SW_LOOKAHEAD
            if ahead < n_tiles:
                scores(n, ahead, ahead)
            else:
                scores(jnp.minimum(n + 1, n_blocks - 1), ahead - n_tiles, ahead - n_tiles)
            finish(n, t, t)
        return carry

    lax.fori_loop(0, n_blocks, body, 0)


def _swa(q, k, v, gate, sink):
    bsz = q.shape[0]
    width = SW_HEADS * SW_DH
    span = SW_BLOCK + 2 * SW_WINDOW
    seq_spec = pl.BlockSpec((1, S_ALL, width), lambda b: (b, 0, 0))
    kv_spec = pl.BlockSpec((1, S_ALL, 128), lambda b: (b, 0, 0))
    return pl.pallas_call(
        _swa_kernel,
        out_shape=jax.ShapeDtypeStruct((bsz, S_ALL, width), BF16),
        grid=(bsz,),
        in_specs=[pl.BlockSpec(memory_space=pltpu.SMEM), seq_spec, kv_spec, kv_spec, seq_spec,
                  pl.BlockSpec((4, SW_BLOCK, span), lambda b: (0, 0, 0))],
        out_specs=seq_spec,
        scratch_shapes=[pltpu.VMEM((SW_HEADS // 2, 2 * SW_BLOCK, span + CTX_LEN), F32)],
        compiler_params=_params("parallel"),
        name="swa_attn",
    )(sink, q, k, v, gate, _swa_mask_table())


def _pad_cols(w, width):
    return jnp.pad(w, ((0, 0), (0, width - w.shape[1])))


def _prep_even(w_in, gate_w, gate_b):
    o = np.cumsum((0, 256, 256, 512, 512, 32, 512, 512, 512, 512))
    gq, gk, gv, gg, lra, nq, nk, nv, ng = [w_in[:, o[i]:o[i + 1]] for i in range(9)]
    w = jnp.concatenate([gq * GLA_DK ** -0.5, gk, gv, gg, nq * (NA_DH ** -0.5 * LOG2E), nk, nv, ng,
                         _pad_cols(lra, 128)], axis=1).astype(BF16)
    gws, gbs = [], []
    for p in range(GLA_HEADS // 2):
        cols = slice(p * 128, (p + 1) * 128)
        zero = jnp.zeros((GLA_RANK, 128), F32)
        gws.append(jnp.concatenate([jnp.concatenate([gate_w[0][:, cols], zero], axis=1),
                                    jnp.concatenate([zero, gate_w[1][:, cols]], axis=1)], axis=0))
        gbs.append(jnp.concatenate([gate_b[0][cols], gate_b[1][cols]])[None])
    return w, jnp.stack(gws).astype(BF16), jnp.stack(gbs)


def _swa_head_perm():
    order = []
    for t in range(SW_HEADS // 2):
        order += list(range(t * SW_DH, (t + 1) * SW_DH))
        order += list(range((t + SW_HEADS // 2) * SW_DH, (t + SW_HEADS // 2 + 1) * SW_DH))
    return np.asarray(order)


def _rope_swap(w):
    nf = SW_DH // 4
    idx = np.arange(w.shape[1]).reshape(-1, 2, nf)[:, ::-1, :].reshape(-1)
    return w[:, idx]


def _prep_odd(w_in, w_out):
    o = np.cumsum((0, 256, 256, 512, 512, 16, 512, 128, 128, 512))
    mq, mk, mv, mg, gates, sq, sk, sv, sg = [w_in[:, o[i]:o[i + 1]] for i in range(9)]
    perm = _swa_head_perm()
    sq = sq[:, perm] * (SW_DH ** -0.5 * LOG2E)
    w = jnp.concatenate([mq, mk * ML_DQK ** -0.5, mv, mg, sq, _rope_swap(sq), sk, _rope_swap(sk),
                         sv, _pad_cols(gates[:, _gate_perm()], 128), sg[:, perm]], axis=1).astype(BF16)
    half = ML_HEADS * ML_DV
    w_o = jnp.concatenate([w_out[:half], w_out[half:][perm]], axis=0).astype(BF16)
    return w, w_o


def _rope_tables():
    nf = SW_DH // 4
    freqs = ROPE_BASE ** (-jnp.arange(nf, dtype=F32) / nf)
    pos = jnp.arange(SEQ)
    rows = (pos // GRID_W).astype(F32)
    cols = (pos % GRID_W).astype(F32)
    ar = rows[:, None] * freqs[None, :]
    ac = cols[:, None] * freqs[None, :]
    cos = jnp.concatenate([jnp.cos(ar), jnp.cos(ar), jnp.cos(ac), jnp.cos(ac)], axis=1)
    sin = jnp.concatenate([-jnp.sin(ar), jnp.sin(ar), -jnp.sin(ac), jnp.sin(ac)], axis=1)
    cos = jnp.concatenate([jnp.ones((CTX_LEN, SW_DH), F32), cos], axis=0)
    sin = jnp.concatenate([jnp.zeros((CTX_LEN, SW_DH), F32), sin], axis=0)
    return jnp.tile(cos, (1, 2)), jnp.tile(sin, (1, 2))


def _gate_perm():
    return np.arange(4 * ML_HEADS).reshape(2, 2, 2, 2).transpose(2, 0, 1, 3).reshape(-1)


def _gate_rows(grow_t):
    bsz = grow_t.shape[0]
    g = grow_t.reshape(bsz, 2, 2, 2, 2, N_CHUNKS, CHUNK)
    return g.transpose(0, 2, 3, 1, 5, 4, 6).reshape(bsz, 2, 2, 2, N_CHUNKS, 2 * CHUNK)


def kernel(x, c, ctx, c_ctx, w_ada, b_ada, ln_g, ln_b, w_in_even, w_out_even, gla_gate_w, gla_gate_b,
           gla_norm_g, na_rpb, w_in_odd, w_out_odd, ml_gate_b, ml_norm_g, sw_sink):
    bsz = x.shape[0]
    mod_rows = 16
    cvec = jnp.zeros((mod_rows, D_MODEL), F32).at[:bsz].set(c).at[bsz].set(c_ctx)
    ada = _ada(cvec, w_ada, b_ada)
    xs = (ctx, x)
    cos_t, sin_t = _rope_tables()
    for l in range(DEPTH):
        i = l // 2
        last = l == DEPTH - 1
        lat = ada[l, :bsz].reshape(bsz, 1, 3, D_MODEL)
        cx = jnp.broadcast_to(ada[l, bsz].reshape(1, 1, 3, D_MODEL), (bsz, 1, 3, D_MODEL))
        mod = jnp.concatenate([lat, cx], axis=1)
        if l % 2 == 0:
            w, gw, gb = _prep_even(w_in_even[i], gla_gate_w[i], gla_gate_b[i])
            gq, gk, gv, gg, nq, nk, nv, ng, lr = _inproj_even(xs, mod, w)
            ya = _gla(gq, gk, gv, gg, lr, gw, gb, gla_norm_g[i].reshape(GLA_HEADS // 2, 1, 256))
            yb = _na(nq, nk, nv, ng, _rpb_table(na_rpb[i]))
            w_o = w_out_even[i].astype(BF16)
        else:
            w, w_o = _prep_odd(w_in_odd[i], w_out_odd[i])
            gate_b = _pad_cols(ml_gate_b[i].reshape(1, 4 * ML_HEADS)[:, _gate_perm()], 128)
            mq, mk, mv, mg, sq, sk, sv, sg, gcol, grow_t = _inproj_odd(xs[0], mod, w, cos_t, sin_t, gate_b)
            grow = _gate_rows(grow_t)
            ya = _mlstm(mq, mk, mv, mg, gcol, grow, ml_norm_g[i].reshape(ML_HEADS // 2, 1, 256))
            yb = _swa(sq, sk, sv, sg, sw_sink[i])
        xs = (_outproj(ya, yb, xs, mod, w_o, ln_g[l].reshape(1, D_MODEL), ln_b[l].reshape(1, D_MODEL), last),)
    return xs[0]
```

```python
import functools
import math

import numpy as np
import jax
import jax.numpy as jnp
from jax import lax
from jax.experimental import pallas as pl
from jax.experimental.pallas import tpu as pltpu

F32 = jnp.float32
BF16 = jnp.bfloat16

D_MODEL = 1024
SEQ = 4096
DEPTH = 4
GRID_W = 64
CTX_LEN = 256
S_ALL = CTX_LEN + SEQ
ALPHA = (2.0 * DEPTH) ** 0.25
LN_EPS = 1e-5
NORM_EPS = 1e-6
NEG = -1e30
LOG2E = 1.4426950408889634
GLA_HEADS, GLA_DK, GLA_DV, GLA_RANK, GLA_TAU = 4, 64, 128, 16, 16.0
NA_HEADS, NA_DH, NA_KH, NA_KW = 8, 64, 8, 16
ML_HEADS, ML_DQK, ML_DV = 4, 64, 128
SW_HEADS, SW_KV_HEADS, SW_DH, SW_WINDOW, SW_BLOCK = 8, 2, 64, 128, 128
ROPE_BASE = 10000.0
CHUNK = 64
N_CHUNKS = S_ALL // CHUNK
CTX_CHUNKS = CTX_LEN // CHUNK
ROW_TILE = 256
N_ROW_TILES = S_ALL // ROW_TILE
GRID_ROWS = SEQ // GRID_W
OUT_SUB = 4
SCAN_SLOTS = 4
SCAN_LOOKAHEAD = 2
NA_UNROLL = 4
NA_LOOKAHEAD = 2
SW_LOOKAHEAD = 1

VMEM_LIMIT = 56 * 1024 * 1024

NT_DIMS = (((1,), (1,)), ((), ()))
TN_DIMS = (((0,), (0,)), ((), ()))


def _dot(a, b):
    return jnp.dot(a, b, preferred_element_type=F32)


def _dot_nt(a, b):
    return lax.dot_general(a, b, NT_DIMS, preferred_element_type=F32)


def _dot_tn(a, b):
    return lax.dot_general(a, b, TN_DIMS, preferred_element_type=F32)


def _silu(x):
    h = 0.5 * x
    return h + h * jnp.tanh(h)


def _log_sigmoid(x):
    return jnp.minimum(x, 0.0) - jnp.log(1.0 + jnp.exp(-jnp.abs(x)))


def _cumsum_rows(x, reverse):
    n = x.shape[0]
    row = lax.broadcasted_iota(jnp.int32, x.shape, 0)
    k = 1
    while k < n:
        if reverse:
            x = x + jnp.where(row < n - k, pltpu.roll(x, n - k, axis=0), 0.0)
        else:
            x = x + jnp.where(row >= k, pltpu.roll(x, k, axis=0), 0.0)
        k *= 2
    return x


def _split_dot(mat_bf16, x):
    hi = x.astype(BF16)
    lo = (x - hi.astype(F32)).astype(BF16)
    return _dot(mat_bf16, hi) + _dot(mat_bf16, lo)


def _split_dot_r(x, mat_bf16):
    hi = x.astype(BF16)
    lo = (x - hi.astype(F32)).astype(BF16)
    return _dot(hi, mat_bf16) + _dot(lo, mat_bf16)


def _halves_to_tiles(x):
    half0 = lax.broadcasted_iota(jnp.int32, x.shape, 1) < 64
    sw = pltpu.roll(x, 64, axis=1)
    return jnp.concatenate([jnp.where(half0, x, sw), jnp.where(half0, sw, x)], axis=1)


def _params(*sem):
    return pltpu.CompilerParams(dimension_semantics=sem, vmem_limit_bytes=VMEM_LIMIT)


def _ada_kernel(c_ref, w_ref, b_ref, o_ref):
    cond = _silu(c_ref[...])
    w = w_ref[0]
    c_hi = cond.astype(BF16)
    c_lo = (cond - c_hi.astype(F32)).astype(BF16)
    w_hi = w.astype(BF16)
    w_lo = (w - w_hi.astype(F32)).astype(BF16)
    o_ref[0] = _dot(c_hi, w_hi) + _dot(c_hi, w_lo) + _dot(c_lo, w_hi) + b_ref[0]


def _ada(cvec, w_ada, b_ada):
    rows = cvec.shape[0]
    nblk = 3
    return pl.pallas_call(
        _ada_kernel,
        out_shape=jax.ShapeDtypeStruct((DEPTH, rows, 3 * D_MODEL), F32),
        grid=(DEPTH, nblk),
        in_specs=[
            pl.BlockSpec((rows, D_MODEL), lambda l, n: (0, 0)),
            pl.BlockSpec((1, D_MODEL, D_MODEL), lambda l, n: (l, 0, n)),
            pl.BlockSpec((1, 1, D_MODEL), lambda l, n: (l, 0, n)),
        ],
        out_specs=pl.BlockSpec((1, rows, D_MODEL), lambda l, n: (l, 0, n)),
        compiler_params=_params("arbitrary", "arbitrary"),
        name="ada_mod",
    )(cvec, w_ada, b_ada.reshape(DEPTH, 1, 3 * D_MODEL))


def _row_tile(x_refs, is_ctx):
    if len(x_refs) == 1:
        return x_refs[0][0]
    return jnp.where(is_ctx, x_refs[0][0], x_refs[1][0])


def _x_specs(xs, tile_of):
    def combined(*ids):
        b, j = tile_of(*ids)
        return (b, j, 0)

    def context(*ids):
        return (tile_of(*ids)[0], 0, 0)

    def latent(*ids):
        b, j = tile_of(*ids)
        return (b, jnp.maximum(j - 1, 0), 0)

    if len(xs) == 1:
        return [pl.BlockSpec((1, ROW_TILE, D_MODEL), combined)]
    return [pl.BlockSpec((1, ROW_TILE, D_MODEL), context), pl.BlockSpec((1, ROW_TILE, D_MODEL), latent)]


def _modulated(x_refs, mod_ref):
    shift = mod_ref[0, 0, 0:1, :]
    scale = mod_ref[0, 0, 1:2, :]
    x = _row_tile(x_refs, pl.program_id(1) == 0)
    return (x * (1.0 + scale) + shift).astype(BF16)


EVEN_SLABS = (256, 256, 512, 512, 512, 512, 512, 512)


def _inproj_even_kernel(*refs, n_x):
    x_refs, (mod_ref, w_ref), outs = refs[:n_x], refs[n_x:n_x + 2], refs[n_x + 2:]
    h = _modulated(x_refs, mod_ref)
    off = 0
    for ref, width in zip(outs[:-1], EVEN_SLABS):
        ref[0] = _dot(h, w_ref[:, off:off + width]).astype(ref.dtype)
        off += width
    lr = _dot(h, w_ref[:, off:off + 128])
    outs[-1][0] = lr[:, :2 * GLA_RANK]


ODD_SLABS = (256, 256, 512, 512)


def _inproj_odd_kernel(x_ref, mod_ref, w_ref, cos_ref, sin_ref, gb_ref,
                       mq, mk, mv, mg, sq, sk, sv, sg, gcol, grow_t):
    h = _modulated((x_ref,), mod_ref)
    off = 0
    for ref, width in zip((mq, mk, mv, mg), ODD_SLABS):
        ref[0] = _dot(h, w_ref[:, off:off + width]).astype(ref.dtype)
        off += width
    cos = cos_ref[...]
    sin = sin_ref[...]
    a = _dot(h, w_ref[:, off:off + 512])
    b = _dot(h, w_ref[:, off + 512:off + 1024])
    for t in range(4):
        cols = slice(t * 128, (t + 1) * 128)
        sq[0, :, cols] = (a[:, cols] * cos + b[:, cols] * sin).astype(BF16)
    off += 1024
    kvg = _dot(h, w_ref[:, off:off + 512])
    sk[0] = (kvg[:, 0:128] * cos + kvg[:, 128:256] * sin).astype(BF16)
    sv[0] = kvg[:, 256:384].astype(BF16)
    g = kvg[:, 384:512] + gb_ref[...]
    for p in range(ML_HEADS // 2):
        gcol[0, p] = g[:, p * 8:(p + 1) * 8]
    grow_t[0] = g.T[0:4 * ML_HEADS, :]
    off += 512
    sg[0] = _dot(h, w_ref[:, off:off + 512]).astype(BF16)


def _mod_index(b, j):
    return (b, jnp.where(j == 0, 1, 0), 0, 0)


def _inproj_even(xs, mod, w):
    bsz = xs[0].shape[0]
    ntot = w.shape[1]
    out_shape = [jax.ShapeDtypeStruct((bsz, S_ALL, wd), BF16) for wd in EVEN_SLABS]
    out_shape.append(jax.ShapeDtypeStruct((bsz, S_ALL, 2 * GLA_RANK), F32))
    out_specs = [pl.BlockSpec((1, ROW_TILE, wd), lambda b, j: (b, j, 0)) for wd in EVEN_SLABS]
    out_specs.append(pl.BlockSpec((1, ROW_TILE, 2 * GLA_RANK), lambda b, j: (b, j, 0)))
    return pl.pallas_call(
        functools.partial(_inproj_even_kernel, n_x=len(xs)),
        out_shape=out_shape,
        grid=(bsz, N_ROW_TILES),
        in_specs=_x_specs(xs, lambda b, j: (b, j)) + [
            pl.BlockSpec((1, 1, 3, D_MODEL), _mod_index),
            pl.BlockSpec((D_MODEL, ntot), lambda b, j: (0, 0)),
        ],
        out_specs=out_specs,
        compiler_params=_params("parallel", "arbitrary"),
        name="inproj_even",
    )(*xs, mod, w)


def _inproj_odd(xc, mod, w, cos_t, sin_t, gate_b):
    bsz = xc.shape[0]
    ntot = w.shape[1]
    widths = (256, 256, 512, 512, 512, 128, 128, 512)
    out_shape = [jax.ShapeDtypeStruct((bsz, S_ALL, wd), BF16) for wd in widths]
    out_shape.append(jax.ShapeDtypeStruct((bsz, ML_HEADS // 2, S_ALL, 8), F32))
    out_shape.append(jax.ShapeDtypeStruct((bsz, 4 * ML_HEADS, S_ALL), F32))
    out_specs = [pl.BlockSpec((1, ROW_TILE, wd), lambda b, j: (b, j, 0)) for wd in widths]
    out_specs.append(pl.BlockSpec((1, ML_HEADS // 2, ROW_TILE, 8), lambda b, j: (b, 0, j, 0)))
    out_specs.append(pl.BlockSpec((1, 4 * ML_HEADS, ROW_TILE), lambda b, j: (b, 0, j)))
    return pl.pallas_call(
        _inproj_odd_kernel,
        out_shape=out_shape,
        grid=(bsz, N_ROW_TILES),
        in_specs=[
            pl.BlockSpec((1, ROW_TILE, D_MODEL), lambda b, j: (b, j, 0)),
            pl.BlockSpec((1, 1, 3, D_MODEL), _mod_index),
            pl.BlockSpec((D_MODEL, ntot), lambda b, j: (0, 0)),
            pl.BlockSpec((ROW_TILE, 128), lambda b, j: (j, 0)),
            pl.BlockSpec((ROW_TILE, 128), lambda b, j: (j, 0)),
            pl.BlockSpec((1, 128), lambda b, j: (0, 0)),
        ],
        out_specs=out_specs,
        compiler_params=_params("parallel", "arbitrary"),
        name="inproj_odd",
    )(xc, mod, w, cos_t, sin_t, gate_b)


def _outproj_kernel(*refs, n_x, n_tiles, n_sub):
    ya_refs, yb_refs = refs[:n_sub], refs[n_sub:2 * n_sub]
    x_refs = refs[2 * n_sub:2 * n_sub + n_sub * n_x]
    mod_refs = refs[2 * n_sub + n_sub * n_x:3 * n_sub + n_sub * n_x]
    w_ref, g_ref, b_ref = refs[3 * n_sub + n_sub * n_x:3 * n_sub + n_sub * n_x + 3]
    o_ref, r_scr = refs[-2:]
    t = pl.program_id(0)

    @pl.when(t == 0)
    def _():
        r_scr[...] = jnp.zeros(r_scr.shape, F32)

    for s in range(n_sub):
        r = r_scr[s]
        mu = jnp.mean(r, axis=-1, keepdims=True)
        d = r - mu
        var = jnp.mean(d * d, axis=-1, keepdims=True)
        o_ref[s * ROW_TILE:(s + 1) * ROW_TILE, :] = d * lax.rsqrt(var + LN_EPS) * g_ref[...] + b_ref[...]

    half = ya_refs[0].shape[2]
    ya = jnp.concatenate([ref[0] for ref in ya_refs], axis=0)
    yb = jnp.concatenate([ref[0] for ref in yb_refs], axis=0)
    y = _dot(ya, w_ref[0:half, :]) + _dot(yb, w_ref[half:2 * half, :])
    for s in range(n_sub):
        x = _row_tile(x_refs[s * n_x:(s + 1) * n_x], lax.rem(n_sub * t + s, n_tiles) == 0)
        r_scr[s] = ALPHA * x + mod_refs[s][0, 0, 2:3, :] * y[s * ROW_TILE:(s + 1) * ROW_TILE]


def _outproj(ya, yb, xs, mod, w, ln_g, ln_b, latent_only):
    bsz = xs[0].shape[0]
    skip = 1 if latent_only else 0
    n_tiles = N_ROW_TILES - skip
    total = bsz * n_tiles
    n_sub = math.gcd(total, OUT_SUB)
    half = ya.shape[2]

    def tile_of(s):
        def fn(t):
            tt = jnp.minimum(n_sub * t + s, total - 1)
            return tt // n_tiles, lax.rem(tt, n_tiles) + skip
        return fn

    def y_index(s):
        def fn(t):
            b, j = tile_of(s)(t)
            return (b, j, 0)
        return fn

    def mod_index(s):
        return lambda t: _mod_index(*tile_of(s)(t))

    subs = range(n_sub)
    const = lambda t: (0, 0)
    x_specs = [spec for s in subs for spec in _x_specs(xs, tile_of(s))]
    out = pl.pallas_call(
        functools.partial(_outproj_kernel, n_x=len(xs), n_tiles=n_tiles, n_sub=n_sub),
        out_shape=jax.ShapeDtypeStruct((total * ROW_TILE, D_MODEL), F32),
        grid=(total // n_sub + 1,),
        in_specs=[pl.BlockSpec((1, ROW_TILE, half), y_index(s)) for s in subs]
        + [pl.BlockSpec((1, ROW_TILE, half), y_index(s)) for s in subs]
        + x_specs
        + [pl.BlockSpec((1, 1, 3, D_MODEL), mod_index(s)) for s in subs]
        + [pl.BlockSpec((2 * half, D_MODEL), const),
           pl.BlockSpec((1, D_MODEL), const),
           pl.BlockSpec((1, D_MODEL), const)],
        out_specs=pl.BlockSpec((n_sub * ROW_TILE, D_MODEL), lambda t: (jnp.maximum(t - 1, 0), 0)),
        scratch_shapes=[pltpu.VMEM((n_sub, ROW_TILE, D_MODEL), F32)],
        compiler_params=_params("arbitrary"),
        name="outproj_ln",
    )(*([ya] * n_sub), *([yb] * n_sub), *(list(xs) * n_sub), *([mod] * n_sub), w, ln_g, ln_b)
    return out.reshape(bsz, n_tiles * ROW_TILE, D_MODEL)


def _for_row_tiles(tile_fn):
    def body(i, carry):
        tile_fn(2 * i)
        tile_fn(2 * i + 1)
        return carry

    lax.fori_loop(0, N_ROW_TILES // 2, body, 0)
    for t in range(N_ROW_TILES - N_ROW_TILES % 2, N_ROW_TILES):
        tile_fn(t)


def _scan_epilogue(of_scr, ob_scr, gate_ref, ng_ref, out_ref):
    def tile(t):
        rows = pl.ds(pl.multiple_of(t * ROW_TILE, ROW_TILE), ROW_TILE)
        o = of_scr[rows, :] + ob_scr[rows, :]
        parts = []
        for hh in range(2):
            oh = o[:, hh * 128:(hh + 1) * 128]
            ms = jnp.mean(oh * oh, axis=-1, keepdims=True)
            parts.append(oh * lax.rsqrt(ms + NORM_EPS))
        y = jnp.concatenate(parts, axis=1) * ng_ref[0]
        out_ref[0, rows, :] = (y * _silu(gate_ref[0, rows, :].astype(F32))).astype(BF16)

    _for_row_tiles(tile)


def _bwd_chunk(i):
    return jnp.where(i < CTX_CHUNKS, CTX_CHUNKS - 1 - i, N_CHUNKS - 1 + CTX_CHUNKS - i)


def _scan_pipeline(local_fn, state_fn):
    for i in range(SCAN_LOOKAHEAD):
        local_fn(i, 0, i)
        local_fn(_bwd_chunk(i), 1, i)

    def body(j, carry):
        for u in range(SCAN_SLOTS):
            i = SCAN_SLOTS * j + u
            nxt = jnp.minimum(i + SCAN_LOOKAHEAD, N_CHUNKS - 1)
            local_fn(nxt, 0, (u + SCAN_LOOKAHEAD) % SCAN_SLOTS)
            local_fn(_bwd_chunk(nxt), 1, (u + SCAN_LOOKAHEAD) % SCAN_SLOTS)
            state_fn(i, 0, u)
            state_fn(_bwd_chunk(i), 1, u)
        return carry

    lax.fori_loop(0, N_CHUNKS // SCAN_SLOTS, body, 0)


def _gla_kernel(q_ref, k_ref, v_ref, gate_ref, lr_ref, gw_ref, gb_ref, ng_ref, out_ref,
                g_scr, of_scr, ob_scr, st_scr, a_scr, qin_scr, u_scr, dl_scr):
    def pre(t):
        rows = pl.ds(pl.multiple_of(t * ROW_TILE, ROW_TILE), ROW_TILE)
        z = _dot(lr_ref[0, rows, :].astype(BF16), gw_ref[0]) + gb_ref[0]
        g_scr[rows, :] = _log_sigmoid(z) * (LOG2E / GLA_TAU)

    _for_row_tiles(pre)
    st_scr[...] = jnp.zeros(st_scr.shape, F32)

    def local(cc, d, slot):
        rows = pl.ds(pl.multiple_of(cc * CHUNK, CHUNK), CHUNK)
        g = g_scr[rows, d * 128:(d + 1) * 128]
        b = _cumsum_rows(g, reverse=(d == 1))
        b_last = b[0:1, :] if d == 1 else b[CHUNK - 1:CHUNK, :]
        q = q_ref[0, rows, :].astype(F32)
        k = k_ref[0, rows, :].astype(F32)
        q_in = q * jnp.exp2(b)
        k_in = (k * jnp.exp2(-b)).astype(BF16)
        k_dec = (k * jnp.exp2(b_last - b)).astype(BF16)

        head0 = lax.broadcasted_iota(jnp.int32, (CHUNK, 128), 1) < GLA_DK
        q_stack = jnp.concatenate([jnp.where(head0, q_in, 0.0), jnp.where(head0, 0.0, q_in)],
                                  axis=0).astype(BF16)
        a = _dot_nt(q_stack, k_in)
        t_i = lax.broadcasted_iota(jnp.int32, (2 * CHUNK, CHUNK), 0) & (CHUNK - 1)
        s_i = lax.broadcasted_iota(jnp.int32, (2 * CHUNK, CHUNK), 1)
        keep = (s_i >= t_i) if d == 1 else (s_i <= t_i)
        a_scr[slot, d] = jnp.where(keep, a, 0.0).astype(BF16)
        qin_scr[slot, d] = q_in.astype(BF16)

        u = _dot_tn(v_ref[0, rows, :], k_dec)
        r_i = lax.broadcasted_iota(jnp.int32, u.shape, 0) < GLA_DV
        c_i = lax.broadcasted_iota(jnp.int32, u.shape, 1) < GLA_DK
        u_scr[slot, d] = jnp.where(r_i == c_i, u, 0.0)
        dl_scr[slot, d] = jnp.exp2(b_last)

    def state(cc, d, slot):
        rows = pl.ds(pl.multiple_of(cc * CHUNK, CHUNK), CHUNK)
        v = v_ref[0, rows, :]
        a = a_scr[slot, d]
        vhead0 = lax.broadcasted_iota(jnp.int32, (CHUNK, 2 * GLA_DV), 1) < GLA_DV
        zero = jnp.zeros_like(v)
        st = st_scr[d]
        o = (_dot(a[:CHUNK], jnp.where(vhead0, v, zero))
             + _dot(a[CHUNK:], jnp.where(vhead0, zero, v))
             + _dot_nt(qin_scr[slot, d], st.astype(BF16)))
        if d == 0:
            of_scr[rows, :] = o
        else:
            ob_scr[rows, :] = o
        st_scr[d] = st * dl_scr[slot, d] + u_scr[slot, d]

    _scan_pipeline(local, state)
    _scan_epilogue(of_scr, ob_scr, gate_ref, ng_ref, out_ref)


def _gla(q, k, v, gate, lr, gw, gb, ng):
    bsz = q.shape[0]
    pairs = GLA_HEADS // 2
    return pl.pallas_call(
        _gla_kernel,
        out_shape=jax.ShapeDtypeStruct((bsz, S_ALL, GLA_HEADS * GLA_DV), BF16),
        grid=(bsz, pairs),
        in_specs=[
            pl.BlockSpec((1, S_ALL, 128), lambda b, p: (b, 0, p)),
            pl.BlockSpec((1, S_ALL, 128), lambda b, p: (b, 0, p)),
            pl.BlockSpec((1, S_ALL, 256), lambda b, p: (b, 0, p)),
            pl.BlockSpec((1, S_ALL, 256), lambda b, p: (b, 0, p)),
            pl.BlockSpec((1, S_ALL, 2 * GLA_RANK), lambda b, p: (b, 0, 0)),
            pl.BlockSpec((1, 2 * GLA_RANK, 256), lambda b, p: (p, 0, 0)),
            pl.BlockSpec((1, 1, 256), lambda b, p: (p, 0, 0)),
            pl.BlockSpec((1, 1, 256), lambda b, p: (p, 0, 0)),
        ],
        out_specs=pl.BlockSpec((1, S_ALL, 256), lambda b, p: (b, 0, p)),
        scratch_shapes=[
            pltpu.VMEM((S_ALL, 256), F32),
            pltpu.VMEM((S_ALL, 256), F32),
            pltpu.VMEM((S_ALL, 256), F32),
            pltpu.VMEM((2, 2 * GLA_DV, 2 * GLA_DK), F32),
            pltpu.VMEM((SCAN_SLOTS, 2, 2 * CHUNK, CHUNK), BF16),
            pltpu.VMEM((SCAN_SLOTS, 2, CHUNK, 2 * GLA_DK), BF16),
            pltpu.VMEM((SCAN_SLOTS, 2, 2 * GLA_DV, 2 * GLA_DK), F32),
            pltpu.VMEM((SCAN_SLOTS, 2, 1, 2 * GLA_DK), F32),
        ],
        compiler_params=_params("parallel", "arbitrary"),
        name="gla_scan",
    )(q, k, v, gate, lr, gw, gb, ng)


def _cummax_rows_chunked(x, reverse):
    n = x.shape[0]
    pos = lax.broadcasted_iota(jnp.int32, x.shape, 0) & (CHUNK - 1)
    k = 1
    while k < CHUNK:
        if reverse:
            x = jnp.maximum(x, jnp.where(pos < CHUNK - k, pltpu.roll(x, n - k, axis=0), NEG))
        else:
            x = jnp.maximum(x, jnp.where(pos >= k, pltpu.roll(x, k, axis=0), NEG))
        k *= 2
    return x


def _mlstm_kernel(q_ref, k_ref, v_ref, gate_ref, gcol_ref, grow_ref, ng_ref, out_ref,
                  bcm_scr, rcm_scr, cmx_scr, rrow_scr, blast_scr, wsmax_scr, blast2_scr, wsmax2_scr,
                  of_scr, ob_scr, st_scr, m2_scr, nd_scr, u_scr):
    lane128 = lax.broadcasted_iota(jnp.int32, (1, 128), 1)
    half0_row = lane128 < CHUNK

    li = lax.broadcasted_iota(jnp.int32, (128, 128), 0)
    lj = lax.broadcasted_iota(jnp.int32, (128, 128), 1)
    same_half = (li < CHUNK) == (lj < CHUNK)
    ones_half = jnp.where(same_half, 1.0, 0.0).astype(BF16)
    for d in range(2):
        within = (li >= lj) if d == 1 else (li <= lj)
        tri = jnp.where(same_half & within, 1.0, 0.0).astype(BF16)
        lf = _log_sigmoid(grow_ref[0, d, 1, 0]) * LOG2E
        b_row = _split_dot_r(lf, tri)
        total = _split_dot_r(lf, ones_half)
        r_row = grow_ref[0, d, 0, 0] * LOG2E - b_row
        rrow_scr[d] = r_row
        blast_scr[d] = total
        mx0 = jnp.max(jnp.where(half0_row, r_row, NEG), axis=-1, keepdims=True)
        mx1 = jnp.max(jnp.where(half0_row, NEG, r_row), axis=-1, keepdims=True)
        wsmax = total + jnp.where(half0_row, mx0, mx1)
        wsmax_scr[d] = wsmax
        blast2_scr[d] = _halves_to_tiles(total)
        wsmax2_scr[d] = _halves_to_tiles(wsmax)

    ti = lax.broadcasted_iota(jnp.int32, (ROW_TILE, ROW_TILE), 0)
    tj = lax.broadcasted_iota(jnp.int32, (ROW_TILE, ROW_TILE), 1)
    same_chunk = (ti >> 6) == (tj >> 6)
    tril = jnp.where(same_chunk & (tj <= ti), 1.0, 0.0).astype(BF16)
    triu = jnp.where(same_chunk & (tj >= ti), 1.0, 0.0).astype(BF16)
    half0_tile = lax.broadcasted_iota(jnp.int32, (ROW_TILE, 128), 1) < CHUNK

    def pre(t):
        rows = pl.ds(pl.multiple_of(t * ROW_TILE, ROW_TILE), ROW_TILE)
        gc = gcol_ref[0, 0, rows, :]
        lf = _log_sigmoid(gc) * LOG2E
        gi = gc * LOG2E
        for d in range(2):
            bcol = _split_dot(triu if d == 1 else tril, lf)
            b0 = jnp.broadcast_to(bcol[:, d * 4 + 2:d * 4 + 3], (ROW_TILE, 128))
            b1 = jnp.broadcast_to(bcol[:, d * 4 + 3:d * 4 + 4], (ROW_TILE, 128))
            i0 = jnp.broadcast_to(gi[:, d * 4:d * 4 + 1], (ROW_TILE, 128))
            i1 = jnp.broadcast_to(gi[:, d * 4 + 1:d * 4 + 2], (ROW_TILE, 128))
            bc = jnp.where(half0_tile, b0, b1)
            rc = jnp.where(half0_tile, i0, i1) - bc
            bcm_scr[d, rows, :] = bc
            rcm_scr[d, rows, :] = rc
            cmx_scr[d, rows, :] = _cummax_rows_chunked(rc, reverse=(d == 1))

    _for_row_tiles(pre)
    st_scr[...] = jnp.zeros(st_scr.shape, F32)
    m2_scr[...] = jnp.zeros(m2_scr.shape, F32)

    def local(cc, d, slot):
        rows = pl.ds(pl.multiple_of(cc * CHUNK, CHUNK), CHUNK)
        rc = rcm_scr[d, rows, :]
        cmx = cmx_scr[d, rows, :]
        rr = rrow_scr[d, pl.ds(cc, 1), :]
        bl = blast_scr[d, pl.ds(cc, 1), :]
        wm = wsmax_scr[d, pl.ds(cc, 1), :]
        q = q_ref[0, rows, :]
        k = k_ref[0, rows, :]
        v = v_ref[0, rows, :]

        half0 = lax.broadcasted_iota(jnp.int32, (CHUNK, 128), 1) < CHUNK
        t_i = lax.broadcasted_iota(jnp.int32, (CHUNK, 128), 0)
        s_i = lax.broadcasted_iota(jnp.int32, (CHUNK, 128), 1) & (CHUNK - 1)
        keep = (s_i >= t_i) if d == 1 else (s_i <= t_i)
        w = jnp.exp2(jnp.where(keep, rr - cmx, NEG))

        zero_k = jnp.zeros_like(k)
        k_stack = jnp.concatenate([jnp.where(half0, k, zero_k), jnp.where(half0, zero_k, k)], axis=0)
        s_w = (_dot_nt(q, k_stack) * w).astype(BF16)

        vhead0 = lax.broadcasted_iota(jnp.int32, (CHUNK, 2 * ML_DV), 1) < ML_DV
        zero_v = jnp.zeros_like(v)
        one_v = jnp.ones_like(v)
        ones0 = jnp.where(vhead0, 1.0, 0.0).astype(BF16)
        ones1 = jnp.where(vhead0, 0.0, 1.0).astype(BF16)
        vo_stack = jnp.concatenate(
            [jnp.concatenate([jnp.where(vhead0, v, zero_v), ones0], axis=1),
             jnp.concatenate([jnp.where(vhead0, zero_v, v), ones1], axis=1)],
            axis=0)
        nd_scr[slot, d] = _dot(s_w, vo_stack)

        kws = (k.astype(F32) * jnp.exp2(rc + (bl - wm))).astype(BF16)
        u = _dot_tn(kws, jnp.concatenate([v, one_v], axis=1))
        for blk in range(4):
            r0 = (blk // 2) * ML_DQK
            c0 = (blk % 2) * 2 * ML_DV + (blk // 2) * ML_DV
            u_scr[slot, d, blk] = u[r0:r0 + ML_DQK, c0:c0 + ML_DV]

    def state(cc, d, slot):
        rows = pl.ds(pl.multiple_of(cc * CHUNK, CHUNK), CHUNK)
        cmx2 = _halves_to_tiles(cmx_scr[d, rows, :])
        bc2 = _halves_to_tiles(bcm_scr[d, rows, :])
        m_old = m2_scr[d]
        g = jnp.maximum(cmx2, m_old)
        w_loc = jnp.exp2(cmx2 - g)
        w_int = jnp.exp2(m_old - g)
        floor = jnp.exp2(-(bc2 + g))

        blocks = [st_scr[d, blk] for blk in range(4)]
        zero = jnp.zeros((ML_DQK, ML_DV), BF16)
        cm0, n0, cm1, n1 = [x.astype(BF16) for x in blocks]
        st = jnp.concatenate([jnp.concatenate([cm0, zero, n0, zero], axis=1),
                              jnp.concatenate([zero, cm1, zero, n1], axis=1)], axis=0)
        sd = _dot(q_ref[0, rows, :], st)
        nd = nd_scr[slot, d]
        num = w_loc * nd[:, :2 * ML_DV] + w_int * sd[:, :2 * ML_DV]
        den = w_loc * nd[:, 2 * ML_DV:] + w_int * sd[:, 2 * ML_DV:]
        h_out = num / jnp.maximum(jnp.abs(den), floor)
        if d == 0:
            of_scr[rows, :] = h_out
        else:
            ob_scr[rows, :] = h_out

        bl = blast2_scr[d, pl.ds(cc, 1), :]
        wm = wsmax2_scr[d, pl.ds(cc, 1), :]
        m_new = jnp.maximum(bl + m_old, wm)
        dec = jnp.exp2(bl + m_old - m_new)
        ws_scale = jnp.exp2(wm - m_new)
        for blk in range(4):
            head = slice((blk // 2) * ML_DV, (blk // 2 + 1) * ML_DV)
            st_scr[d, blk] = blocks[blk] * dec[:, head] + u_scr[slot, d, blk] * ws_scale[:, head]
        m2_scr[d] = m_new

    _scan_pipeline(local, state)
    _scan_epilogue(of_scr, ob_scr, gate_ref, ng_ref, out_ref)


def _mlstm(q, k, v, gate, gcol, grow, ng):
    bsz = q.shape[0]
    pairs = ML_HEADS // 2
    return pl.pallas_call(
        _mlstm_kernel,
        out_shape=jax.ShapeDtypeStruct((bsz, S_ALL, ML_HEADS * ML_DV), BF16),
        grid=(bsz, pairs),
        in_specs=[
            pl.BlockSpec((1, S_ALL, 128), lambda b, p: (b, 0, p)),
            pl.BlockSpec((1, S_ALL, 128), lambda b, p: (b, 0, p)),
            pl.BlockSpec((1, S_ALL, 256), lambda b, p: (b, 0, p)),
            pl.BlockSpec((1, S_ALL, 256), lambda b, p: (b, 0, p)),
            pl.BlockSpec((1, 1, S_ALL, 8), lambda b, p: (b, p, 0, 0)),
            pl.BlockSpec((1, 2, 2, 1, N_CHUNKS, 128), lambda b, p: (b, 0, 0, p, 0, 0)),
            pl.BlockSpec((1, 1, 256), lambda b, p: (p, 0, 0)),
        ],
        out_specs=pl.BlockSpec((1, S_ALL, 256), lambda b, p: (b, 0, p)),
        scratch_shapes=[
            pltpu.VMEM((2, S_ALL, 128), F32),
            pltpu.VMEM((2, S_ALL, 128), F32),
            pltpu.VMEM((2, S_ALL, 128), F32),
            pltpu.VMEM((2, N_CHUNKS, 128), F32),
            pltpu.VMEM((2, N_CHUNKS, 128), F32),
            pltpu.VMEM((2, N_CHUNKS, 128), F32),
            pltpu.VMEM((2, N_CHUNKS, 2 * ML_DV), F32),
            pltpu.VMEM((2, N_CHUNKS, 2 * ML_DV), F32),
            pltpu.VMEM((S_ALL, 2 * ML_DV), F32),
            pltpu.VMEM((S_ALL, 2 * ML_DV), F32),
            pltpu.VMEM((2, 4, ML_DQK, ML_DV), F32),
            pltpu.VMEM((2, 1, 2 * ML_DV), F32),
            pltpu.VMEM((SCAN_SLOTS, 2, CHUNK, 4 * ML_DV), F32),
            pltpu.VMEM((SCAN_SLOTS, 2, 4, ML_DQK, ML_DV), F32),
        ],
        compiler_params=_params("parallel", "arbitrary"),
        name="mlstm_scan",
    )(q, k, v, gate, gcol, grow, ng)


def _rpb_table_kernel(rpb_ref, o_ref):
    h = pl.program_id(0)
    n_dc = 2 * NA_KW - 1
    qi = lax.broadcasted_iota(jnp.int32, (GRID_W, 128), 0)
    lane = lax.broadcasted_iota(jnp.int32, (GRID_W, 128), 1)
    wi = lane & (GRID_W - 1)
    second = lane >= GRID_W
    dc = jnp.clip(wi - qi + (NA_KW - 1), 0, n_dc - 1)
    cs = jnp.clip(qi - NA_KW // 2, 0, GRID_W - NA_KW)
    col_ok = (wi >= cs) & (wi < cs + NA_KW)
    for dr in range(2 * NA_KH - 2):
        base = h * ((2 * NA_KH - 1) * n_dc) + dr * n_dc
        acc = jnp.zeros((GRID_W, 128), F32)
        for c in range(n_dc):
            val = jnp.where(second, rpb_ref[base + n_dc + c], rpb_ref[base + c])
            acc = jnp.where(dc == c, val, acc)
        o_ref[0, dr] = jnp.where(col_ok, acc * LOG2E, NEG)


def _rpb_table(rpb):
    flat = rpb.reshape(-1)
    return pl.pallas_call(
        _rpb_table_kernel,
        out_shape=jax.ShapeDtypeStruct((NA_HEADS, 2 * NA_KH - 2, GRID_W, 128), F32),
        grid=(NA_HEADS,),
        in_specs=[pl.BlockSpec(memory_space=pltpu.SMEM)],
        out_specs=pl.BlockSpec((1, 2 * NA_KH - 2, GRID_W, 128), lambda h: (h, 0, 0, 0)),
        compiler_params=_params("arbitrary"),
        name="na_rpb_table",
    )(flat)


def _softmax_weights(s_parts, extra_logit=None):
    def lane_tiles(a):
        return [a[:, i:i + 128] for i in range(0, a.shape[1], 128)]

    m = functools.reduce(jnp.maximum, [t for s in s_parts for t in lane_tiles(s)]).max(axis=-1, keepdims=True)
    if extra_logit is not None:
        m = jnp.maximum(m, extra_logit)
    p_parts, p_tiles = [], []
    for s in s_parts:
        p = jnp.exp2(s - m)
        p_tiles += lane_tiles(p)
        p_parts.append(p.astype(BF16))
    l = functools.reduce(jnp.add, p_tiles).sum(axis=-1, keepdims=True)
    if extra_logit is not None:
        l = l + jnp.exp2(extra_logit - m)
    return p_parts, l


def _softmax_pv(s_parts, v_parts, extra_logit=None):
    p_parts, l = _softmax_weights(s_parts, extra_logit)
    o = None
    for p, v in zip(p_parts, v_parts):
        pv = _dot(p, v)
        o = pv if o is None else o + pv
    return o / l


def _stack_heads(x):
    head0 = lax.broadcasted_iota(jnp.int32, x.shape, 1) < 64
    zero = jnp.zeros_like(x)
    return jnp.concatenate([jnp.where(head0, x, zero), jnp.where(head0, zero, x)], axis=0)


def _merge_heads(o):
    n = o.shape[0] // 2
    head0 = lax.broadcasted_iota(jnp.int32, (n, o.shape[1]), 1) < 64
    return jnp.where(head0, o[:n], o[n:])


def _na_kernel(q_ref, k_ref, v_ref, gate_ref, tab_ref, out_ref, s_scr, sc_scr):
    kc = k_ref[0, 0:CTX_LEN, :]
    vc = v_ref[0, 0:CTX_LEN, :]
    n_lat = NA_KH * GRID_W
    rows_q = 2 * GRID_W
    n_groups = GRID_ROWS // NA_UNROLL

    def rows_of(r):
        rs = jnp.clip(r - NA_KH // 2, 0, GRID_ROWS - NA_KH)
        q_rows = pl.ds(pl.multiple_of(CTX_LEN + r * GRID_W, GRID_W), GRID_W)
        k_rows = pl.ds(pl.multiple_of(CTX_LEN + rs * GRID_W, GRID_W), n_lat)
        return rs, q_rows, k_rows

    def window_scores(r, slot):
        rs, q_rows, k_rows = rows_of(r)
        dr0 = rs - r + NA_KH - 1
        qs = _stack_heads(q_ref[0, q_rows, :])
        bias = jnp.concatenate(
            [jnp.concatenate([tab_ref[hh, dr0 + 2 * j] for j in range(NA_KH // 2)], axis=1)
             for hh in range(2)], axis=0)
        s_scr[slot] = _dot_nt(qs, k_ref[0, k_rows, :]) + bias

    def context_scores(g, slot):
        qs = jnp.concatenate([_stack_heads(q_ref[0, rows_of(g * NA_UNROLL + u)[1], :])
                              for u in range(NA_UNROLL)], axis=0)
        sc_scr[slot] = _dot_nt(qs, kc)

    context_scores(0, 0)
    for r in range(NA_LOOKAHEAD):
        window_scores(r, r)

    def body(i, carry):
        for g_slot in range(2):
            g = 2 * i + g_slot
            context_scores(jnp.minimum(g + 1, n_groups - 1), 1 - g_slot)
            for u in range(NA_UNROLL):
                r = g * NA_UNROLL + u
                window_scores(jnp.minimum(r + NA_LOOKAHEAD, GRID_ROWS - 1), (u + NA_LOOKAHEAD) % NA_UNROLL)
                _, q_rows, k_rows = rows_of(r)
                s_ctx = sc_scr[g_slot, u * rows_q:(u + 1) * rows_q, :]
                o = _merge_heads(_softmax_pv([s_scr[u], s_ctx], [v_ref[0, k_rows, :], vc]))
                out_ref[0, q_rows, :] = (o * _silu(gate_ref[0, q_rows, :].astype(F32))).astype(BF16)
        return carry

    lax.fori_loop(0, n_groups // 2, body, 0)

    for t in range(CTX_LEN // 128):
        rows = slice(t * 128, (t + 1) * 128)
        qs = _stack_heads(q_ref[0, rows, :])
        o = _merge_heads(_softmax_pv([_dot_nt(qs, kc)], [vc]))
        out_ref[0, rows, :] = (o * _silu(gate_ref[0, rows, :].astype(F32))).astype(BF16)


def _na(q, k, v, gate, table):
    bsz = q.shape[0]
    pairs = NA_HEADS // 2
    seq_spec = pl.BlockSpec((1, S_ALL, 128), lambda b, p: (b, 0, p))
    return pl.pallas_call(
        _na_kernel,
        out_shape=jax.ShapeDtypeStruct((bsz, S_ALL, NA_HEADS * NA_DH), BF16),
        grid=(bsz, pairs),
        in_specs=[seq_spec, seq_spec, seq_spec, seq_spec,
                  pl.BlockSpec((2, 2 * NA_KH - 2, GRID_W, 128), lambda b, p: (p, 0, 0, 0))],
        out_specs=seq_spec,
        scratch_shapes=[pltpu.VMEM((NA_UNROLL, 2 * GRID_W, NA_KH * GRID_W), F32),
                        pltpu.VMEM((2, NA_UNROLL * 2 * GRID_W, CTX_LEN), F32)],
        compiler_params=_params("parallel", "arbitrary"),
        name="na_attn",
    )(q, k, v, gate, table)


def _swa_mask_table():
    span = SW_BLOCK + 2 * SW_WINDOW
    r = np.arange(SW_BLOCK)[:, None]
    c = np.arange(span)[None, :]
    cases = [np.abs(r + off - c) <= SW_WINDOW for off in (0, SW_WINDOW, 2 * SW_WINDOW)]
    cases.append(np.zeros((SW_BLOCK, span), bool))
    return jnp.asarray(np.where(np.stack(cases), 0.0, NEG), F32)


def _swa_kernel(sink_ref, q_ref, k_ref, v_ref, gate_ref, mask_ref, out_ref, s_scr):
    n_blocks = S_ALL // SW_BLOCK
    ctx_blocks = CTX_LEN // SW_BLOCK
    n_tiles = SW_HEADS // 2
    span = SW_BLOCK + 2 * SW_WINDOW
    kc = k_ref[0, 0:CTX_LEN, :]
    vc = v_ref[0, 0:CTX_LEN, :]
    first = lax.broadcasted_iota(jnp.int32, (2 * SW_BLOCK, 1), 0) < SW_BLOCK

    def rows_of(n):
        qstart = (n - ctx_blocks) * SW_BLOCK
        kstart = jnp.clip(qstart - SW_WINDOW, 0, SEQ - span)
        q_rows = pl.ds(pl.multiple_of(n * SW_BLOCK, SW_BLOCK), SW_BLOCK)
        k_rows = pl.ds(pl.multiple_of(CTX_LEN + kstart, SW_BLOCK), span)
        return q_rows, k_rows

    def scores(n, t, slot):
        q_rows, k_rows = rows_of(n)
        case = jnp.where(n < ctx_blocks, 3,
                         jnp.where(n == ctx_blocks, 0, jnp.where(n == n_blocks - 1, 2, 1)))
        qs = _stack_heads(q_ref[0, q_rows, t * 128:(t + 1) * 128])
        mb = mask_ref[case]
        s_scr[slot, :, 0:span] = _dot_nt(qs, k_ref[0, k_rows, :]) + jnp.concatenate([mb, mb], axis=0)
        s_scr[slot, :, span:span + CTX_LEN] = _dot_nt(qs, kc)

    def finish(n, t, slot):
        q_rows, k_rows = rows_of(n)
        cols = slice(t * 128, (t + 1) * 128)
        sink = jnp.where(first, sink_ref[t], sink_ref[t + n_tiles]) * LOG2E
        o = _merge_heads(_softmax_pv([s_scr[slot, :, 0:span], s_scr[slot, :, span:span + CTX_LEN]],
                                     [v_ref[0, k_rows, :], vc], extra_logit=sink))
        out_ref[0, q_rows, cols] = (o * _silu(gate_ref[0, q_rows, cols].astype(F32))).astype(BF16)

    for t in range(SW_LOOKAHEAD):
        scores(0, t, t)

    def body(n, carry):
        for t in range(n_tiles):
            ahead = t + SW_LOOKAHEAD
            if ahead < n_tiles:
                scores(n, ahead, ahead)
            else:
                scores(jnp.minimum(n + 1, n_blocks - 1), ahead - n_tiles, ahead - n_tiles)
            finish(n, t, t)
        return carry

    lax.fori_loop(0, n_blocks, body, 0)


def _swa(q, k, v, gate, sink):
    bsz = q.shape[0]
    width = SW_HEADS * SW_DH
    span = SW_BLOCK + 2 * SW_WINDOW
    seq_spec = pl.BlockSpec((1, S_ALL, width), lambda b: (b, 0, 0))
    kv_spec = pl.BlockSpec((1, S_ALL, 128), lambda b: (b, 0, 0))
    return pl.pallas_call(
        _swa_kernel,
        out_shape=jax.ShapeDtypeStruct((bsz, S_ALL, width), BF16),
        grid=(bsz,),
        in_specs=[pl.BlockSpec(memory_space=pltpu.SMEM), seq_spec, kv_spec, kv_spec, seq_spec,
                  pl.BlockSpec((4, SW_BLOCK, span), lambda b: (0, 0, 0))],
        out_specs=seq_spec,
        scratch_shapes=[pltpu.VMEM((SW_HEADS // 2, 2 * SW_BLOCK, span + CTX_LEN), F32)],
        compiler_params=_params("parallel"),
        name="swa_attn",
    )(sink, q, k, v, gate, _swa_mask_table())


def _pad_cols(w, width):
    return jnp.pad(w, ((0, 0), (0, width - w.shape[1])))


def _prep_even(w_in, gate_w, gate_b):
    o = np.cumsum((0, 256, 256, 512, 512, 32, 512, 512, 512, 512))
    gq, gk, gv, gg, lra, nq, nk, nv, ng = [w_in[:, o[i]:o[i + 1]] for i in range(9)]
    w = jnp.concatenate([gq * GLA_DK ** -0.5, gk, gv, gg, nq * (NA_DH ** -0.5 * LOG2E), nk, nv, ng,
                         _pad_cols(lra, 128)], axis=1).astype(BF16)
    gws, gbs = [], []
    for p in range(GLA_HEADS // 2):
        cols = slice(p * 128, (p + 1) * 128)
        zero = jnp.zeros((GLA_RANK, 128), F32)
        gws.append(jnp.concatenate([jnp.concatenate([gate_w[0][:, cols], zero], axis=1),
                                    jnp.concatenate([zero, gate_w[1][:, cols]], axis=1)], axis=0))
        gbs.append(jnp.concatenate([gate_b[0][cols], gate_b[1][cols]])[None])
    return w, jnp.stack(gws).astype(BF16), jnp.stack(gbs)


def _swa_head_perm():
    order = []
    for t in range(SW_HEADS // 2):
        order += list(range(t * SW_DH, (t + 1) * SW_DH))
        order += list(range((t + SW_HEADS // 2) * SW_DH, (t + SW_HEADS // 2 + 1) * SW_DH))
    return np.asarray(order)


def _rope_swap(w):
    nf = SW_DH // 4
    idx = np.arange(w.shape[1]).reshape(-1, 2, nf)[:, ::-1, :].reshape(-1)
    return w[:, idx]


def _prep_odd(w_in, w_out):
    o = np.cumsum((0, 256, 256, 512, 512, 16, 512, 128, 128, 512))
    mq, mk, mv, mg, gates, sq, sk, sv, sg = [w_in[:, o[i]:o[i + 1]] for i in range(9)]
    perm = _swa_head_perm()
    sq = sq[:, perm] * (SW_DH ** -0.5 * LOG2E)
    w = jnp.concatenate([mq, mk * ML_DQK ** -0.5, mv, mg, sq, _rope_swap(sq), sk, _rope_swap(sk),
                         sv, _pad_cols(gates[:, _gate_perm()], 128), sg[:, perm]], axis=1).astype(BF16)
    half = ML_HEADS * ML_DV
    w_o = jnp.concatenate([w_out[:half], w_out[half:][perm]], axis=0).astype(BF16)
    return w, w_o


def _rope_tables():
    nf = SW_DH // 4
    freqs = ROPE_BASE ** (-jnp.arange(nf, dtype=F32) / nf)
    pos = jnp.arange(SEQ)
    rows = (pos // GRID_W).astype(F32)
    cols = (pos % GRID_W).astype(F32)
    ar = rows[:, None] * freqs[None, :]
    ac = cols[:, None] * freqs[None, :]
    cos = jnp.concatenate([jnp.cos(ar), jnp.cos(ar), jnp.cos(ac), jnp.cos(ac)], axis=1)
    sin = jnp.concatenate([-jnp.sin(ar), jnp.sin(ar), -jnp.sin(ac), jnp.sin(ac)], axis=1)
    cos = jnp.concatenate([jnp.ones((CTX_LEN, SW_DH), F32), cos], axis=0)
    sin = jnp.concatenate([jnp.zeros((CTX_LEN, SW_DH), F32), sin], axis=0)
    return jnp.tile(cos, (1, 2)), jnp.tile(sin, (1, 2))


def _gate_perm():
    return np.arange(4 * ML_HEADS).reshape(2, 2, 2, 2).transpose(2, 0, 1, 3).reshape(-1)


def _gate_rows(grow_t):
    bsz = grow_t.shape[0]
    g = grow_t.reshape(bsz, 2, 2, 2, 2, N_CHUNKS, CHUNK)
    return g.transpose(0, 2, 3, 1, 5, 4, 6).reshape(bsz, 2, 2, 2, N_CHUNKS, 2 * CHUNK)


def kernel(x, c, ctx, c_ctx, w_ada, b_ada, ln_g, ln_b, w_in_even, w_out_even, gla_gate_w, gla_gate_b,
           gla_norm_g, na_rpb, w_in_odd, w_out_odd, ml_gate_b, ml_norm_g, sw_sink):
    bsz = x.shape[0]
    mod_rows = 16
    cvec = jnp.zeros((mod_rows, D_MODEL), F32).at[:bsz].set(c).at[bsz].set(c_ctx)
    ada = _ada(cvec, w_ada, b_ada)
    xs = (ctx, x)
    cos_t, sin_t = _rope_tables()
    for l in range(DEPTH):
        i = l // 2
        last = l == DEPTH - 1
        lat = ada[l, :bsz].reshape(bsz, 1, 3, D_MODEL)
        cx = jnp.broadcast_to(ada[l, bsz].reshape(1, 1, 3, D_MODEL), (bsz, 1, 3, D_MODEL))
        mod = jnp.concatenate([lat, cx], axis=1)
        if l % 2 == 0:
            w, gw, gb = _prep_even(w_in_even[i], gla_gate_w[i], gla_gate_b[i])
            gq, gk, gv, gg, nq, nk, nv, ng, lr = _inproj_even(xs, mod, w)
            ya = _gla(gq, gk, gv, gg, lr, gw, gb, gla_norm_g[i].reshape(GLA_HEADS // 2, 1, 256))
            yb = _na(nq, nk, nv, ng, _rpb_table(na_rpb[i]))
            w_o = w_out_even[i].astype(BF16)
        else:
            w, w_o = _prep_odd(w_in_odd[i], w_out_odd[i])
            gate_b = _pad_cols(ml_gate_b[i].reshape(1, 4 * ML_HEADS)[:, _gate_perm()], 128)
            mq, mk, mv, mg, sq, sk, sv, sg, gcol, grow_t = _inproj_odd(xs[0], mod, w, cos_t, sin_t, gate_b)
            grow = _gate_rows(grow_t)
            ya = _mlstm(mq, mk, mv, mg, gcol, grow, ml_norm_g[i].reshape(ML_HEADS // 2, 1, 256))
            yb = _swa(sq, sk, sv, sg, sw_sink[i])
        xs = (_outproj(ya, yb, xs, mod, w_o, ln_g[l].reshape(1, D_MODEL), ln_b[l].reshape(1, D_MODEL), last),)
    return xs[0]
```
